```python
import math
import jax
import jax.numpy as jnp
from jax import lax
import numpy as np

D_MODEL = 1024
BATCH = 8
SEQ = 2048
DEPTH = 2

GRID_W = 64
CTX_LEN = 256
NORM_EPS = 1e-6

RW_HEAD_DIM = 64
RW_DIM = D_MODEL // 2
RW_HEADS = RW_DIM // RW_HEAD_DIM
RW_LORA_W = 32
RW_LORA_A = 32
RW_LORA_G = 96
RW_GN_EPS = 64e-5
RW_COLS = 3 * RW_DIM + 2 * RW_LORA_W + 2 * RW_LORA_A + RW_LORA_G
RW_SPLITS = (RW_DIM, 2 * RW_DIM, 3 * RW_DIM, 3 * RW_DIM + 2 * RW_LORA_W, 3 * RW_DIM + 2 * RW_LORA_W + 2 * RW_LORA_A)

MB_HEAD_DIM = 64
MB_DIM = D_MODEL // 2
MB_HEADS = MB_DIM // MB_HEAD_DIM
MB_GROUPS = 2
MB_STATE = 128
MB_CONV = 5
MB_CHUNK = 128
MB_XBC = MB_DIM + 2 * MB_GROUPS * MB_STATE
MB_COLS = MB_DIM + MB_XBC + 2 * MB_HEADS
MB_SPLITS = (MB_DIM, MB_DIM + MB_XBC)

MIX_IN = RW_COLS + MB_COLS
MIX_OUT = RW_DIM + MB_DIM

NA_HEAD_DIM = 64
NA_HEADS = D_MODEL // NA_HEAD_DIM
NA_KH = 8
NA_KW = 16
NA_QB = 16
NA_KB = 32

D_FF = 2816
N_EXPERTS = 8
TOP_K = 2

kernel_name = 'hybrid_rwkv7_mamba2_natten_moe_dit'


def _rms_norm(h, g):
    hf = h.astype(jnp.float32)
    hf = hf * lax.rsqrt(jnp.mean(hf * hf, axis=-1, keepdims=True) + NORM_EPS)
    return (hf * g.astype(jnp.float32)).astype(h.dtype)


def _modulate(h, shift, scale):
    return h * (1 + scale) + shift


def _heads(t, n_heads):
    return t.reshape(t.shape[:-1] + (n_heads, t.shape[-1] // n_heads))


def _swiglu(h, w1, w3, w2):
    return (jax.nn.silu(h @ w1) * (h @ w3)) @ w2


def _centred_shift(z, mu_prev, mu_next):
    z_prev = jnp.pad(z, ((0, 0), (1, 0), (0, 0)))[:, :-1]
    z_next = jnp.pad(z, ((0, 0), (0, 1), (0, 0)))[:, 1:]
    return z + mu_prev * (z_prev - z) + mu_next * (z_next - z)


def _centred_dwconv(x, w, b):
    k_w, ch = w.shape
    y = lax.conv_general_dilated(x, w[:, None, :].astype(x.dtype), window_strides=(1,),
                                 padding=[(k_w // 2, k_w // 2)],
                                 dimension_numbers=('NWC', 'WIO', 'NWC'), feature_group_count=ch)
    return y + b


def _rwkv7_scan(s0, r, w, k, v, a, b):
    def step(S, inp):
        r_t, w_t, k_t, v_t, a_t, b_t = inp
        sa = jnp.einsum('bhvk,bhk->bhv', S, a_t)
        S = S * w_t[:, :, None, :] + sa[..., None] * b_t[:, :, None, :] + v_t[..., None] * k_t[:, :, None, :]
        return S, jnp.einsum('bhvk,bhk->bhv', S, r_t)
    xs = tuple(jnp.moveaxis(t, 1, 0) for t in (r, w, k, v, a, b))
    s_fin, ys = lax.scan(step, s0, xs)
    return s_fin, jnp.moveaxis(ys, 0, 1)


def _segsum(x):
    n = x.shape[-1]
    cs = jnp.cumsum(x, axis=-1)
    return jnp.where(np.tril(np.ones((n, n), dtype=bool)), cs[..., :, None] - cs[..., None, :], -jnp.inf)


def _ssd_chunked(s0, X, A, Bm, Cm):
    b, t, nh, p = X.shape
    n = Bm.shape[-1]
    nc, cl = t // MB_CHUNK, MB_CHUNK
    X = X.reshape(b, nc, cl, nh, p)
    Bm = Bm.reshape(b, nc, cl, nh, n)
    Cm = Cm.reshape(b, nc, cl, nh, n)
    A = A.reshape(b, nc, cl, nh).transpose(0, 3, 1, 2)
    A_cs = jnp.cumsum(A, axis=-1)
    L = jnp.exp(_segsum(A))
    y_diag = jnp.einsum('bclhn,bcshn,bhcls,bcshp->bclhp', Cm, Bm, L, X)
    decay_states = jnp.exp(A_cs[..., -1:] - A_cs)
    states = jnp.einsum('bclhn,bhcl,bclhp->bchpn', Bm, decay_states, X)
    states = jnp.concatenate([s0[:, None], states], axis=1)
    chunk_decay = jnp.exp(_segsum(jnp.pad(A_cs[..., -1], ((0, 0), (0, 0), (1, 0)))))
    states = jnp.einsum('bhzc,bchpn->bzhpn', chunk_decay, states)
    y_off = jnp.einsum('bclhn,bchpn,bhcl->bclhp', Cm, states[:, :-1], jnp.exp(A_cs))
    return states[:, -1], (y_diag + y_off).reshape(b, t, nh, p)


def _prefix_scan(scan_fn, ctx_args, lat_args, s0, reverse):
    if reverse:
        ctx_args = tuple(jnp.flip(t, axis=1) for t in ctx_args)
        lat_args = tuple(jnp.flip(t, axis=1) for t in lat_args)
    s_ctx, y_ctx = scan_fn(s0, *ctx_args)
    _, y_lat = scan_fn(s_ctx, *lat_args)
    if reverse:
        y_ctx, y_lat = jnp.flip(y_ctx, axis=1), jnp.flip(y_lat, axis=1)
    return y_ctx, y_lat


def _rwkv_mamba_mixer(a_ctx, a_lat, w_in, w_out, rw_mu, rw_w0, rw_w_up, rw_a0, rw_a_up, rw_g_up,
                      rw_k_k, rw_k_a, rw_r_k, rw_gn_w, rw_gn_b, mb_conv_w, mb_conv_b,
                      mb_dt_bias, mb_a_log, mb_d, mb_norm_w):
    f32 = jnp.float32

    def prepare(h):
        b, t, _ = h.shape
        z = h @ w_in
        z_rw = _centred_shift(z[..., :RW_COLS], rw_mu[0], rw_mu[1])
        r, k, v, lw, la, lg = jnp.split(z_rw, RW_SPLITS, axis=-1)
        lw = lw.reshape(b, t, 2, RW_LORA_W)
        la = la.reshape(b, t, 2, RW_LORA_A)
        logw = (rw_w0 + jnp.einsum('btdl,dlc->btdc', jnp.tanh(lw), rw_w_up)).astype(f32)
        decay = jnp.exp(-jnp.exp(-jax.nn.softplus(-logw) - 0.5))
        iclr = jax.nn.sigmoid((rw_a0 + jnp.einsum('btdl,dlc->btdc', la, rw_a_up)).astype(f32))
        gate = jax.nn.sigmoid(lg) @ rw_g_up
        kk = _heads(k * rw_k_k, RW_HEADS).astype(f32)
        kk = kk / jnp.maximum(jnp.sqrt(jnp.sum(kk * kk, axis=-1, keepdims=True)), 1e-12)
        k_dir = _heads((k[:, :, None, :] * (1 + (iclr - 1) * rw_k_a)).astype(f32), RW_HEADS)
        rh = _heads(r.astype(f32), RW_HEADS)
        vh = _heads(v.astype(f32), RW_HEADS)
        dh = _heads(decay, RW_HEADS)
        ih = _heads(iclr, RW_HEADS)
        rw_args = [(rh, dh[:, :, d], k_dir[:, :, d], vh, -kk, kk * ih[:, :, d]) for d in range(2)]
        bonus_f = jnp.sum(rh * k_dir[:, :, 0] * rw_r_k[0].astype(f32), axis=-1, keepdims=True)
        bonus_b = jnp.sum(rh * k_dir[:, :, 1] * rw_r_k[1].astype(f32), axis=-1, keepdims=True)
        bonus = (bonus_f + bonus_b) * vh
        zg, xbc, dt_raw = jnp.split(z[..., RW_COLS:], MB_SPLITS, axis=-1)
        xbc = jax.nn.silu(_centred_dwconv(xbc, mb_conv_w, mb_conv_b))
        xm, bm, cm = jnp.split(xbc, (MB_DIM, MB_DIM + MB_GROUPS * MB_STATE), axis=-1)
        xm = _heads(xm.astype(f32), MB_HEADS)
        rep = MB_HEADS // MB_GROUPS
        bm = jnp.repeat(_heads(bm.astype(f32), MB_GROUPS), rep, axis=2)
        cm = jnp.repeat(_heads(cm.astype(f32), MB_GROUPS), rep, axis=2)
        dt = jax.nn.softplus(dt_raw.reshape(b, t, 2, MB_HEADS).astype(f32) + mb_dt_bias.astype(f32))
        a_neg = -jnp.exp(mb_a_log.astype(f32))
        mb_args = [(xm * dt[:, :, d, :, None], dt[:, :, d] * a_neg[d], bm, cm) for d in range(2)]
        return rw_args, mb_args, bonus, gate, xm, zg

    pc, pl = prepare(a_ctx), prepare(a_lat)
    bsz = a_lat.shape[0]
    s0_rw = jnp.zeros((bsz, RW_HEADS, RW_HEAD_DIM, RW_HEAD_DIM), f32)
    s0_mb = jnp.zeros((bsz, MB_HEADS, MB_HEAD_DIM, MB_STATE), f32)
    yrf_c, yrf_l = _prefix_scan(_rwkv7_scan, pc[0][0], pl[0][0], s0_rw, False)
    yrb_c, yrb_l = _prefix_scan(_rwkv7_scan, pc[0][1], pl[0][1], s0_rw, True)
    ymf_c, ymf_l = _prefix_scan(_ssd_chunked, pc[1][0], pl[1][0], s0_mb, False)
    ymb_c, ymb_l = _prefix_scan(_ssd_chunked, pc[1][1], pl[1][1], s0_mb, True)

    def finish(h, p, y_rw, y_mb):
        b, t, _ = h.shape
        _, _, bonus, gate, xm, zg = p
        mu = jnp.mean(y_rw, axis=-1, keepdims=True)
        var = jnp.mean(jnp.square(y_rw - mu), axis=-1, keepdims=True)
        y_rw = ((y_rw - mu) * lax.rsqrt(var + RW_GN_EPS)).reshape(b, t, RW_DIM) * rw_gn_w + rw_gn_b
        o_rw = (y_rw + bonus.reshape(b, t, RW_DIM)) * gate
        y_mb = (y_mb + mb_d.astype(f32)[:, None] * xm).reshape(b, t, MB_DIM) * jax.nn.silu(zg.astype(f32))
        y_mb = y_mb.reshape(b, t, MB_GROUPS, MB_DIM // MB_GROUPS)
        y_mb = (y_mb * lax.rsqrt(jnp.mean(y_mb * y_mb, axis=-1, keepdims=True) + NORM_EPS)).reshape(b, t, MB_DIM) * mb_norm_w
        return jnp.concatenate([o_rw, y_mb], axis=-1).astype(h.dtype) @ w_out

    return (finish(a_ctx, pc, yrf_c + yrb_c, ymf_c + ymb_c),
            finish(a_lat, pl, yrf_l + yrb_l, ymf_l + ymb_l))


def _neighbourhood_attention(a_ctx, a_lat, w_qkv, w_out, rpb, with_ctx_queries):
    b, t, d_model = a_lat.shape
    rows = t // GRID_W
    kh = min(NA_KH, rows)
    scale = NA_HEAD_DIM ** -0.5
    q, k, v = jnp.split(a_lat @ w_qkv, 3, axis=-1)
    qc, kc, vc = jnp.split(a_ctx @ w_qkv, 3, axis=-1)
    q_grid = (_heads(q, NA_HEADS) * scale).reshape(b, rows, GRID_W, NA_HEADS, NA_HEAD_DIM)
    k_grid = _heads(k, NA_HEADS).reshape(b, rows, GRID_W, NA_HEADS, NA_HEAD_DIM)
    v_grid = _heads(v, NA_HEADS).reshape(b, rows, GRID_W, NA_HEADS, NA_HEAD_DIM)
    qc, kc, vc = _heads(qc, NA_HEADS) * scale, _heads(kc, NA_HEADS), _heads(vc, NA_HEADS)

    n_cb = GRID_W // NA_QB
    q_cols = np.arange(GRID_W).reshape(n_cb, NA_QB)
    kb_start = np.clip(np.arange(n_cb) * NA_QB - NA_KW // 2, 0, GRID_W - NA_KB)
    key_cols = kb_start[:, None] + np.arange(NA_KB)
    win_start = np.clip(q_cols - NA_KW // 2, 0, GRID_W - NA_KW)
    kcol = key_cols[:, None, :]
    col_in = (kcol >= win_start[..., None]) & (kcol < win_start[..., None] + NA_KW)
    col_bias_idx = np.clip(kcol - q_cols[:, :, None] + NA_KW - 1, 0, 2 * NA_KW - 2)
    mask = np.broadcast_to(col_in[:, :, None, :], (n_cb, NA_QB, kh, NA_KB)).reshape(n_cb, NA_QB, kh * NA_KB)
    n_lat = kh * NA_KB

    def row_block(r):
        rs = jnp.clip(r - kh // 2, 0, rows - kh)
        k_rows = lax.dynamic_slice_in_dim(k_grid, rs, kh, axis=1)
        v_rows = lax.dynamic_slice_in_dim(v_grid, rs, kh, axis=1)
        k_blk = k_rows[:, :, key_cols].transpose(0, 2, 1, 3, 4, 5).reshape(b, n_cb, n_lat, NA_HEADS, NA_HEAD_DIM)
        v_blk = v_rows[:, :, key_cols].transpose(0, 2, 1, 3, 4, 5).reshape(b, n_cb, n_lat, NA_HEADS, NA_HEAD_DIM)
        q_blk = lax.dynamic_index_in_dim(q_grid, r, axis=1, keepdims=False).reshape(b, n_cb, NA_QB, NA_HEADS, NA_HEAD_DIM)
        row_idx = rs + jnp.arange(kh) - r + (NA_KH - 1)
        bias = rpb[:, row_idx][:, :, col_bias_idx]
        bias = bias.transpose(0, 2, 3, 1, 4).reshape(NA_HEADS, n_cb, NA_QB, n_lat).astype(jnp.float32)
        s_lat = jnp.einsum('bjqhd,bjkhd->bhjqk', q_blk, k_blk).astype(jnp.float32) + bias
        s_lat = jnp.where(mask, s_lat, -jnp.inf)
        s_ctx = jnp.einsum('bjqhd,bkhd->bhjqk', q_blk, kc).astype(jnp.float32)
        p = jax.nn.softmax(jnp.concatenate([s_lat, s_ctx], axis=-1), axis=-1).astype(v.dtype)
        o = (jnp.einsum('bhjqk,bjkhd->bjqhd', p[..., :n_lat], v_blk)
             + jnp.einsum('bhjqk,bkhd->bjqhd', p[..., n_lat:], vc))
        return o.reshape(b, GRID_W, d_model)

    o_rows = lax.map(row_block, jnp.arange(rows))
    o_lat = o_rows.transpose(1, 0, 2, 3).reshape(b, t, d_model) @ w_out
    o_ctx = None
    if with_ctx_queries:
        s = jnp.einsum('bqhd,bkhd->bhqk', qc, kc).astype(jnp.float32)
        p = jax.nn.softmax(s, axis=-1).astype(vc.dtype)
        o_ctx = jnp.einsum('bhqk,bkhd->bqhd', p, vc).reshape(a_ctx.shape) @ w_out
    return o_ctx, o_lat


def _moe_swiglu(h, router_w, router_b, w1, w3, w2):
    logits = (h @ router_w).astype(jnp.float32) + router_b.astype(jnp.float32)
    top_vals, top_idx = lax.top_k(logits, TOP_K)
    top_w = jax.nn.softmax(top_vals, axis=-1)
    gates = jnp.sum(jax.nn.one_hot(top_idx, N_EXPERTS, dtype=jnp.float32) * top_w[..., None], axis=-2).astype(h.dtype)
    out = jnp.zeros_like(h)
    for e in range(N_EXPERTS):
        out = out + gates[..., e:e + 1] * _swiglu(h, w1[e], w3[e], w2[e])
    return out


def setup_inputs(seed: int = 0) -> dict:
    key = jax.random.key(seed)
    keys = iter(jax.random.split(key, 64))
    d = D_MODEL
    ne, no = (DEPTH + 1) // 2, DEPTH // 2

    def nrm(shape, scale):
        return scale * jax.random.normal(next(keys), shape, jnp.float32)

    def uni(shape, lo, hi):
        return jax.random.uniform(next(keys), shape, jnp.float32, lo, hi)

    dt0 = jnp.exp(uni((ne, 2, MB_HEADS), math.log(1e-3), math.log(1e-1)))
    return {
        'x': nrm((BATCH, SEQ, d), 1.0),
        'c': nrm((BATCH, d), 1.0),
        'ctx': nrm((BATCH, CTX_LEN, d), 1.0),
        'c_ctx': nrm((d,), 1.0),
        'w_ada': nrm((DEPTH, d, 6 * d), 0.5 * d ** -0.5),
        'b_ada': nrm((DEPTH, 6 * d), 0.02),
        'norm_mix': 1.0 + nrm((DEPTH, d), 0.05),
        'norm_ffn': 1.0 + nrm((DEPTH, d), 0.05),
        'norm_final': 1.0 + nrm((d,), 0.05),
        'mix_w_in': nrm((ne, d, MIX_IN), d ** -0.5),
        'mix_w_out': nrm((ne, MIX_OUT, d), MIX_OUT ** -0.5),
        'rw_mu': uni((ne, 2, RW_COLS), 0.0, 0.5),
        'rw_w0': uni((ne, 2, RW_DIM), -6.0, 1.0),
        'rw_w_up': nrm((ne, 2, RW_LORA_W, RW_DIM), 0.5 * RW_LORA_W ** -0.5),
        'rw_a0': nrm((ne, 2, RW_DIM), 0.5),
        'rw_a_up': nrm((ne, 2, RW_LORA_A, RW_DIM), 0.5 * RW_LORA_A ** -0.5),
        'rw_g_up': nrm((ne, RW_LORA_G, RW_DIM), RW_LORA_G ** -0.5),
        'rw_k_k': 0.85 + nrm((ne, RW_DIM), 0.05),
        'rw_k_a': 1.0 + nrm((ne, RW_DIM), 0.05),
        'rw_r_k': nrm((ne, 2, RW_HEADS, RW_HEAD_DIM), 0.1),
        'rw_gn_w': 1.0 + nrm((ne, RW_DIM), 0.05),
        'rw_gn_b': nrm((ne, RW_DIM), 0.02),
        'mb_conv_w': nrm((ne, MB_CONV, MB_XBC), MB_CONV ** -0.5),
        'mb_conv_b': nrm((ne, MB_XBC), 0.02),
        'mb_dt_bias': dt0 + jnp.log(-jnp.expm1(-dt0)),
        'mb_a_log': jnp.log(uni((ne, 2, MB_HEADS), 1.0, 16.0)),
        'mb_d': 1.0 + nrm((ne, MB_HEADS), 0.1),
        'mb_norm_w': 1.0 + nrm((ne, MB_DIM), 0.05),
        'ffn_w1': nrm((ne, d, D_FF), d ** -0.5),
        'ffn_w3': nrm((ne, d, D_FF), d ** -0.5),
        'ffn_w2': nrm((ne, D_FF, d), D_FF ** -0.5),
        'na_w_qkv': nrm((no, d, 3 * d), d ** -0.5),
        'na_w_out': nrm((no, d, d), d ** -0.5),
        'na_rpb': nrm((no, NA_HEADS, 2 * NA_KH - 1, 2 * NA_KW - 1), 0.1),
        'moe_router_w': nrm((no, d, N_EXPERTS), d ** -0.5),
        'moe_router_b': nrm((no, N_EXPERTS), 0.01),
        'moe_w1': nrm((no, N_EXPERTS, d, D_FF), d ** -0.5),
        'moe_w3': nrm((no, N_EXPERTS, d, D_FF), d ** -0.5),
        'moe_w2': nrm((no, N_EXPERTS, D_FF, d), D_FF ** -0.5),
    }


def reference(x, c, ctx, c_ctx, w_ada, b_ada, norm_mix, norm_ffn, norm_final, mix_w_in, mix_w_out,
              rw_mu, rw_w0, rw_w_up, rw_a0, rw_a_up, rw_g_up, rw_k_k, rw_k_a, rw_r_k, rw_gn_w, rw_gn_b,
              mb_conv_w, mb_conv_b, mb_dt_bias, mb_a_log, mb_d, mb_norm_w, ffn_w1, ffn_w3, ffn_w2,
              na_w_qkv, na_w_out, na_rpb, moe_router_w, moe_router_b, moe_w1, moe_w3, moe_w2):
    h_ctx, h_lat = ctx, x
    c_lat_act, c_ctx_act = jax.nn.silu(c), jax.nn.silu(c_ctx)
    for i in range(DEPTH):
        j = i // 2
        update_ctx = i < DEPTH - 1
        m_lat = jnp.split((c_lat_act @ w_ada[i] + b_ada[i])[:, None, :], 6, axis=-1)
        m_ctx = jnp.split(c_ctx_act @ w_ada[i] + b_ada[i], 6, axis=-1)
        a_lat = _modulate(_rms_norm(h_lat, norm_mix[i]), m_lat[0], m_lat[1])
        a_ctx = _modulate(_rms_norm(h_ctx, norm_mix[i]), m_ctx[0], m_ctx[1])
        if i % 2 == 0:
            o_ctx, o_lat = _rwkv_mamba_mixer(a_ctx, a_lat, mix_w_in[j], mix_w_out[j], rw_mu[j], rw_w0[j],
                                             rw_w_up[j], rw_a0[j], rw_a_up[j], rw_g_up[j], rw_k_k[j],
                                             rw_k_a[j], rw_r_k[j], rw_gn_w[j], rw_gn_b[j], mb_conv_w[j],
                                             mb_conv_b[j], mb_dt_bias[j], mb_a_log[j], mb_d[j], mb_norm_w[j])
        else:
            o_ctx, o_lat = _neighbourhood_attention(a_ctx, a_lat, na_w_qkv[j], na_w_out[j], na_rpb[j], update_ctx)

        def channel_mixer(t):
            if i % 2 == 0:
                return _swiglu(t, ffn_w1[j], ffn_w3[j], ffn_w2[j])
            return _moe_swiglu(t, moe_router_w[j], moe_router_b[j], moe_w1[j], moe_w3[j], moe_w2[j])

        h_lat = h_lat + m_lat[2] * o_lat
        h_lat = h_lat + m_lat[5] * channel_mixer(_modulate(_rms_norm(h_lat, norm_ffn[i]), m_lat[3], m_lat[4]))
        if update_ctx:
            h_ctx = h_ctx + m_ctx[2] * o_ctx
            h_ctx = h_ctx + m_ctx[5] * channel_mixer(_modulate(_rms_norm(h_ctx, norm_ffn[i]), m_ctx[3], m_ctx[4]))
    return _rms_norm(h_lat, norm_final)
```

```python
import functools
import math

import numpy as np
import jax
import jax.numpy as jnp
from jax import lax
from jax.experimental import pallas as pl
from jax.experimental.pallas import tpu as pltpu

F32 = jnp.float32
BF16 = jnp.bfloat16

D = 1024
NORM_EPS = 1e-6
GRID_W = 64

HEAD = 64
RW_DIM = 512
RW_HEADS = 8
RW_LORA = 32
RW_LORA_G = 96
RW_GN_EPS = 64e-5
RW_COLS = 3 * RW_DIM + 4 * RW_LORA + RW_LORA_G
RW_PAD = 1792
RW_CHUNK = 64

MB_DIM = 512
MB_HEADS = 8
MB_GROUPS = 2
MB_STATE = 128
MB_CONV = 5
MB_XBC = MB_DIM + 2 * MB_GROUPS * MB_STATE
MB_PAD = 1792
MB_CHUNK = 128

MIX_PAD = RW_PAD + MB_PAD

NA_HEADS = 16
NA_KH = 8
NA_KW = 16

D_FF = 2816
N_EXPERTS = 8

TM = 256
HALO = 16
TN = 256
TF = 256
VMEM_LIMIT = 56 * 1024 * 1024


def _cparams(sem):
    return pltpu.CompilerParams(dimension_semantics=sem, vmem_limit_bytes=VMEM_LIMIT)


def _bdot(a, b):
    return jnp.dot(a.astype(BF16), b.astype(BF16), preferred_element_type=F32)


def _bdot_nt(a, b):
    return lax.dot_general(a.astype(BF16), b.astype(BF16), (((1,), (1,)), ((), ())),
                           preferred_element_type=F32)


def _hdot(a, b):
    return jnp.dot(a, b, precision=lax.Precision.HIGHEST, preferred_element_type=F32)


def _bmm(spec, a, b):
    return jnp.einsum(spec, a.astype(BF16), b.astype(BF16), preferred_element_type=F32)


def _sigmoid(x):
    return 1.0 / (1.0 + jnp.exp(-x))


def _silu(x):
    return x * _sigmoid(x)


def _softplus(x):
    return jnp.maximum(x, 0.0) + jnp.log(1.0 + jnp.exp(-jnp.abs(x)))


def _norm_mod(h, g, shift, scale):
    hn = h * lax.rsqrt(jnp.mean(h * h, axis=-1, keepdims=True) + NORM_EPS)
    return (hn * g) * (1.0 + scale) + shift


def _seg64_sum(x):
    outs = []
    for p in range(x.shape[-1] // 128):
        xp = x[:, 128 * p:128 * (p + 1)]
        lo = lax.broadcasted_iota(jnp.int32, xp.shape, 1) < HEAD
        s_lo = jnp.sum(jnp.where(lo, xp, 0.0), axis=-1, keepdims=True)
        s_hi = jnp.sum(jnp.where(lo, 0.0, xp), axis=-1, keepdims=True)
        outs.append(jnp.where(lo, s_lo, s_hi))
    return jnp.concatenate(outs, axis=-1)


def _heads(x):
    return jnp.stack([x[:, HEAD * h:HEAD * (h + 1)] for h in range(x.shape[-1] // HEAD)], axis=0)


def _unheads(x):
    return jnp.concatenate([x[h] for h in range(x.shape[0])], axis=-1)


def _ada_kernel(c_ref, w_ref, b_ref, o_ref):
    o_ref[0] = _hdot(_silu(c_ref[...]), w_ref[0]) + b_ref[0]


def _ada(cvec, w_ada, b_ada):
    depth = w_ada.shape[0]
    tn = 1536
    return pl.pallas_call(
        _ada_kernel,
        grid=(depth, 6 * D // tn),
        in_specs=[pl.BlockSpec((16, D), lambda l, j: (0, 0)),
                  pl.BlockSpec((1, D, tn), lambda l, j: (l, 0, j)),
                  pl.BlockSpec((1, 1, tn), lambda l, j: (l, 0, j))],
        out_specs=pl.BlockSpec((1, 16, tn), lambda l, j: (l, 0, j)),
        out_shape=jax.ShapeDtypeStruct((depth, 16, 6 * D), F32),
        compiler_params=_cparams(("arbitrary", "arbitrary")),
        name="adaln",
    )(cvec, w_ada, b_ada.reshape(depth, 1, 6 * D))


def _mod_row(i, nt, nl, nb):
    return jnp.where(i % nt >= nl, nb, i // nt)


def _inproj_kernel(hp_ref, h_ref, hn_ref, mod_ref, g_ref, w_ref, cf_ref, o_ref, a_scr, z_scr,
                   *, nt, nl, silu_lo, silu_hi):
    i = pl.program_id(0)
    j = pl.program_id(1)

    @pl.when(j == 0)
    def _():
        t = i % nt
        first = (t == 0) | (t == nl)
        last = (t == nl - 1) | (t == nt - 1)
        shift = mod_ref[0, :, 0:D]
        scale = mod_ref[0, :, D:2 * D]
        g = g_ref[...]
        a_scr[0:HALO] = jnp.where(first, 0.0, _norm_mod(hp_ref[...], g, shift, scale)).astype(BF16)
        a_scr[HALO:HALO + TM] = _norm_mod(h_ref[...], g, shift, scale).astype(BF16)
        a_scr[HALO + TM:] = jnp.where(last, 0.0, _norm_mod(hn_ref[...], g, shift, scale)).astype(BF16)

    z_scr[...] = jnp.dot(a_scr[...], w_ref[...], preferred_element_type=F32)
    acc = cf_ref[5:6, :] + cf_ref[0:1, :] * z_scr[HALO - 2:HALO - 2 + TM, :]
    for k in range(1, MB_CONV):
        acc = acc + cf_ref[k:k + 1, :] * z_scr[HALO - 2 + k:HALO - 2 + k + TM, :]
    is_silu = (j >= silu_lo) & (j < silu_hi)

    @pl.when(is_silu)
    def _():
        o_ref[...] = _silu(acc)

    @pl.when(jnp.logical_not(is_silu))
    def _():
        o_ref[...] = acc


def _inproj(h2, mods_l, g, w, cf, nb, nt, nl, silu_lo, silu_hi):
    rows = h2.shape[0]
    n = w.shape[1]
    hb = TM // HALO
    return pl.pallas_call(
        functools.partial(_inproj_kernel, nt=nt, nl=nl, silu_lo=silu_lo, silu_hi=silu_hi),
        grid=(rows // TM, n // TN),
        in_specs=[pl.BlockSpec((HALO, D), lambda i, j: (jnp.maximum(i * hb - 1, 0), 0)),
                  pl.BlockSpec((TM, D), lambda i, j: (i, 0)),
                  pl.BlockSpec((HALO, D), lambda i, j: (jnp.minimum((i + 1) * hb, rows // HALO - 1), 0)),
                  pl.BlockSpec((1, 1, 6 * D), lambda i, j: (_mod_row(i, nt, nl, nb), 0, 0)),
                  pl.BlockSpec((1, D), lambda i, j: (0, 0)),
                  pl.BlockSpec((D, TN), lambda i, j: (0, j)),
                  pl.BlockSpec((8, TN), lambda i, j: (0, j))],
        out_specs=pl.BlockSpec((TM, TN), lambda i, j: (i, j)),
        out_shape=jax.ShapeDtypeStruct((rows, n), F32),
        scratch_shapes=[pltpu.VMEM((TM + 2 * HALO, D), BF16), pltpu.VMEM((TM + 2 * HALO, TN), F32)],
        compiler_params=_cparams(("arbitrary", "arbitrary")),
        name="mix_inproj",
    )(h2, h2, h2, mods_l, g, w, cf)


def _nmm_kernel(h_ref, mod_ref, g_ref, w_ref, o_ref, a_scr):
    @pl.when(pl.program_id(1) == 0)
    def _():
        a_scr[...] = _norm_mod(h_ref[...], g_ref[...], mod_ref[0, :, 0:D], mod_ref[0, :, D:2 * D]).astype(BF16)

    o_ref[...] = jnp.dot(a_scr[...], w_ref[...], preferred_element_type=F32).astype(o_ref.dtype)


def _nmm(h2, mods_l, g, w, nb, nt, nl, out_dtype):
    rows = h2.shape[0]
    n = w.shape[1]
    return pl.pallas_call(
        _nmm_kernel,
        grid=(rows // TM, n // TN),
        in_specs=[pl.BlockSpec((TM, D), lambda i, j: (i, 0)),
                  pl.BlockSpec((1, 1, 6 * D), lambda i, j: (_mod_row(i, nt, nl, nb), 0, 0)),
                  pl.BlockSpec((1, D), lambda i, j: (0, 0)),
                  pl.BlockSpec((D, TN), lambda i, j: (0, j))],
        out_specs=pl.BlockSpec((TM, TN), lambda i, j: (i, j)),
        out_shape=jax.ShapeDtypeStruct((rows, n), out_dtype),
        scratch_shapes=[pltpu.VMEM((TM, D), BF16)],
        compiler_params=_cparams(("arbitrary", "arbitrary")),
        name="norm_proj",
    )(h2, mods_l, g, w)


def _rwkv_kernel(z_ref, w0_ref, wlw_ref, a0_ref, wla_ref, kk_ref, ka_ref, y_ref, s_scr):
    d = pl.program_id(1)

    @pl.when(pl.program_id(2) == 0)
    def _():
        s_scr[...] = jnp.zeros_like(s_scr)

    L = RW_CHUNK
    z = z_ref[0]
    r = z[:, 0:RW_DIM]
    k = z[:, RW_DIM:2 * RW_DIM]
    v = z[:, 2 * RW_DIM:3 * RW_DIM]
    lora = z[:, 3 * RW_DIM:3 * RW_DIM + 128]
    logw = w0_ref[0] + _hdot(jnp.tanh(lora), wlw_ref[0])
    logdec = -math.exp(-0.5) * _sigmoid(logw)
    iclr = _sigmoid(a0_ref[0] + _hdot(lora, wla_ref[0]))
    kk = k * kk_ref[...]
    kk = kk / jnp.maximum(jnp.sqrt(_seg64_sum(kk * kk)), 1e-12)
    kdir = k * (1.0 + (iclr - 1.0) * ka_ref[...])
    bvec = kk * iclr

    row = lax.broadcasted_iota(jnp.int32, (L, L), 0)
    col = lax.broadcasted_iota(jnp.int32, (L, L), 1)
    ahead = (row - col) * (1 - 2 * d)
    incl = ahead >= 0
    strict = ahead > 0
    lc = _hdot(jnp.where(incl, 1.0, 0.0), logdec)
    ltot = jnp.sum(logdec, axis=0, keepdims=True)
    g_in = jnp.exp(lc)
    g_ex = jnp.exp(lc - logdec)
    g_inv = jnp.exp(-lc)
    g_tail = jnp.exp(ltot - lc)

    ar = _heads(jnp.concatenate([-kk * g_ex, r * g_in], axis=0))
    bk = _heads(jnp.concatenate([bvec * g_inv, kdir * g_inv], axis=0))
    bk_tail = _heads(jnp.concatenate([bvec * g_tail, kdir * g_tail], axis=0))
    vh = _heads(v)
    s0 = s_scr[...]

    m1 = _bmm('hlk,hsk->hls', ar, bk)
    nmat = jnp.where(strict, m1[:, :L, :L], 0.0)
    a_ak = jnp.where(strict, m1[:, :L, L:], 0.0)
    m_r = jnp.concatenate([jnp.where(incl, m1[:, L:, :L], 0.0), jnp.where(incl, m1[:, L:, L:], 0.0)], axis=2)
    m2 = _bmm('hlk,hvk->hlv', ar, s0)
    x = m2[:, :L] + _bmm('hls,hsv->hlv', a_ak, vh)
    p = nmat
    steps = int(math.log2(L))
    for i in range(steps):
        x = x + _bmm('hls,hsv->hlv', p, x)
        if i < steps - 1:
            p = _bmm('hls,hst->hlt', p, p)
    uv = jnp.concatenate([x, vh], axis=1)
    y = m2[:, L:] + _bmm('hls,hsv->hlv', m_r, uv)
    y_ref[0, 0] = _unheads(y)
    g_tot = _heads(jnp.exp(ltot))
    s_scr[...] = s0 * g_tot + _bmm('hvl,hlk->hvk', jnp.swapaxes(uv, 1, 2), bk_tail)


def _rwkv_scan(z3, w0, wlw, a0, wla, kkw, kaw, seq):
    nb, t, _ = z3.shape
    nc = t // RW_CHUNK
    ncl = seq // RW_CHUNK

    def chunk(d, c):
        return jnp.where(d == 0, (c + ncl) % nc, nc - 1 - c)

    return pl.pallas_call(
        _rwkv_kernel,
        grid=(nb, 2, nc),
        in_specs=[pl.BlockSpec((1, RW_CHUNK, RW_PAD), lambda b, d, c: (b, chunk(d, c), 0)),
                  pl.BlockSpec((1, 1, RW_DIM), lambda b, d, c: (d, 0, 0)),
                  pl.BlockSpec((1, 128, RW_DIM), lambda b, d, c: (d, 0, 0)),
                  pl.BlockSpec((1, 1, RW_DIM), lambda b, d, c: (d, 0, 0)),
                  pl.BlockSpec((1, 128, RW_DIM), lambda b, d, c: (d, 0, 0)),
                  pl.BlockSpec((1, RW_DIM), lambda b, d, c: (0, 0)),
                  pl.BlockSpec((1, RW_DIM), lambda b, d, c: (0, 0))],
        out_specs=pl.BlockSpec((1, 1, RW_CHUNK, RW_DIM), lambda b, d, c: (d, b, chunk(d, c), 0)),
        out_shape=jax.ShapeDtypeStruct((2, nb, t, RW_DIM), F32),
        scratch_shapes=[pltpu.VMEM((RW_HEADS, HEAD, HEAD), F32)],
        compiler_params=_cparams(("arbitrary", "arbitrary", "arbitrary")),
        name="rwkv7_scan",
    )(z3, w0, wlw, a0, wla, kkw, kaw)


def _ssd_kernel(z_ref, dtb_ref, alog_ref, y_ref, s_scr):
    d = pl.program_id(1)

    @pl.when(pl.program_id(2) == 0)
    def _():
        s_scr[...] = jnp.zeros_like(s_scr)

    L = MB_CHUNK
    z = z_ref[0]
    xm = z[:, 0:MB_DIM]
    dt_all = _softplus(z[:, 1536:1664] + dtb_ref[...])
    a_all = dt_all * (-jnp.exp(alog_ref[...]))
    row = lax.broadcasted_iota(jnp.int32, (L, L), 0)
    col = lax.broadcasted_iota(jnp.int32, (L, L), 1)
    incl = (row - col) * (1 - 2 * d) >= 0
    cs = _hdot(jnp.where(incl, 1.0, 0.0), a_all)
    cs_t = cs.T
    tot = jnp.sum(a_all, axis=0, keepdims=True)
    fwd = d == 0
    gmat = []
    for g in range(MB_GROUPS):
        bg = z[:, 512 + 128 * g:640 + 128 * g]
        cg = z[:, 768 + 128 * g:896 + 128 * g]
        gmat.append((bg, cg, _bdot_nt(cg, bg)))
    outs = []
    for h in range(MB_HEADS):
        bg, cg, cb = gmat[h // (MB_HEADS // MB_GROUPS)]
        cs_col = jnp.where(fwd, cs[:, h:h + 1], cs[:, 8 + h:9 + h])
        cs_row = jnp.where(fwd, cs_t[h:h + 1, :], cs_t[8 + h:9 + h, :])
        dt_col = jnp.where(fwd, dt_all[:, h:h + 1], dt_all[:, 8 + h:9 + h])
        tot_h = jnp.where(fwd, tot[:, h:h + 1], tot[:, 8 + h:9 + h])
        lmat = jnp.exp(jnp.where(incl, cs_col - cs_row, -jnp.inf))
        xh = xm[:, HEAD * h:HEAD * (h + 1)] * dt_col
        s0 = s_scr[h]
        y = _bdot(cb * lmat, xh) + jnp.exp(cs_col) * _bdot_nt(cg, s0)
        outs.append(y)
        xd = xh * jnp.exp(tot_h - cs_col)
        s_scr[h] = s0 * jnp.exp(tot_h) + _bdot(xd.T, bg)
    y_ref[0, 0] = jnp.concatenate(outs, axis=-1)


def _ssd_scan(z3, dtb, alog, seq):
    nb, t, _ = z3.shape
    nc = t // MB_CHUNK
    ncl = seq // MB_CHUNK

    def chunk(d, c):
        return jnp.where(d == 0, (c + ncl) % nc, nc - 1 - c)

    return pl.pallas_call(
        _ssd_kernel,
        grid=(nb, 2, nc),
        in_specs=[pl.BlockSpec((1, MB_CHUNK, MB_PAD), lambda b, d, c: (b, chunk(d, c), 1)),
                  pl.BlockSpec((1, 128), lambda b, d, c: (0, 0)),
                  pl.BlockSpec((1, 128), lambda b, d, c: (0, 0))],
        out_specs=pl.BlockSpec((1, 1, MB_CHUNK, MB_DIM), lambda b, d, c: (d, b, chunk(d, c), 0)),
        out_shape=jax.ShapeDtypeStruct((2, nb, t, MB_DIM), F32),
        scratch_shapes=[pltpu.VMEM((MB_HEADS, HEAD, MB_STATE), F32)],
        compiler_params=_cparams(("arbitrary", "arbitrary", "arbitrary")),
        name="ssd_scan",
    )(z3, dtb, alog)


def _finish_kernel(zr_ref, zm_ref, yr_ref, ym_ref, h_ref, mod_ref, a0_ref, wla_ref, ka_ref, rk_ref, gup_ref,
                   gnw_ref, gnb_ref, mbd_ref, mbn_ref, wout_ref, o_ref):
    zr = zr_ref[...]
    r = zr[:, 0:RW_DIM]
    k = zr[:, RW_DIM:2 * RW_DIM]
    v = zr[:, 2 * RW_DIM:3 * RW_DIM]
    lora = zr[:, 3 * RW_DIM:3 * RW_DIM + 128]
    gate = _bdot(_sigmoid(zr[:, 3 * RW_DIM + 128:3 * RW_DIM + 256]), gup_ref[...])
    bonus = jnp.zeros_like(r)
    for dd in range(2):
        iclr = _sigmoid(a0_ref[dd] + _hdot(lora, wla_ref[dd]))
        bonus = bonus + r * (k * (1.0 + (iclr - 1.0) * ka_ref[...])) * rk_ref[dd]
    bonus = _seg64_sum(bonus) * v
    y = yr_ref[0] + yr_ref[1]
    mu = _seg64_sum(y) * (1.0 / HEAD)
    yc = y - mu
    var = _seg64_sum(yc * yc) * (1.0 / HEAD)
    y = yc * lax.rsqrt(var + RW_GN_EPS) * gnw_ref[...] + gnb_ref[...]
    o_rw = (y + bonus) * gate

    zm = zm_ref[...]
    xm = zm[:, 0:MB_DIM]
    zg = zm[:, 1024:1536]
    ym = (ym_ref[0] + ym_ref[1] + mbd_ref[...] * xm) * _silu(zg)
    gw = MB_DIM // MB_GROUPS
    parts = []
    for g in range(MB_GROUPS):
        yg = ym[:, gw * g:gw * (g + 1)]
        parts.append(yg * lax.rsqrt(jnp.mean(yg * yg, axis=-1, keepdims=True) + NORM_EPS))
    o_mb = jnp.concatenate(parts, axis=-1) * mbn_ref[...]
    o = jnp.concatenate([o_rw, o_mb], axis=-1)
    o_ref[...] = h_ref[...] + mod_ref[0, :, 2 * D:3 * D] * _bdot(o, wout_ref[...])


def _finish(z2, yrw, ymb, h2, mods_l, a0, wla, kaw, rk, gup, gnw, gnb, mbd, mbn, wout, nb, nt, nl):
    rows = h2.shape[0]
    full = lambda *shape: pl.BlockSpec(shape, lambda i: (0,) * len(shape))
    return pl.pallas_call(
        _finish_kernel,
        grid=(rows // TM,),
        in_specs=[pl.BlockSpec((TM, RW_PAD), lambda i: (i, 0)),
                  pl.BlockSpec((TM, MB_PAD), lambda i: (i, 1)),
                  pl.BlockSpec((2, TM, RW_DIM), lambda i: (0, i, 0)),
                  pl.BlockSpec((2, TM, MB_DIM), lambda i: (0, i, 0)),
                  pl.BlockSpec((TM, D), lambda i: (i, 0)),
                  pl.BlockSpec((1, 1, 6 * D), lambda i: (_mod_row(i, nt, nl, nb), 0, 0)),
                  full(2, 1, RW_DIM), full(2, 128, RW_DIM), full(1, RW_DIM), full(2, 1, RW_DIM),
                  full(128, RW_DIM), full(1, RW_DIM), full(1, RW_DIM), full(1, MB_DIM), full(1, MB_DIM),
                  full(D, D)],
        out_specs=pl.BlockSpec((TM, D), lambda i: (i, 0)),
        out_shape=jax.ShapeDtypeStruct((rows, D), F32),
        compiler_params=_cparams(("arbitrary",)),
        name="mix_finish",
    )(z2, z2, yrw, ymb, h2, mods_l, a0, wla, kaw, rk, gup, gnw, gnb, mbd, mbn, wout)


def _ffn_kernel(h_ref, mod_ref, g_ref, w1_ref, w3_ref, w2_ref, o_ref, a_scr, acc_scr):
    f = pl.program_id(1)

    @pl.when(f == 0)
    def _():
        a_scr[...] = _norm_mod(h_ref[...], g_ref[...], mod_ref[0, :, 3 * D:4 * D],
                               mod_ref[0, :, 4 * D:5 * D]).astype(BF16)
        acc_scr[...] = jnp.zeros_like(acc_scr)

    a = a_scr[...]
    hid = _silu(jnp.dot(a, w1_ref[...], preferred_element_type=F32)) * jnp.dot(a, w3_ref[...],
                                                                               preferred_element_type=F32)
    acc_scr[...] += jnp.dot(hid.astype(BF16), w2_ref[...], preferred_element_type=F32)

    @pl.when(f == pl.num_programs(1) - 1)
    def _():
        o_ref[...] = h_ref[...] + mod_ref[0, :, 5 * D:6 * D] * acc_scr[...]


def _ffn(h2, mods_l, g, w1, w3, w2, nb, nt, nl):
    rows = h2.shape[0]
    return pl.pallas_call(
        _ffn_kernel,
        grid=(rows // TM, D_FF // TF),
        in_specs=[pl.BlockSpec((TM, D), lambda i, f: (i, 0)),
                  pl.BlockSpec((1, 1, 6 * D), lambda i, f: (_mod_row(i, nt, nl, nb), 0, 0)),
                  pl.BlockSpec((1, D), lambda i, f: (0, 0)),
                  pl.BlockSpec((D, TF), lambda i, f: (0, f)),
                  pl.BlockSpec((D, TF), lambda i, f: (0, f)),
                  pl.BlockSpec((TF, D), lambda i, f: (f, 0))],
        out_specs=pl.BlockSpec((TM, D), lambda i, f: (i, 0)),
        out_shape=jax.ShapeDtypeStruct((rows, D), F32),
        scratch_shapes=[pltpu.VMEM((TM, D), BF16), pltpu.VMEM((TM, D), F32)],
        compiler_params=_cparams(("arbitrary", "arbitrary")),
        name="ffn_swiglu",
    )(h2, mods_l, g, w1, w3, w2)


def _attn_kernel(q_ref, k_ref, v_ref, kc_ref, vc_ref, bias_ref, o_ref, *, n_rows):
    r = pl.program_id(1)
    rs = jnp.clip(r - NA_KH // 2, 0, n_rows - NA_KH)
    start = pl.multiple_of(rs * GRID_W, GRID_W)
    nk = NA_KH * GRID_W
    for h in range(NA_HEADS):
        sl = slice(HEAD * h, HEAD * (h + 1))
        q = q_ref[0, :, sl]
        kh = k_ref[0, pl.ds(start, nk), sl]
        vh = v_ref[0, pl.ds(start, nk), sl]
        s_lat = _bdot_nt(q, kh) * (HEAD ** -0.5) + bias_ref[0, h]
        s_ctx = _bdot_nt(q, kc_ref[0, :, sl]) * (HEAD ** -0.5)
        m = jnp.maximum(jnp.max(s_lat, axis=-1, keepdims=True), jnp.max(s_ctx, axis=-1, keepdims=True))
        p_lat = jnp.exp(s_lat - m)
        p_ctx = jnp.exp(s_ctx - m)
        den = jnp.sum(p_lat, axis=-1, keepdims=True) + jnp.sum(p_ctx, axis=-1, keepdims=True)
        o = _bdot(p_lat, vh) + _bdot(p_ctx, vc_ref[0, :, sl])
        o_ref[0, :, sl] = (o / den).astype(o_ref.dtype)


def _attention(qkv, bias, seq, ctx_len):
    nb = qkv.shape[0]
    n_rows = seq // GRID_W
    nk = NA_KH * GRID_W

    def cfg(b, r):
        return r - jnp.clip(r - NA_KH // 2, 0, n_rows - NA_KH)

    return pl.pallas_call(
        functools.partial(_attn_kernel, n_rows=n_rows),
        grid=(nb, n_rows),
        in_specs=[pl.BlockSpec((1, GRID_W, D), lambda b, r: (b, r, 0)),
                  pl.BlockSpec((1, seq, D), lambda b, r: (b, 0, 1)),
                  pl.BlockSpec((1, seq, D), lambda b, r: (b, 0, 2)),
                  pl.BlockSpec((1, ctx_len, D), lambda b, r: (b, seq // ctx_len, 1)),
                  pl.BlockSpec((1, ctx_len, D), lambda b, r: (b, seq // ctx_len, 2)),
                  pl.BlockSpec((1, NA_HEADS, GRID_W, nk), lambda b, r: (cfg(b, r), 0, 0, 0))],
        out_specs=pl.BlockSpec((1, GRID_W, D), lambda b, r: (b, r, 0)),
        out_shape=jax.ShapeDtypeStruct((nb, seq, D), BF16),
        compiler_params=_cparams(("arbitrary", "arbitrary")),
        name="nbr_attention",
    )(qkv, qkv, qkv, qkv, qkv, bias)


def _attn_bias_table(rpb):
    q = np.arange(GRID_W)[:, None]
    c = np.arange(GRID_W)[None, :]
    ws = np.clip(q - NA_KW // 2, 0, GRID_W - NA_KW)
    inside = (c >= ws) & (c < ws + NA_KW)
    cidx = np.clip(c - q + NA_KW - 1, 0, 2 * NA_KW - 2)
    ridx = np.arange(NA_KH)[None, :] + (NA_KH - 1) - np.arange(NA_KH)[:, None]
    tab = rpb[:, ridx][:, :, :, cidx]
    tab = jnp.where(inside[None, None, None], tab.astype(F32), -jnp.inf)
    tab = tab.transpose(1, 0, 3, 2, 4)
    return tab.reshape(NA_KH, NA_HEADS, GRID_W, NA_KH * GRID_W)


def _proj_res_kernel(o_ref, h_ref, mod_ref, w_ref, out_ref):
    out_ref[0] = h_ref[0] + mod_ref[0, :, 2 * D:3 * D] * jnp.dot(o_ref[0], w_ref[...],
                                                                 preferred_element_type=F32)


def _proj_res(o3, h3, mods_l, w, seq):
    nb = o3.shape[0]
    tm = 512
    return pl.pallas_call(
        _proj_res_kernel,
        grid=(nb, seq // tm),
        in_specs=[pl.BlockSpec((1, tm, D), lambda b, i: (b, i, 0)),
                  pl.BlockSpec((1, tm, D), lambda b, i: (b, i, 0)),
                  pl.BlockSpec((1, 1, 6 * D), lambda b, i: (b, 0, 0)),
                  pl.BlockSpec((D, D), lambda b, i: (0, 0))],
        out_specs=pl.BlockSpec((1, tm, D), lambda b, i: (b, i, 0)),
        out_shape=jax.ShapeDtypeStruct((nb, seq, D), F32),
        compiler_params=_cparams(("arbitrary", "arbitrary")),
        name="attn_outproj",
    )(o3, h3, mods_l, w)


def _moe_kernel(h_ref, mod_ref, g_ref, rw_ref, rb_ref, w1_ref, w3_ref, w2_ref, gf_ref, o_ref,
                a_scr, gate_scr, acc_scr):
    e = pl.program_id(2)
    f = pl.program_id(3)

    @pl.when((e == 0) & (f == 0))
    def _():
        a = _norm_mod(h_ref[0], g_ref[...], mod_ref[0, :, 3 * D:4 * D], mod_ref[0, :, 4 * D:5 * D])
        a_scr[...] = a.astype(BF16)
        logits = _hdot(a, rw_ref[...]) + rb_ref[...]
        lane = lax.broadcasted_iota(jnp.int32, logits.shape, 1)
        logits = jnp.where(lane < N_EXPERTS, logits, -jnp.inf)
        m1 = jnp.max(logits, axis=-1, keepdims=True)
        i1 = jnp.min(jnp.where(logits == m1, lane, 128), axis=-1, keepdims=True)
        rest = jnp.where(lane == i1, -jnp.inf, logits)
        m2 = jnp.max(rest, axis=-1, keepdims=True)
        i2 = jnp.min(jnp.where(rest == m2, lane, 128), axis=-1, keepdims=True)
        ex = jnp.exp(m2 - m1)
        gate_scr[...] = jnp.where(lane == i1, 1.0 / (1.0 + ex), 0.0) + jnp.where(lane == i2, ex / (1.0 + ex), 0.0)
        acc_scr[...] = jnp.zeros_like(acc_scr)

    a = a_scr[...]
    gates = gate_scr[...]
    lane = lax.broadcasted_iota(jnp.int32, gates.shape, 1)
    ge = jnp.sum(jnp.where(lane == e, gates, 0.0), axis=-1, keepdims=True)
    hid = _silu(jnp.dot(a, w1_ref[0], preferred_element_type=F32)) * jnp.dot(a, w3_ref[0],
                                                                             preferred_element_type=F32)
    acc_scr[...] += jnp.dot((hid * ge).astype(BF16), w2_ref[0], preferred_element_type=F32)

    @pl.when((e == pl.num_programs(2) - 1) & (f == pl.num_programs(3) - 1))
    def _():
        out = h_ref[0] + mod_ref[0, :, 5 * D:6 * D] * acc_scr[...]
        out = out * lax.rsqrt(jnp.mean(out * out, axis=-1, keepdims=True) + NORM_EPS)
        o_ref[0] = out * gf_ref[...]


def _moe(h3, mods_l, g, rw, rb, w1, w3, w2, gf):
    nb, seq, _ = h3.shape
    tm = 512
    return pl.pallas_call(
        _moe_kernel,
        grid=(nb, seq // tm, N_EXPERTS, D_FF // TF),
        in_specs=[pl.BlockSpec((1, tm, D), lambda b, i, e, f: (b, i, 0)),
                  pl.BlockSpec((1, 1, 6 * D), lambda b, i, e, f: (b, 0, 0)),
                  pl.BlockSpec((1, D), lambda b, i, e, f: (0, 0)),
                  pl.BlockSpec((D, 128), lambda b, i, e, f: (0, 0)),
                  pl.BlockSpec((1, 128), lambda b, i, e, f: (0, 0)),
                  pl.BlockSpec((1, D, TF), lambda b, i, e, f: (e, 0, f)),
                  pl.BlockSpec((1, D, TF), lambda b, i, e, f: (e, 0, f)),
                  pl.BlockSpec((1, TF, D), lambda b, i, e, f: (e, f, 0)),
                  pl.BlockSpec((1, D), lambda b, i, e, f: (0, 0))],
        out_specs=pl.BlockSpec((1, tm, D), lambda b, i, e, f: (b, i, 0)),
        out_shape=jax.ShapeDtypeStruct((nb, seq, D), F32),
        scratch_shapes=[pltpu.VMEM((tm, D), BF16), pltpu.VMEM((tm, 128), F32), pltpu.VMEM((tm, D), F32)],
        compiler_params=_cparams(("arbitrary", "arbitrary", "arbitrary", "arbitrary")),
        name="moe_final",
    )(h3, mods_l, g, rw, rb, w1, w3, w2, gf)


def _pad_cols(a, n):
    return jnp.pad(a, [(0, 0)] * (a.ndim - 1) + [(0, n - a.shape[-1])])


def _mix_in_layout(w_in, rw_mu, conv_w, conv_b):
    w_rw = _pad_cols(w_in[:, :RW_COLS], RW_PAD)
    mb = w_in[:, RW_COLS:]
    w_mb = _pad_cols(jnp.concatenate([mb[:, MB_DIM:MB_DIM + MB_XBC], mb[:, :MB_DIM], mb[:, MB_DIM + MB_XBC:]],
                                     axis=1), MB_PAD)
    w = jnp.concatenate([w_rw, w_mb], axis=1).astype(BF16)
    mu_p = _pad_cols(rw_mu[0], RW_PAD)
    mu_n = _pad_cols(rw_mu[1], RW_PAD)
    zeros = jnp.zeros((RW_PAD,), F32)
    cf_rw = jnp.stack([zeros, mu_p, 1.0 - mu_p - mu_n, mu_n, zeros, zeros, zeros, zeros])
    ident = jnp.zeros((8, MB_PAD - MB_XBC), F32).at[2].set(1.0)
    cf_xbc = jnp.concatenate([conv_w, conv_b[None], jnp.zeros((2, MB_XBC), F32)], axis=0)
    cf = jnp.concatenate([cf_rw, cf_xbc, ident], axis=1)
    return w, cf


def _lora_pad(up, offset):
    out = jnp.zeros((2, 128, up.shape[-1]), F32)
    for d in range(2):
        out = out.at[d, offset + RW_LORA * d:offset + RW_LORA * (d + 1)].set(up[d])
    return out


def kernel(x, c, ctx, c_ctx, w_ada, b_ada, norm_mix, norm_ffn, norm_final, mix_w_in, mix_w_out, rw_mu, rw_w0,
           rw_w_up, rw_a0, rw_a_up, rw_g_up, rw_k_k, rw_k_a, rw_r_k, rw_gn_w, rw_gn_b, mb_conv_w, mb_conv_b,
           mb_dt_bias, mb_a_log, mb_d, mb_norm_w, ffn_w1, ffn_w3, ffn_w2, na_w_qkv, na_w_out, na_rpb,
           moe_router_w, moe_router_b, moe_w1, moe_w3, moe_w2):
    nb, seq, _ = x.shape
    ctx_len = ctx.shape[1]
    t = seq + ctx_len
    nt, nl = t // TM, seq // TM
    assert seq % TM == 0 and ctx_len % TM == 0 and seq % ctx_len == 0 and nb < 16

    cvec = jnp.zeros((16, D), F32).at[:nb].set(c).at[nb].set(c_ctx)
    mods = _ada(cvec, w_ada, b_ada).reshape(w_ada.shape[0], 16, 1, 6 * D)

    h = jnp.concatenate([x, ctx], axis=1).reshape(nb * t, D)

    w_in, cf = _mix_in_layout(mix_w_in[0], rw_mu[0], mb_conv_w[0], mb_conv_b[0])
    z = _inproj(h, mods[0], norm_mix[0][None], w_in, cf, nb, nt, nl, RW_PAD // TN, (RW_PAD + MB_XBC) // TN)
    z3 = z.reshape(nb, t, MIX_PAD)
    wlw = _lora_pad(rw_w_up[0], 0)
    wla = _lora_pad(rw_a_up[0], 2 * RW_LORA)
    w0 = rw_w0[0][:, None, :]
    a0 = rw_a0[0][:, None, :]
    yrw = _rwkv_scan(z3, w0, wlw, a0, wla, rw_k_k[0][None], rw_k_a[0][None], seq)
    dtb = _pad_cols(mb_dt_bias[0].reshape(1, 2 * MB_HEADS), 128)
    alog = _pad_cols(mb_a_log[0].reshape(1, 2 * MB_HEADS), 128)
    ymb = _ssd_scan(z3, dtb, alog, seq)
    gup = jnp.pad(rw_g_up[0], ((0, 128 - RW_LORA_G), (0, 0))).astype(BF16)
    h = _finish(z, yrw.reshape(2, nb * t, RW_DIM), ymb.reshape(2, nb * t, MB_DIM), h, mods[0], a0, wla,
                rw_k_a[0][None], rw_r_k[0].reshape(2, 1, RW_DIM), gup, rw_gn_w[0][None], rw_gn_b[0][None],
                jnp.repeat(mb_d[0], HEAD)[None], mb_norm_w[0][None], mix_w_out[0].astype(BF16), nb, nt, nl)
    h = _ffn(h, mods[0], norm_ffn[0][None], ffn_w1[0].astype(BF16), ffn_w3[0].astype(BF16),
             ffn_w2[0].astype(BF16), nb, nt, nl)

    qkv = _nmm(h, mods[1], norm_mix[1][None], na_w_qkv[0].astype(BF16), nb, nt, nl, BF16)
    o = _attention(qkv.reshape(nb, t, 3 * D), _attn_bias_table(na_rpb[0]), seq, ctx_len)
    mods1 = mods[1].reshape(16, 1, 6 * D)
    h3 = _proj_res(o, h.reshape(nb, t, D), mods1, na_w_out[0].astype(BF16), seq)
    rw = _pad_cols(moe_router_w[0], 128)
    rb = _pad_cols(moe_router_b[0][None], 128)
    return _moe(h3, mods1, norm_ffn[1][None], rw, rb, moe_w1[0].astype(BF16), moe_w3[0].astype(BF16),
                moe_w2[0].astype(BF16), norm_final[None])
```

```python
import functools
import math

import numpy as np
import jax
import jax.numpy as jnp
from jax import lax
from jax.experimental import pallas as pl
from jax.experimental.pallas import tpu as pltpu

F32 = jnp.float32
BF16 = jnp.bfloat16

D = 1024
NORM_EPS = 1e-6
GRID_W = 64

HEAD = 64
RW_DIM = 512
RW_HEADS = 8
RW_LORA = 32
RW_LORA_G = 96
RW_GN_EPS = 64e-5
RW_COLS = 3 * RW_DIM + 4 * RW_LORA + RW_LORA_G
RW_PAD = 1792
RW_CHUNK = 64

MB_DIM = 512
MB_HEADS = 8
MB_GROUPS = 2
MB_STATE = 128
MB_CONV = 5
MB_XBC = MB_DIM + 2 * MB_GROUPS * MB_STATE
MB_PAD = 1792
MB_CHUNK = 128

MIX_PAD = RW_PAD + MB_PAD

NA_HEADS = 16
NA_KH = 8
NA_KW = 16

D_FF = 2816
N_EXPERTS = 8

TM = 256
HALO = 16
TN = 256
TF = 256
VMEM_LIMIT = 56 * 1024 * 1024


def _cparams(sem):
    return pltpu.CompilerParams(dimension_semantics=sem, vmem_limit_bytes=VMEM_LIMIT)


def _bdot(a, b):
    return jnp.dot(a.astype(BF16), b.astype(BF16), preferred_element_type=F32)


def _bdot_nt(a, b):
    return lax.dot_general(a.astype(BF16), b.astype(BF16), (((1,), (1,)), ((), ())),
                           preferred_element_type=F32)


def _hdot(a, b):
    return jnp.dot(a, b, precision=lax.Precision.HIGHEST, preferred_element_type=F32)


def _bmm(spec, a, b):
    return jnp.einsum(spec, a.astype(BF16), b.astype(BF16), preferred_element_type=F32)


def _sigmoid(x):
    return 1.0 / (1.0 + jnp.exp(-x))


def _silu(x):
    return x * _sigmoid(x)


def _softplus(x):
    return jnp.maximum(x, 0.0) + jnp.log(1.0 + jnp.exp(-jnp.abs(x)))


def _norm_mod(h, g, shift, scale):
    hn = h * lax.rsqrt(jnp.mean(h * h, axis=-1, keepdims=True) + NORM_EPS)
    return (hn * g) * (1.0 + scale) + shift


def _seg64_sum(x):
    outs = []
    for p in range(x.shape[-1] // 128):
        xp = x[:, 128 * p:128 * (p + 1)]
        lo = lax.broadcasted_iota(jnp.int32, xp.shape, 1) < HEAD
        s_lo = jnp.sum(jnp.where(lo, xp, 0.0), axis=-1, keepdims=True)
        s_hi = jnp.sum(jnp.where(lo, 0.0, xp), axis=-1, keepdims=True)
        outs.append(jnp.where(lo, s_lo, s_hi))
    return jnp.concatenate(outs, axis=-1)


def _heads(x):
    return jnp.stack([x[:, HEAD * h:HEAD * (h + 1)] for h in range(x.shape[-1] // HEAD)], axis=0)


def _unheads(x):
    return jnp.concatenate([x[h] for h in range(x.shape[0])], axis=-1)


def _ada_kernel(c_ref, w_ref, b_ref, o_ref):
    o_ref[0] = _hdot(_silu(c_ref[...]), w_ref[0]) + b_ref[0]


def _ada(cvec, w_ada, b_ada):
    depth = w_ada.shape[0]
    tn = 1536
    return pl.pallas_call(
        _ada_kernel,
        grid=(depth, 6 * D // tn),
        in_specs=[pl.BlockSpec((16, D), lambda l, j: (0, 0)),
                  pl.BlockSpec((1, D, tn), lambda l, j: (l, 0, j)),
                  pl.BlockSpec((1, 1, tn), lambda l, j: (l, 0, j))],
        out_specs=pl.BlockSpec((1, 16, tn), lambda l, j: (l, 0, j)),
        out_shape=jax.ShapeDtypeStruct((depth, 16, 6 * D), F32),
        compiler_params=_cparams(("arbitrary", "arbitrary")),
        name="adaln",
    )(cvec, w_ada, b_ada.reshape(depth, 1, 6 * D))


def _mod_row(i, nt, nl, nb):
    return jnp.where(i % nt >= nl, nb, i // nt)


def _inproj_kernel(hp_ref, h_ref, hn_ref, mod_ref, g_ref, w_ref, cf_ref, o_ref, a_scr, z_scr,
                   *, nt, nl, silu_lo, silu_hi):
    i = pl.program_id(0)
    j = pl.program_id(1)

    @pl.when(j == 0)
    def _():
        t = i % nt
        first = (t == 0) | (t == nl)
        last = (t == nl - 1) | (t == nt - 1)
        shift = mod_ref[0, :, 0:D]
        scale = mod_ref[0, :, D:2 * D]
        g = g_ref[...]
        a_scr[0:HALO] = jnp.where(first, 0.0, _norm_mod(hp_ref[...], g, shift, scale)).astype(BF16)
        a_scr[HALO:HALO + TM] = _norm_mod(h_ref[...], g, shift, scale).astype(BF16)
        a_scr[HALO + TM:] = jnp.where(last, 0.0, _norm_mod(hn_ref[...], g, shift, scale)).astype(BF16)

    z_scr[...] = jnp.dot(a_scr[...], w_ref[...], preferred_element_type=F32)
    acc = cf_ref[5:6, :] + cf_ref[0:1, :] * z_scr[HALO - 2:HALO - 2 + TM, :]
    for k in range(1, MB_CONV):
        acc = acc + cf_ref[k:k + 1, :] * z_scr[HALO - 2 + k:HALO - 2 + k + TM, :]
    is_silu = (j >= silu_lo) & (j < silu_hi)

    @pl.when(is_silu)
    def _():
        o_ref[...] = _silu(acc)

    @pl.when(jnp.logical_not(is_silu))
    def _():
        o_ref[...] = acc


def _inproj(h2, mods_l, g, w, cf, nb, nt, nl, silu_lo, silu_hi):
    rows = h2.shape[0]
    n = w.shape[1]
    hb = TM // HALO
    return pl.pallas_call(
        functools.partial(_inproj_kernel, nt=nt, nl=nl, silu_lo=silu_lo, silu_hi=silu_hi),
        grid=(rows // TM, n // TN),
        in_specs=[pl.BlockSpec((HALO, D), lambda i, j: (jnp.maximum(i * hb - 1, 0), 0)),
                  pl.BlockSpec((TM, D), lambda i, j: (i, 0)),
                  pl.BlockSpec((HALO, D), lambda i, j: (jnp.minimum((i + 1) * hb, rows // HALO - 1), 0)),
                  pl.BlockSpec((1, 1, 6 * D), lambda i, j: (_mod_row(i, nt, nl, nb), 0, 0)),
                  pl.BlockSpec((1, D), lambda i, j: (0, 0)),
                  pl.BlockSpec((D, TN), lambda i, j: (0, j)),
                  pl.BlockSpec((8, TN), lambda i, j: (0, j))],
        out_specs=pl.BlockSpec((TM, TN), lambda i, j: (i, j)),
        out_shape=jax.ShapeDtypeStruct((rows, n), F32),
        scratch_shapes=[pltpu.VMEM((TM + 2 * HALO, D), BF16), pltpu.VMEM((TM + 2 * HALO, TN), F32)],
        compiler_params=_cparams(("arbitrary", "arbitrary")),
        name="mix_inproj",
    )(h2, h2, h2, mods_l, g, w, cf)


def _nmm_kernel(h_ref, mod_ref, g_ref, w_ref, o_ref, a_scr):
    @pl.when(pl.program_id(1) == 0)
    def _():
        a_scr[...] = _norm_mod(h_ref[...], g_ref[...], mod_ref[0, :, 0:D], mod_ref[0, :, D:2 * D]).astype(BF16)

    o_ref[...] = jnp.dot(a_scr[...], w_ref[...], preferred_element_type=F32).astype(o_ref.dtype)


def _nmm(h2, mods_l, g, w, nb, nt, nl, out_dtype):
    rows = h2.shape[0]
    n = w.shape[1]
    return pl.pallas_call(
        _nmm_kernel,
        grid=(rows // TM, n // TN),
        in_specs=[pl.BlockSpec((TM, D), lambda i, j: (i, 0)),
                  pl.BlockSpec((1, 1, 6 * D), lambda i, j: (_mod_row(i, nt, nl, nb), 0, 0)),
                  pl.BlockSpec((1, D), lambda i, j: (0, 0)),
                  pl.BlockSpec((D, TN), lambda i, j: (0, j))],
        out_specs=pl.BlockSpec((TM, TN), lambda i, j: (i, j)),
        out_shape=jax.ShapeDtypeStruct((rows, n), out_dtype),
        scratch_shapes=[pltpu.VMEM((TM, D), BF16)],
        compiler_params=_cparams(("arbitrary", "arbitrary")),
        name="norm_proj",
    )(h2, mods_l, g, w)


def _rwkv_kernel(z_ref, w0_ref, wlw_ref, a0_ref, wla_ref, kk_ref, ka_ref, y_ref, s_scr):
    d = pl.program_id(1)

    @pl.when(pl.program_id(2) == 0)
    def _():
        s_scr[...] = jnp.zeros_like(s_scr)

    L = RW_CHUNK
    z = z_ref[0]
    r = z[:, 0:RW_DIM]
    k = z[:, RW_DIM:2 * RW_DIM]
    v = z[:, 2 * RW_DIM:3 * RW_DIM]
    lora = z[:, 3 * RW_DIM:3 * RW_DIM + 128]
    logw = w0_ref[0] + _hdot(jnp.tanh(lora), wlw_ref[0])
    logdec = -math.exp(-0.5) * _sigmoid(logw)
    iclr = _sigmoid(a0_ref[0] + _hdot(lora, wla_ref[0]))
    kk = k * kk_ref[...]
    kk = kk / jnp.maximum(jnp.sqrt(_seg64_sum(kk * kk)), 1e-12)
    kdir = k * (1.0 + (iclr - 1.0) * ka_ref[...])
    bvec = kk * iclr

    row = lax.broadcasted_iota(jnp.int32, (L, L), 0)
    col = lax.broadcasted_iota(jnp.int32, (L, L), 1)
    ahead = (row - col) * (1 - 2 * d)
    incl = ahead >= 0
    strict = ahead > 0
    lc = _hdot(jnp.where(incl, 1.0, 0.0), logdec)
    ltot = jnp.sum(logdec, axis=0, keepdims=True)
    g_in = jnp.exp(lc)
    g_ex = jnp.exp(lc - logdec)
    g_inv = jnp.exp(-lc)
    g_tail = jnp.exp(ltot - lc)

    ar = _heads(jnp.concatenate([-kk * g_ex, r * g_in], axis=0))
    bk = _heads(jnp.concatenate([bvec * g_inv, kdir * g_inv], axis=0))
    bk_tail = _heads(jnp.concatenate([bvec * g_tail, kdir * g_tail], axis=0))
    vh = _heads(v)
    s0 = s_scr[...]

    m1 = _bmm('hlk,hsk->hls', ar, bk)
    nmat = jnp.where(strict, m1[:, :L, :L], 0.0)
    a_ak = jnp.where(strict, m1[:, :L, L:], 0.0)
    m_r = jnp.concatenate([jnp.where(incl, m1[:, L:, :L], 0.0), jnp.where(incl, m1[:, L:, L:], 0.0)], axis=2)
    m2 = _bmm('hlk,hvk->hlv', ar, s0)
    x = m2[:, :L] + _bmm('hls,hsv->hlv', a_ak, vh)
    p = nmat
    steps = int(math.log2(L))
    for i in range(steps):
        x = x + _bmm('hls,hsv->hlv', p, x)
        if i < steps - 1:
            p = _bmm('hls,hst->hlt', p, p)
    uv = jnp.concatenate([x, vh], axis=1)
    y = m2[:, L:] + _bmm('hls,hsv->hlv', m_r, uv)
    y_ref[0, 0] = _unheads(y)
    g_tot = _heads(jnp.exp(ltot))
    s_scr[...] = s0 * g_tot + _bmm('hvl,hlk->hvk', jnp.swapaxes(uv, 1, 2), bk_tail)


def _rwkv_scan(z3, w0, wlw, a0, wla, kkw, kaw, seq):
    nb, t, _ = z3.shape
    nc = t // RW_CHUNK
    ncl = seq // RW_CHUNK

    def chunk(d, c):
        return jnp.where(d == 0, (c + ncl) % nc, nc - 1 - c)

    return pl.pallas_call(
        _rwkv_kernel,
        grid=(nb, 2, nc),
        in_specs=[pl.BlockSpec((1, RW_CHUNK, RW_PAD), lambda b, d, c: (b, chunk(d, c), 0)),
                  pl.BlockSpec((1, 1, RW_DIM), lambda b, d, c: (d, 0, 0)),
                  pl.BlockSpec((1, 128, RW_DIM), lambda b, d, c: (d, 0, 0)),
                  pl.BlockSpec((1, 1, RW_DIM), lambda b, d, c: (d, 0, 0)),
                  pl.BlockSpec((1, 128, RW_DIM), lambda b, d, c: (d, 0, 0)),
                  pl.BlockSpec((1, RW_DIM), lambda b, d, c: (0, 0)),
                  pl.BlockSpec((1, RW_DIM), lambda b, d, c: (0, 0))],
        out_specs=pl.BlockSpec((1, 1, RW_CHUNK, RW_DIM), lambda b, d, c: (d, b, chunk(d, c), 0)),
        out_shape=jax.ShapeDtypeStruct((2, nb, t, RW_DIM), F32),
        scratch_shapes=[pltpu.VMEM((RW_HEADS, HEAD, HEAD), F32)],
        compiler_params=_cparams(("arbitrary", "arbitrary", "arbitrary")),
        name="rwkv7_scan",
    )(z3, w0, wlw, a0, wla, kkw, kaw)


def _ssd_kernel(z_ref, dtb_ref, alog_ref, y_ref, s_scr):
    d = pl.program_id(1)

    @pl.when(pl.program_id(2) == 0)
    def _():
        s_scr[...] = jnp.zeros_like(s_scr)

    L = MB_CHUNK
    z = z_ref[0]
    xm = z[:, 0:MB_DIM]
    dt_all = _softplus(z[:, 1536:1664] + dtb_ref[...])
    a_all = dt_all * (-jnp.exp(alog_ref[...]))
    row = lax.broadcasted_iota(jnp.int32, (L, L), 0)
    col = lax.broadcasted_iota(jnp.int32, (L, L), 1)
    incl = (row - col) * (1 - 2 * d) >= 0
    cs = _hdot(jnp.where(incl, 1.0, 0.0), a_all)
    cs_t = cs.T
    tot = jnp.sum(a_all, axis=0, keepdims=True)
    fwd = d == 0
    gmat = []
    for g in range(MB_GROUPS):
        bg = z[:, 512 + 128 * g:640 + 128 * g]
        cg = z[:, 768 + 128 * g:896 + 128 * g]
        gmat.append((bg, cg, _bdot_nt(cg, bg)))
    outs = []
    for h in range(MB_HEADS):
        bg, cg, cb = gmat[h // (MB_HEADS // MB_GROUPS)]
        cs_col = jnp.where(fwd, cs[:, h:h + 1], cs[:, 8 + h:9 + h])
        cs_row = jnp.where(fwd, cs_t[h:h + 1, :], cs_t[8 + h:9 + h, :])
        dt_col = jnp.where(fwd, dt_all[:, h:h + 1], dt_all[:, 8 + h:9 + h])
        tot_h = jnp.where(fwd, tot[:, h:h + 1], tot[:, 8 + h:9 + h])
        lmat = jnp.exp(jnp.where(incl, cs_col - cs_row, -jnp.inf))
        xh = xm[:, HEAD * h:HEAD * (h + 1)] * dt_col
        s0 = s_scr[h]
        y = _bdot(cb * lmat, xh) + jnp.exp(cs_col) * _bdot_nt(cg, s0)
        outs.append(y)
        xd = xh * jnp.exp(tot_h - cs_col)
        s_scr[h] = s0 * jnp.exp(tot_h) + _bdot(xd.T, bg)
    y_ref[0, 0] = jnp.concatenate(outs, axis=-1)


def _ssd_scan(z3, dtb, alog, seq):
    nb, t, _ = z3.shape
    nc = t // MB_CHUNK
    ncl = seq // MB_CHUNK

    def chunk(d, c):
        return jnp.where(d == 0, (c + ncl) % nc, nc - 1 - c)

    return pl.pallas_call(
        _ssd_kernel,
        grid=(nb, 2, nc),
        in_specs=[pl.BlockSpec((1, MB_CHUNK, MB_PAD), lambda b, d, c: (b, chunk(d, c), 1)),
                  pl.BlockSpec((1, 128), lambda b, d, c: (0, 0)),
                  pl.BlockSpec((1, 128), lambda b, d, c: (0, 0))],
        out_specs=pl.BlockSpec((1, 1, MB_CHUNK, MB_DIM), lambda b, d, c: (d, b, chunk(d, c), 0)),
        out_shape=jax.ShapeDtypeStruct((2, nb, t, MB_DIM), F32),
        scratch_shapes=[pltpu.VMEM((MB_HEADS, HEAD, MB_STATE), F32)],
        compiler_params=_cparams(("arbitrary", "arbitrary", "arbitrary")),
        name="ssd_scan",
    )(z3, dtb, alog)


def _finish_kernel(zr_ref, zm_ref, yr_ref, ym_ref, h_ref, mod_ref, a0_ref, wla_ref, ka_ref, rk_ref, gup_ref,
                   gnw_ref, gnb_ref, mbd_ref, mbn_ref, wout_ref, o_ref):
    zr = zr_ref[...]
    r = zr[:, 0:RW_DIM]
    k = zr[:, RW_DIM:2 * RW_DIM]
    v = zr[:, 2 * RW_DIM:3 * RW_DIM]
    lora = zr[:, 3 * RW_DIM:3 * RW_DIM + 128]
    gate = _bdot(_sigmoid(zr[:, 3 * RW_DIM + 128:3 * RW_DIM + 256]), gup_ref[...])
    bonus = jnp.zeros_like(r)
    for dd in range(2):
        iclr = _sigmoid(a0_ref[dd] + _hdot(lora, wla_ref[dd]))
        bonus = bonus + r * (k * (1.0 + (iclr - 1.0) * ka_ref[...])) * rk_ref[dd]
    bonus = _seg64_sum(bonus) * v
    y = yr_ref[0] + yr_ref[1]
    mu = _seg64_sum(y) * (1.0 / HEAD)
    yc = y - mu
    var = _seg64_sum(yc * yc) * (1.0 / HEAD)
    y = yc * lax.rsqrt(var + RW_GN_EPS) * gnw_ref[...] + gnb_ref[...]
    o_rw = (y + bonus) * gate

    zm = zm_ref[...]
    xm = zm[:, 0:MB_DIM]
    zg = zm[:, 1024:1536]
    ym = (ym_ref[0] + ym_ref[1] + mbd_ref[...] * xm) * _silu(zg)
    gw = MB_DIM // MB_GROUPS
    parts = []
    for g in range(MB_GROUPS):
        yg = ym[:, gw * g:gw * (g + 1)]
        parts.append(yg * lax.rsqrt(jnp.mean(yg * yg, axis=-1, keepdims=True) + NORM_EPS))
    o_mb = jnp.concatenate(parts, axis=-1) * mbn_ref[...]
    o = jnp.concatenate([o_rw, o_mb], axis=-1)
    o_ref[...] = h_ref[...] + mod_ref[0, :, 2 * D:3 * D] * _bdot(o, wout_ref[...])


def _finish(z2, yrw, ymb, h2, mods_l, a0, wla, kaw, rk, gup, gnw, gnb, mbd, mbn, wout, nb, nt, nl):
    rows = h2.shape[0]
    full = lambda *shape: pl.BlockSpec(shape, lambda i: (0,) * len(shape))
    return pl.pallas_call(
        _finish_kernel,
        grid=(rows // TM,),
        in_specs=[pl.BlockSpec((TM, RW_PAD), lambda i: (i, 0)),
                  pl.BlockSpec((TM, MB_PAD), lambda i: (i, 1)),
                  pl.BlockSpec((2, TM, RW_DIM), lambda i: (0, i, 0)),
                  pl.BlockSpec((2, TM, MB_DIM), lambda i: (0, i, 0)),
                  pl.BlockSpec((TM, D), lambda i: (i, 0)),
                  pl.BlockSpec((1, 1, 6 * D), lambda i: (_mod_row(i, nt, nl, nb), 0, 0)),
                  full(2, 1, RW_DIM), full(2, 128, RW_DIM), full(1, RW_DIM), full(2, 1, RW_DIM),
                  full(128, RW_DIM), full(1, RW_DIM), full(1, RW_DIM), full(1, MB_DIM), full(1, MB_DIM),
                  full(D, D)],
        out_specs=pl.BlockSpec((TM, D), lambda i: (i, 0)),
        out_shape=jax.ShapeDtypeStruct((rows, D), F32),
        compiler_params=_cparams(("arbitrary",)),
        name="mix_finish",
    )(z2, z2, yrw, ymb, h2, mods_l, a0, wla, kaw, rk, gup, gnw, gnb, mbd, mbn, wout)


def _ffn_kernel(h_ref, mod_ref, g_ref, w1_ref, w3_ref, w2_ref, o_ref, a_scr, acc_scr):
    f = pl.program_id(1)

    @pl.when(f == 0)
    def _():
        a_scr[...] = _norm_mod(h_ref[...], g_ref[...], mod_ref[0, :, 3 * D:4 * D],
                               mod_ref[0, :, 4 * D:5 * D]).astype(BF16)
        acc_scr[...] = jnp.zeros_like(acc_scr)

    a = a_scr[...]
    hid = _silu(jnp.dot(a, w1_ref[...], preferred_element_type=F32)) * jnp.dot(a, w3_ref[...],
                                                                               preferred_element_type=F32)
    acc_scr[...] += jnp.dot(hid.astype(BF16), w2_ref[...], preferred_element_type=F32)

    @pl.when(f == pl.num_programs(1) - 1)
    def _():
        o_ref[...] = h_ref[...] + mod_ref[0, :, 5 * D:6 * D] * acc_scr[...]


def _ffn(h2, mods_l, g, w1, w3, w2, nb, nt, nl):
    rows = h2.shape[0]
    return pl.pallas_call(
        _ffn_kernel,
        grid=(rows // TM, D_FF // TF),
        in_specs=[pl.BlockSpec((TM, D), lambda i, f: (i, 0)),
                  pl.BlockSpec((1, 1, 6 * D), lambda i, f: (_mod_row(i, nt, nl, nb), 0, 0)),
                  pl.BlockSpec((1, D), lambda i, f: (0, 0)),
                  pl.BlockSpec((D, TF), lambda i, f: (0, f)),
                  pl.BlockSpec((D, TF), lambda i, f: (0, f)),
                  pl.BlockSpec((TF, D), lambda i, f: (f, 0))],
        out_specs=pl.BlockSpec((TM, D), lambda i, f: (i, 0)),
        out_shape=jax.ShapeDtypeStruct((rows, D), F32),
        scratch_shapes=[pltpu.VMEM((TM, D), BF16), pltpu.VMEM((TM, D), F32)],
        compiler_params=_cparams(("arbitrary", "arbitrary")),
        name="ffn_swiglu",
    )(h2, mods_l, g, w1, w3, w2)


def _attn_kernel(q_ref, k_ref, v_ref, kc_ref, vc_ref, bias_ref, o_ref, *, n_rows):
    r = pl.program_id(1)
    rs = jnp.clip(r - NA_KH // 2, 0, n_rows - NA_KH)
    start = pl.multiple_of(rs * GRID_W, GRID_W)
    nk = NA_KH * GRID_W
    for h in range(NA_HEADS):
        sl = slice(HEAD * h, HEAD * (h + 1))
        q = q_ref[0, :, sl]
        kh = k_ref[0, pl.ds(start, nk), sl]
        vh = v_ref[0, pl.ds(start, nk), sl]
        s_lat = _bdot_nt(q, kh) * (HEAD ** -0.5) + bias_ref[0, h]
        s_ctx = _bdot_nt(q, kc_ref[0, :, sl]) * (HEAD ** -0.5)
        m = jnp.maximum(jnp.max(s_lat, axis=-1, keepdims=True), jnp.max(s_ctx, axis=-1, keepdims=True))
        p_lat = jnp.exp(s_lat - m)
        p_ctx = jnp.exp(s_ctx - m)
        den = jnp.sum(p_lat, axis=-1, keepdims=True) + jnp.sum(p_ctx, axis=-1, keepdims=True)
        o = _bdot(p_lat, vh) + _bdot(p_ctx, vc_ref[0, :, sl])
        o_ref[0, :, sl] = (o / den).astype(o_ref.dtype)


def _attention(qkv, bias, seq, ctx_len):
    nb = qkv.shape[0]
    n_rows = seq // GRID_W
    nk = NA_KH * GRID_W

    def cfg(b, r):
        return r - jnp.clip(r - NA_KH // 2, 0, n_rows - NA_KH)

    return pl.pallas_call(
        functools.partial(_attn_kernel, n_rows=n_rows),
        grid=(nb, n_rows),
        in_specs=[pl.BlockSpec((1, GRID_W, D), lambda b, r: (b, r, 0)),
                  pl.BlockSpec((1, seq, D), lambda b, r: (b, 0, 1)),
                  pl.BlockSpec((1, seq, D), lambda b, r: (b, 0, 2)),
                  pl.BlockSpec((1, ctx_len, D), lambda b, r: (b, seq // ctx_len, 1)),
                  pl.BlockSpec((1, ctx_len, D), lambda b, r: (b, seq // ctx_len, 2)),
                  pl.BlockSpec((1, NA_HEADS, GRID_W, nk), lambda b, r: (cfg(b, r), 0, 0, 0))],
        out_specs=pl.BlockSpec((1, GRID_W, D), lambda b, r: (b, r, 0)),
        out_shape=jax.ShapeDtypeStruct((nb, seq, D), BF16),
        compiler_params=_cparams(("arbitrary", "arbitrary")),
        name="nbr_attention",
    )(qkv, qkv, qkv, qkv, qkv, bias)


def _attn_bias_table(rpb):
    q = np.arange(GRID_W)[:, None]
    c = np.arange(GRID_W)[None, :]
    ws = np.clip(q - NA_KW // 2, 0, GRID_W - NA_KW)
    inside = (c >= ws) & (c < ws + NA_KW)
    cidx = np.clip(c - q + NA_KW - 1, 0, 2 * NA_KW - 2)
    ridx = np.arange(NA_KH)[None, :] + (NA_KH - 1) - np.arange(NA_KH)[:, None]
    tab = rpb[:, ridx][:, :, :, cidx]
    tab = jnp.where(inside[None, None, None], tab.astype(F32), -jnp.inf)
    tab = tab.transpose(1, 0, 3, 2, 4)
    return tab.reshape(NA_KH, NA_HEADS, GRID_W, NA_KH * GRID_W)


def _proj_res_kernel(o_ref, h_ref, mod_ref, w_ref, out_ref):
    out_ref[0] = h_ref[0] + mod_ref[0, :, 2 * D:3 * D] * jnp.dot(o_ref[0], w_ref[...],
                                                                 preferred_element_type=F32)


def _proj_res(o3, h3, mods_l, w, seq):
    nb = o3.shape[0]
    tm = 512
    return pl.pallas_call(
        _proj_res_kernel,
        grid=(nb, seq // tm),
        in_specs=[pl.BlockSpec((1, tm, D), lambda b, i: (b, i, 0)),
                  pl.BlockSpec((1, tm, D), lambda b, i: (b, i, 0)),
                  pl.BlockSpec((1, 1, 6 * D), lambda b, i: (b, 0, 0)),
                  pl.BlockSpec((D, D), lambda b, i: (0, 0))],
        out_specs=pl.BlockSpec((1, tm, D), lambda b, i: (b, i, 0)),
        out_shape=jax.ShapeDtypeStruct((nb, seq, D), F32),
        compiler_params=_cparams(("arbitrary", "arbitrary")),
        name="attn_outproj",
    )(o3, h3, mods_l, w)


MOE_TM = 512
FF_CHUNK = 1408
ROW_TILE = 8


def _to_token_tiles(ref, val):
    n = val.shape[0]
    for s in range(ROW_TILE):
        ref[pl.ds(s, n, stride=ROW_TILE), :] = val[:, 128 * s:128 * (s + 1)]


def _from_token_tiles(ref, n):
    return jnp.concatenate([ref[pl.ds(s, n, stride=ROW_TILE), :] for s in range(ROW_TILE)], axis=-1)


def _router_kernel(h_ref, mod_ref, g_ref, rw_ref, rb_ref, a8_ref, idx_ref, gate_ref):
    a = _norm_mod(h_ref[0], g_ref[...], mod_ref[0, :, 3 * D:4 * D], mod_ref[0, :, 4 * D:5 * D])
    _to_token_tiles(a8_ref, a)
    logits = _hdot(a, rw_ref[...]) + rb_ref[...]
    lane = lax.broadcasted_iota(jnp.int32, logits.shape, 1)
    logits = jnp.where(lane < N_EXPERTS, logits, -jnp.inf)
    m1 = jnp.max(logits, axis=-1, keepdims=True)
    i1 = jnp.min(jnp.where(logits == m1, lane, 128), axis=-1, keepdims=True)
    rest = jnp.where(lane == i1, -jnp.inf, logits)
    m2 = jnp.max(rest, axis=-1, keepdims=True)
    i2 = jnp.min(jnp.where(rest == m2, lane, 128), axis=-1, keepdims=True)
    ex = jnp.exp(m2 - m1)
    idx_ref[...] = jnp.where(lane == 0, i1, jnp.where(lane == 1, i2, 0))
    gate_ref[...] = jnp.where(lane == 0, 1.0 / (1.0 + ex), jnp.where(lane == 1, ex / (1.0 + ex), 0.0))


def _router(h3, mods_l, g, rw, rb):
    nb, seq, _ = h3.shape
    tm = 512
    ni = seq // tm
    rows = nb * seq
    return pl.pallas_call(
        _router_kernel,
        grid=(nb, ni),
        in_specs=[pl.BlockSpec((1, tm, D), lambda b, i: (b, i, 0)),
                  pl.BlockSpec((1, 1, 6 * D), lambda b, i: (b, 0, 0)),
                  pl.BlockSpec((1, D), lambda b, i: (0, 0)),
                  pl.BlockSpec((D, 128), lambda b, i: (0, 0)),
                  pl.BlockSpec((1, 128), lambda b, i: (0, 0))],
        out_specs=[pl.BlockSpec((tm * ROW_TILE, 128), lambda b, i: (b * ni + i, 0)),
                   pl.BlockSpec((tm, 128), lambda b, i: (b * ni + i, 0)),
                   pl.BlockSpec((tm, 128), lambda b, i: (b * ni + i, 0))],
        out_shape=[jax.ShapeDtypeStruct((rows * ROW_TILE, 128), F32),
                   jax.ShapeDtypeStruct((rows, 128), jnp.int32),
                   jax.ShapeDtypeStruct((rows, 128), F32)],
        compiler_params=_cparams(("arbitrary", "arbitrary")),
        name="moe_router",
    )(h3, mods_l, g, rw, rb)


def _route_plan(idx, rows):
    n_pairs = 2 * rows
    n_tiles = n_pairs // MOE_TM + N_EXPERTS
    expert = jnp.concatenate([idx[:, 0], idx[:, 1]])
    onehot = (expert[:, None] == jnp.arange(N_EXPERTS)[None, :]).astype(jnp.int32)
    cum = jnp.cumsum(onehot, axis=0)
    counts = cum[-1]
    rank = jnp.sum(onehot * cum, axis=1) - 1
    padded = (counts + MOE_TM - 1) // MOE_TM * MOE_TM
    ends = jnp.cumsum(padded)
    starts = ends - padded
    pos = starts[expert] + rank
    pair = jnp.arange(n_pairs, dtype=jnp.int32)
    tok_of_row = jnp.zeros((n_tiles * MOE_TM,), jnp.int32).at[pos].set(pair % rows)
    dst_of_row = jnp.zeros((n_tiles * MOE_TM,), jnp.int32).at[pos].set(pair)
    tile_start = jnp.arange(n_tiles, dtype=jnp.int32) * MOE_TM
    tile_expert = jnp.minimum(jnp.sum((tile_start[:, None] >= ends[None, :]).astype(jnp.int32), axis=1),
                              N_EXPERTS - 1)
    n_valid = jnp.clip(starts[tile_expert] + counts[tile_expert] - tile_start, 0, MOE_TM)
    n_valid = jnp.where(tile_start < ends[-1], n_valid, 0)
    return tile_expert.astype(jnp.int32), n_valid.astype(jnp.int32), tok_of_row, dst_of_row


def _token_copy(src, src_tok, dst, dst_tok, sem, n_tok=1):
    n = n_tok * ROW_TILE
    return pltpu.make_async_copy(src.at[pl.ds(pl.multiple_of(src_tok * ROW_TILE, ROW_TILE), n)],
                                 dst.at[pl.ds(pl.multiple_of(dst_tok * ROW_TILE, ROW_TILE), n)], sem)


def _expert_kernel(texp_ref, nval_ref, tok_ref, dst_ref, a8_hbm, w1_ref, w3_ref, w2_ref, y8_hbm,
                   xbuf, ybuf, gsem, ssem):
    i = pl.program_id(0)
    nv = nval_ref[i]

    @pl.when(nv > 0)
    def _():
        base = i * MOE_TM

        def gather(r, carry):
            _token_copy(a8_hbm, tok_ref[base + r], xbuf, r, gsem).start()
            return carry

        lax.fori_loop(0, MOE_TM, gather, 0)
        _token_copy(a8_hbm, 0, xbuf, 0, gsem, MOE_TM).wait()
        x = _from_token_tiles(xbuf, MOE_TM).astype(BF16)
        acc = jnp.zeros((MOE_TM, D), F32)
        for c in range(D_FF // FF_CHUNK):
            sl = slice(FF_CHUNK * c, FF_CHUNK * (c + 1))
            hid = _silu(jnp.dot(x, w1_ref[0, :, sl], preferred_element_type=F32)) * jnp.dot(
                x, w3_ref[0, :, sl], preferred_element_type=F32)
            acc = acc + jnp.dot(hid.astype(BF16), w2_ref[0, sl, :], preferred_element_type=F32)
        _to_token_tiles(ybuf, acc)

        def scatter(r, carry):
            _token_copy(ybuf, r, y8_hbm, dst_ref[base + r], ssem).start()
            return carry

        lax.fori_loop(0, nv, scatter, 0)
        pltpu.make_async_copy(ybuf.at[pl.ds(0, nv * ROW_TILE)], y8_hbm.at[pl.ds(0, nv * ROW_TILE)], ssem).wait()


def _experts(tile_expert, n_valid, tok_of_row, dst_of_row, a8, w1, w3, w2, rows):
    n_tiles = tile_expert.shape[0]
    wspec = lambda shape: pl.BlockSpec(shape, lambda i, te, nv, tk, ds: (te[i], 0, 0),
                                       pipeline_mode=pl.Buffered(1))
    return pl.pallas_call(
        _expert_kernel,
        grid_spec=pltpu.PrefetchScalarGridSpec(
            num_scalar_prefetch=4,
            grid=(n_tiles,),
            in_specs=[pl.BlockSpec(memory_space=pl.ANY),
                      wspec((1, D, D_FF)), wspec((1, D, D_FF)), wspec((1, D_FF, D))],
            out_specs=pl.BlockSpec(memory_space=pl.ANY),
            scratch_shapes=[pltpu.VMEM((MOE_TM * ROW_TILE, 128), F32), pltpu.VMEM((MOE_TM * ROW_TILE, 128), F32),
                            pltpu.SemaphoreType.DMA, pltpu.SemaphoreType.DMA]),
        out_shape=jax.ShapeDtypeStruct((2 * rows * ROW_TILE, 128), F32),
        compiler_params=_cparams(("arbitrary",)),
        name="moe_experts",
    )(tile_expert, n_valid, tok_of_row, dst_of_row, a8, w1, w3, w2)


def _combine_kernel(h_ref, mod_ref, gate_ref, y_ref, gf_ref, o_ref):
    tm = h_ref.shape[1]
    gates = gate_ref[...]
    y = gates[:, 0:1] * _from_token_tiles(y_ref.at[0], tm) + gates[:, 1:2] * _from_token_tiles(y_ref.at[1], tm)
    out = h_ref[0] + mod_ref[0, :, 5 * D:6 * D] * y
    out = out * lax.rsqrt(jnp.mean(out * out, axis=-1, keepdims=True) + NORM_EPS)
    o_ref[0] = out * gf_ref[...]


def _combine(h3, mods_l, gates, y8, gf):
    nb, seq, _ = h3.shape
    tm = 512
    ni = seq // tm
    return pl.pallas_call(
        _combine_kernel,
        grid=(nb, ni),
        in_specs=[pl.BlockSpec((1, tm, D), lambda b, i: (b, i, 0)),
                  pl.BlockSpec((1, 1, 6 * D), lambda b, i: (b, 0, 0)),
                  pl.BlockSpec((tm, 128), lambda b, i: (b * ni + i, 0)),
                  pl.BlockSpec((2, tm * ROW_TILE, 128), lambda b, i: (0, b * ni + i, 0)),
                  pl.BlockSpec((1, D), lambda b, i: (0, 0))],
        out_specs=pl.BlockSpec((1, tm, D), lambda b, i: (b, i, 0)),
        out_shape=jax.ShapeDtypeStruct((nb, seq, D), F32),
        compiler_params=_cparams(("arbitrary", "arbitrary")),
        name="moe_combine_norm",
    )(h3, mods_l, gates, y8, gf)


def _moe(h3, mods_l, g, rw, rb, w1, w3, w2, gf):
    nb, seq, _ = h3.shape
    rows = nb * seq
    a8, idx, gates = _router(h3, mods_l, g, rw, rb)
    tile_expert, n_valid, tok_of_row, dst_of_row = _route_plan(idx, rows)
    y8 = _experts(tile_expert, n_valid, tok_of_row, dst_of_row, a8, w1, w3, w2, rows)
    return _combine(h3, mods_l, gates, y8.reshape(2, rows * ROW_TILE, 128), gf)


def _pad_cols(a, n):
    return jnp.pad(a, [(0, 0)] * (a.ndim - 1) + [(0, n - a.shape[-1])])


def _mix_in_layout(w_in, rw_mu, conv_w, conv_b):
    w_rw = _pad_cols(w_in[:, :RW_COLS], RW_PAD)
    mb = w_in[:, RW_COLS:]
    w_mb = _pad_cols(jnp.concatenate([mb[:, MB_DIM:MB_DIM + MB_XBC], mb[:, :MB_DIM], mb[:, MB_DIM + MB_XBC:]],
                                     axis=1), MB_PAD)
    w = jnp.concatenate([w_rw, w_mb], axis=1).astype(BF16)
    mu_p = _pad_cols(rw_mu[0], RW_PAD)
    mu_n = _pad_cols(rw_mu[1], RW_PAD)
    zeros = jnp.zeros((RW_PAD,), F32)
    cf_rw = jnp.stack([zeros, mu_p, 1.0 - mu_p - mu_n, mu_n, zeros, zeros, zeros, zeros])
    ident = jnp.zeros((8, MB_PAD - MB_XBC), F32).at[2].set(1.0)
    cf_xbc = jnp.concatenate([conv_w, conv_b[None], jnp.zeros((2, MB_XBC), F32)], axis=0)
    cf = jnp.concatenate([cf_rw, cf_xbc, ident], axis=1)
    return w, cf


def _lora_pad(up, offset):
    out = jnp.zeros((2, 128, up.shape[-1]), F32)
    for d in range(2):
        out = out.at[d, offset + RW_LORA * d:offset + RW_LORA * (d + 1)].set(up[d])
    return out


def kernel(x, c, ctx, c_ctx, w_ada, b_ada, norm_mix, norm_ffn, norm_final, mix_w_in, mix_w_out, rw_mu, rw_w0,
           rw_w_up, rw_a0, rw_a_up, rw_g_up, rw_k_k, rw_k_a, rw_r_k, rw_gn_w, rw_gn_b, mb_conv_w, mb_conv_b,
           mb_dt_bias, mb_a_log, mb_d, mb_norm_w, ffn_w1, ffn_w3, ffn_w2, na_w_qkv, na_w_out, na_rpb,
           moe_router_w, moe_router_b, moe_w1, moe_w3, moe_w2):
    nb, seq, _ = x.shape
    ctx_len = ctx.shape[1]
    t = seq + ctx_len
    nt, nl = t // TM, seq // TM
    assert seq % TM == 0 and ctx_len % TM == 0 and seq % ctx_len == 0 and nb < 16

    cvec = jnp.zeros((16, D), F32).at[:nb].set(c).at[nb].set(c_ctx)
    mods = _ada(cvec, w_ada, b_ada).reshape(w_ada.shape[0], 16, 1, 6 * D)

    h = jnp.concatenate([x, ctx], axis=1).reshape(nb * t, D)

    w_in, cf = _mix_in_layout(mix_w_in[0], rw_mu[0], mb_conv_w[0], mb_conv_b[0])
    z = _inproj(h, mods[0], norm_mix[0][None], w_in, cf, nb, nt, nl, RW_PAD // TN, (RW_PAD + MB_XBC) // TN)
    z3 = z.reshape(nb, t, MIX_PAD)
    wlw = _lora_pad(rw_w_up[0], 0)
    wla = _lora_pad(rw_a_up[0], 2 * RW_LORA)
    w0 = rw_w0[0][:, None, :]
    a0 = rw_a0[0][:, None, :]
    yrw = _rwkv_scan(z3, w0, wlw, a0, wla, rw_k_k[0][None], rw_k_a[0][None], seq)
    dtb = _pad_cols(mb_dt_bias[0].reshape(1, 2 * MB_HEADS), 128)
    alog = _pad_cols(mb_a_log[0].reshape(1, 2 * MB_HEADS), 128)
    ymb = _ssd_scan(z3, dtb, alog, seq)
    gup = jnp.pad(rw_g_up[0], ((0, 128 - RW_LORA_G), (0, 0))).astype(BF16)
    h = _finish(z, yrw.reshape(2, nb * t, RW_DIM), ymb.reshape(2, nb * t, MB_DIM), h, mods[0], a0, wla,
                rw_k_a[0][None], rw_r_k[0].reshape(2, 1, RW_DIM), gup, rw_gn_w[0][None], rw_gn_b[0][None],
                jnp.repeat(mb_d[0], HEAD)[None], mb_norm_w[0][None], mix_w_out[0].astype(BF16), nb, nt, nl)
    h = _ffn(h, mods[0], norm_ffn[0][None], ffn_w1[0].astype(BF16), ffn_w3[0].astype(BF16),
             ffn_w2[0].astype(BF16), nb, nt, nl)

    qkv = _nmm(h, mods[1], norm_mix[1][None], na_w_qkv[0].astype(BF16), nb, nt, nl, BF16)
    o = _attention(qkv.reshape(nb, t, 3 * D), _attn_bias_table(na_rpb[0]), seq, ctx_len)
    mods1 = mods[1].reshape(16, 1, 6 * D)
    h3 = _proj_res(o, h.reshape(nb, t, D), mods1, na_w_out[0].astype(BF16), seq)
    rw = _pad_cols(moe_router_w[0], 128)
    rb = _pad_cols(moe_router_b[0][None], 128)
    return _moe(h3, mods1, norm_ffn[1][None], rw, rb, moe_w1[0].astype(BF16), moe_w3[0].astype(BF16),
                moe_w2[0].astype(BF16), norm_final[None])
```

```python
import functools
import math

import numpy as np
import jax
import jax.numpy as jnp
from jax import lax
from jax.experimental import pallas as pl
from jax.experimental.pallas import tpu as pltpu

F32 = jnp.float32
BF16 = jnp.bfloat16

D = 1024
NORM_EPS = 1e-6
GRID_W = 64

HEAD = 64
RW_DIM = 512
RW_HEADS = 8
RW_LORA = 32
RW_LORA_G = 96
RW_GN_EPS = 64e-5
RW_COLS = 3 * RW_DIM + 4 * RW_LORA + RW_LORA_G
RW_PAD = 1792
RW_CHUNK = 64

MB_DIM = 512
MB_HEADS = 8
MB_GROUPS = 2
MB_STATE = 128
MB_CONV = 5
MB_XBC = MB_DIM + 2 * MB_GROUPS * MB_STATE
MB_PAD = 1792
MB_CHUNK = 128

MIX_PAD = RW_PAD + MB_PAD

NA_HEADS = 16
NA_KH = 8
NA_KW = 16

D_FF = 2816
N_EXPERTS = 8

TM = 256
TMB = 768
HALO = 16
TN = 256
TF = 256
VMEM_LIMIT = 56 * 1024 * 1024


def _cparams(sem):
    return pltpu.CompilerParams(dimension_semantics=sem, vmem_limit_bytes=VMEM_LIMIT)


def _bdot(a, b):
    return jnp.dot(a.astype(BF16), b.astype(BF16), preferred_element_type=F32)


def _bdot_nt(a, b):
    return lax.dot_general(a.astype(BF16), b.astype(BF16), (((1,), (1,)), ((), ())),
                           preferred_element_type=F32)


def _hdot(a, b):
    return jnp.dot(a, b, precision=lax.Precision.HIGHEST, preferred_element_type=F32)


def _bmm(spec, a, b):
    return jnp.einsum(spec, a.astype(BF16), b.astype(BF16), preferred_element_type=F32)


def _sigmoid(x):
    return 1.0 / (1.0 + jnp.exp(-x))


def _silu(x):
    return x * _sigmoid(x)


def _softplus(x):
    return jnp.maximum(x, 0.0) + jnp.log(1.0 + jnp.exp(-jnp.abs(x)))


def _norm_mod(h, g, shift, scale):
    hn = h * lax.rsqrt(jnp.mean(h * h, axis=-1, keepdims=True) + NORM_EPS)
    return (hn * g) * (1.0 + scale) + shift


def _seg64_sum(x):
    outs = []
    for p in range(x.shape[-1] // 128):
        xp = x[:, 128 * p:128 * (p + 1)]
        lo = lax.broadcasted_iota(jnp.int32, xp.shape, 1) < HEAD
        s_lo = jnp.sum(jnp.where(lo, xp, 0.0), axis=-1, keepdims=True)
        s_hi = jnp.sum(jnp.where(lo, 0.0, xp), axis=-1, keepdims=True)
        outs.append(jnp.where(lo, s_lo, s_hi))
    return jnp.concatenate(outs, axis=-1)


def _heads(x):
    return jnp.stack([x[:, HEAD * h:HEAD * (h + 1)] for h in range(x.shape[-1] // HEAD)], axis=0)


def _unheads(x):
    return jnp.concatenate([x[h] for h in range(x.shape[0])], axis=-1)


def _ada_kernel(c_ref, w_ref, b_ref, o_ref):
    o_ref[0] = _hdot(_silu(c_ref[...]), w_ref[0]) + b_ref[0]


def _ada(cvec, w_ada, b_ada):
    depth = w_ada.shape[0]
    tn = 1536
    return pl.pallas_call(
        _ada_kernel,
        grid=(depth, 6 * D // tn),
        in_specs=[pl.BlockSpec((16, D), lambda l, j: (0, 0)),
                  pl.BlockSpec((1, D, tn), lambda l, j: (l, 0, j)),
                  pl.BlockSpec((1, 1, tn), lambda l, j: (l, 0, j))],
        out_specs=pl.BlockSpec((1, 16, tn), lambda l, j: (l, 0, j)),
        out_shape=jax.ShapeDtypeStruct((depth, 16, 6 * D), F32),
        compiler_params=_cparams(("arbitrary", "arbitrary")),
        name="adaln",
    )(cvec, w_ada, b_ada.reshape(depth, 1, 6 * D))


def _mod_row(i, nt, nl, nb):
    return jnp.where(i % nt >= nl, nb, i // nt)


def _resident(shape):
    return pl.BlockSpec(shape, lambda *_: (0,) * len(shape), pipeline_mode=pl.Buffered(1))


def _inproj_kernel(hp_ref, h_ref, hn_ref, mod_ref, g_ref, w_ref, cf_ref, o_ref, a_scr, z_scr,
                   *, nt, nl, silu_lo, silu_hi):
    t = pl.program_id(0) % nt
    first = (t == 0) | (t == nl)
    last = (t == nl - 1) | (t == nt - 1)
    shift = mod_ref[0, :, 0:D]
    scale = mod_ref[0, :, D:2 * D]
    g = g_ref[...]
    a_scr[0:HALO] = jnp.where(first, 0.0, _norm_mod(hp_ref[...], g, shift, scale)).astype(BF16)
    a_scr[HALO:HALO + TM] = _norm_mod(h_ref[...], g, shift, scale).astype(BF16)
    a_scr[HALO + TM:] = jnp.where(last, 0.0, _norm_mod(hn_ref[...], g, shift, scale)).astype(BF16)
    a = a_scr[...]
    for j in range(w_ref.shape[1] // TN):
        sl = slice(TN * j, TN * (j + 1))
        z = z_scr.at[j % 2]
        z[...] = jnp.dot(a, w_ref[:, sl], preferred_element_type=F32)

        def tap(k):
            return cf_ref[k:k + 1, sl] * z[HALO - 2 + k:HALO - 2 + k + TM, :]

        if j < silu_lo:
            o_ref[:, sl] = tap(1) + tap(2) + tap(3)
        elif j < silu_hi:
            o_ref[:, sl] = _silu(cf_ref[5:6, sl] + tap(0) + tap(1) + tap(2) + tap(3) + tap(4))
        else:
            o_ref[:, sl] = z[HALO:HALO + TM, :]


def _inproj(h2, mods_l, g, w, cf, nb, nt, nl, silu_lo, silu_hi):
    rows = h2.shape[0]
    n = w.shape[1]
    hb = TM // HALO
    return pl.pallas_call(
        functools.partial(_inproj_kernel, nt=nt, nl=nl, silu_lo=silu_lo, silu_hi=silu_hi),
        grid=(rows // TM,),
        in_specs=[pl.BlockSpec((HALO, D), lambda i: (jnp.maximum(i * hb - 1, 0), 0)),
                  pl.BlockSpec((TM, D), lambda i: (i, 0)),
                  pl.BlockSpec((HALO, D), lambda i: (jnp.minimum((i + 1) * hb, rows // HALO - 1), 0)),
                  pl.BlockSpec((1, 1, 6 * D), lambda i: (_mod_row(i, nt, nl, nb), 0, 0)),
                  _resident((1, D)), _resident((D, n)), _resident((8, n))],
        out_specs=pl.BlockSpec((TM, n), lambda i: (i, 0)),
        out_shape=jax.ShapeDtypeStruct((rows, n), F32),
        scratch_shapes=[pltpu.VMEM((TM + 2 * HALO, D), BF16), pltpu.VMEM((2, TM + 2 * HALO, TN), F32)],
        compiler_params=_cparams(("arbitrary",)),
        name="mix_inproj",
    )(h2, h2, h2, mods_l, g, w, cf)


def _ctx_rows(ntb, n_lat_last, tm):
    row = lax.broadcasted_iota(jnp.int32, (tm, 1), 0)
    return (pl.program_id(0) % ntb == ntb - 1) & (row >= n_lat_last)


def _row_mod(mod_ref, modc_ref, lo, is_ctx):
    return jnp.where(is_ctx, modc_ref[0, :, lo:lo + D], mod_ref[0, :, lo:lo + D])


def _big_tile_specs(nb, ntb):
    return [pl.BlockSpec((TMB, D), lambda i: (i, 0)),
            pl.BlockSpec((1, 1, 6 * D), lambda i: (i // ntb, 0, 0)),
            pl.BlockSpec((1, 1, 6 * D), lambda i: (nb, 0, 0)),
            _resident((1, D))]


def _nmm_kernel(h_ref, mod_ref, modc_ref, g_ref, w_ref, o_ref, *, ntb, n_lat_last):
    is_ctx = _ctx_rows(ntb, n_lat_last, TMB)
    a = _norm_mod(h_ref[...], g_ref[...], _row_mod(mod_ref, modc_ref, 0, is_ctx),
                  _row_mod(mod_ref, modc_ref, D, is_ctx)).astype(BF16)
    tn = 2 * TN
    for j in range(w_ref.shape[1] // tn):
        sl = slice(tn * j, tn * (j + 1))
        o_ref[:, sl] = jnp.dot(a, w_ref[:, sl], preferred_element_type=F32).astype(o_ref.dtype)


def _nmm(h2, mods_l, g, w, nb, ntb, n_lat_last, out_dtype):
    rows = h2.shape[0]
    n = w.shape[1]
    return pl.pallas_call(
        functools.partial(_nmm_kernel, ntb=ntb, n_lat_last=n_lat_last),
        grid=(rows // TMB,),
        in_specs=_big_tile_specs(nb, ntb) + [_resident((D, n))],
        out_specs=pl.BlockSpec((TMB, n), lambda i: (i, 0)),
        out_shape=jax.ShapeDtypeStruct((rows, n), out_dtype),
        compiler_params=_cparams(("arbitrary",)),
        name="norm_proj",
    )(h2, mods_l, mods_l, g, w)


def _rwkv_kernel(z_ref, w0_ref, wlw_ref, a0_ref, wla_ref, kk_ref, ka_ref, y_ref, s_scr):
    d = pl.program_id(1)

    @pl.when(pl.program_id(2) == 0)
    def _():
        s_scr[...] = jnp.zeros_like(s_scr)

    L = RW_CHUNK
    z = z_ref[0]
    r = z[:, 0:RW_DIM]
    k = z[:, RW_DIM:2 * RW_DIM]
    v = z[:, 2 * RW_DIM:3 * RW_DIM]
    lora = z[:, 3 * RW_DIM:3 * RW_DIM + 128]
    logw = w0_ref[0] + _hdot(jnp.tanh(lora), wlw_ref[0])
    logdec = -math.exp(-0.5) * _sigmoid(logw)
    iclr = _sigmoid(a0_ref[0] + _hdot(lora, wla_ref[0]))
    kk = k * kk_ref[...]
    kk = kk / jnp.maximum(jnp.sqrt(_seg64_sum(kk * kk)), 1e-12)
    kdir = k * (1.0 + (iclr - 1.0) * ka_ref[...])
    bvec = kk * iclr

    row = lax.broadcasted_iota(jnp.int32, (L, L), 0)
    col = lax.broadcasted_iota(jnp.int32, (L, L), 1)
    ahead = (row - col) * (1 - 2 * d)
    incl = ahead >= 0
    strict = ahead > 0
    lc = _hdot(jnp.where(incl, 1.0, 0.0), logdec)
    ltot = jnp.sum(logdec, axis=0, keepdims=True)
    g_in = jnp.exp(lc)
    g_ex = jnp.exp(lc - logdec)
    g_inv = jnp.exp(-lc)
    g_tail = jnp.exp(ltot - lc)

    ar = _heads(jnp.concatenate([-kk * g_ex, r * g_in], axis=0))
    bk = _heads(jnp.concatenate([bvec * g_inv, kdir * g_inv], axis=0))
    bk_tail = _heads(jnp.concatenate([bvec * g_tail, kdir * g_tail], axis=0))
    vh = _heads(v)
    s0 = s_scr[...]

    m1 = _bmm('hlk,hsk->hls', ar, bk)
    nmat = jnp.where(strict, m1[:, :L, :L], 0.0)
    a_ak = jnp.where(strict, m1[:, :L, L:], 0.0)
    m_r = jnp.concatenate([jnp.where(incl, m1[:, L:, :L], 0.0), jnp.where(incl, m1[:, L:, L:], 0.0)], axis=2)
    m2 = _bmm('hlk,hvk->hlv', ar, s0)
    x = m2[:, :L] + _bmm('hls,hsv->hlv', a_ak, vh)
    p = nmat
    steps = int(math.log2(L))
    for i in range(steps):
        x = x + _bmm('hls,hsv->hlv', p, x)
        if i < steps - 1:
            p = _bmm('hls,hst->hlt', p, p)
    uv = jnp.concatenate([x, vh], axis=1)
    y = m2[:, L:] + _bmm('hls,hsv->hlv', m_r, uv)
    y_ref[0, 0] = _unheads(y)
    g_tot = _heads(jnp.exp(ltot))
    s_scr[...] = s0 * g_tot + _bmm('hvl,hlk->hvk', jnp.swapaxes(uv, 1, 2), bk_tail)


def _rwkv_scan(z3, w0, wlw, a0, wla, kkw, kaw, seq):
    nb, t, _ = z3.shape
    nc = t // RW_CHUNK
    ncl = seq // RW_CHUNK

    def chunk(d, c):
        return jnp.where(d == 0, (c + ncl) % nc, nc - 1 - c)

    return pl.pallas_call(
        _rwkv_kernel,
        grid=(nb, 2, nc),
        in_specs=[pl.BlockSpec((1, RW_CHUNK, RW_PAD), lambda b, d, c: (b, chunk(d, c), 0)),
                  pl.BlockSpec((1, 1, RW_DIM), lambda b, d, c: (d, 0, 0)),
                  pl.BlockSpec((1, 128, RW_DIM), lambda b, d, c: (d, 0, 0)),
                  pl.BlockSpec((1, 1, RW_DIM), lambda b, d, c: (d, 0, 0)),
                  pl.BlockSpec((1, 128, RW_DIM), lambda b, d, c: (d, 0, 0)),
                  pl.BlockSpec((1, RW_DIM), lambda b, d, c: (0, 0)),
                  pl.BlockSpec((1, RW_DIM), lambda b, d, c: (0, 0))],
        out_specs=pl.BlockSpec((1, 1, RW_CHUNK, RW_DIM), lambda b, d, c: (d, b, chunk(d, c), 0)),
        out_shape=jax.ShapeDtypeStruct((2, nb, t, RW_DIM), F32),
        scratch_shapes=[pltpu.VMEM((RW_HEADS, HEAD, HEAD), F32)],
        compiler_params=_cparams(("arbitrary", "arbitrary", "arbitrary")),
        name="rwkv7_scan",
    )(z3, w0, wlw, a0, wla, kkw, kaw)


def _ssd_kernel(z_ref, dtb_ref, alog_ref, y_ref, s_scr):
    d = pl.program_id(1)

    @pl.when(pl.program_id(2) == 0)
    def _():
        s_scr[...] = jnp.zeros_like(s_scr)

    L = MB_CHUNK
    z = z_ref[0]
    xm = z[:, 0:MB_DIM]
    dt_all = _softplus(z[:, 1536:1664] + dtb_ref[...])
    a_all = dt_all * (-jnp.exp(alog_ref[...]))
    row = lax.broadcasted_iota(jnp.int32, (L, L), 0)
    col = lax.broadcasted_iota(jnp.int32, (L, L), 1)
    incl = (row - col) * (1 - 2 * d) >= 0
    cs = _hdot(jnp.where(incl, 1.0, 0.0), a_all)
    cs_t = cs.T
    tot = jnp.sum(a_all, axis=0, keepdims=True)
    fwd = d == 0
    gmat = []
    for g in range(MB_GROUPS):
        bg = z[:, 512 + 128 * g:640 + 128 * g]
        cg = z[:, 768 + 128 * g:896 + 128 * g]
        gmat.append((bg, cg, _bdot_nt(cg, bg)))
    outs = []
    for h in range(MB_HEADS):
        bg, cg, cb = gmat[h // (MB_HEADS // MB_GROUPS)]
        cs_col = jnp.where(fwd, cs[:, h:h + 1], cs[:, 8 + h:9 + h])
        cs_row = jnp.where(fwd, cs_t[h:h + 1, :], cs_t[8 + h:9 + h, :])
        dt_col = jnp.where(fwd, dt_all[:, h:h + 1], dt_all[:, 8 + h:9 + h])
        tot_h = jnp.where(fwd, tot[:, h:h + 1], tot[:, 8 + h:9 + h])
        lmat = jnp.exp(jnp.where(incl, cs_col - cs_row, -jnp.inf))
        xh = xm[:, HEAD * h:HEAD * (h + 1)] * dt_col
        s0 = s_scr[h]
        y = _bdot(cb * lmat, xh) + jnp.exp(cs_col) * _bdot_nt(cg, s0)
        outs.append(y)
        xd = xh * jnp.exp(tot_h - cs_col)
        s_scr[h] = s0 * jnp.exp(tot_h) + _bdot(xd.T, bg)
    y_ref[0, 0] = jnp.concatenate(outs, axis=-1)


def _ssd_scan(z3, dtb, alog, seq):
    nb, t, _ = z3.shape
    nc = t // MB_CHUNK
    ncl = seq // MB_CHUNK

    def chunk(d, c):
        return jnp.where(d == 0, (c + ncl) % nc, nc - 1 - c)

    return pl.pallas_call(
        _ssd_kernel,
        grid=(nb, 2, nc),
        in_specs=[pl.BlockSpec((1, MB_CHUNK, MB_PAD), lambda b, d, c: (b, chunk(d, c), 1)),
                  pl.BlockSpec((1, 128), lambda b, d, c: (0, 0)),
                  pl.BlockSpec((1, 128), lambda b, d, c: (0, 0))],
        out_specs=pl.BlockSpec((1, 1, MB_CHUNK, MB_DIM), lambda b, d, c: (d, b, chunk(d, c), 0)),
        out_shape=jax.ShapeDtypeStruct((2, nb, t, MB_DIM), F32),
        scratch_shapes=[pltpu.VMEM((MB_HEADS, HEAD, MB_STATE), F32)],
        compiler_params=_cparams(("arbitrary", "arbitrary", "arbitrary")),
        name="ssd_scan",
    )(z3, dtb, alog)


def _finish_kernel(zr_ref, zm_ref, yr_ref, ym_ref, h_ref, mod_ref, a0_ref, wla_ref, ka_ref, rk_ref, gup_ref,
                   gnw_ref, gnb_ref, mbd_ref, mbn_ref, wout_ref, o_ref):
    zr = zr_ref[...]
    r = zr[:, 0:RW_DIM]
    k = zr[:, RW_DIM:2 * RW_DIM]
    v = zr[:, 2 * RW_DIM:3 * RW_DIM]
    lora = zr[:, 3 * RW_DIM:3 * RW_DIM + 128]
    gate = _bdot(_sigmoid(zr[:, 3 * RW_DIM + 128:3 * RW_DIM + 256]), gup_ref[...])
    bonus = jnp.zeros_like(r)
    for dd in range(2):
        iclr = _sigmoid(a0_ref[dd] + _hdot(lora, wla_ref[dd]))
        bonus = bonus + r * (k * (1.0 + (iclr - 1.0) * ka_ref[...])) * rk_ref[dd]
    bonus = _seg64_sum(bonus) * v
    y = yr_ref[0] + yr_ref[1]
    mu = _seg64_sum(y) * (1.0 / HEAD)
    yc = y - mu
    var = _seg64_sum(yc * yc) * (1.0 / HEAD)
    y = yc * lax.rsqrt(var + RW_GN_EPS) * gnw_ref[...] + gnb_ref[...]
    o_rw = (y + bonus) * gate

    zm = zm_ref[...]
    xm = zm[:, 0:MB_DIM]
    zg = zm[:, 1024:1536]
    ym = (ym_ref[0] + ym_ref[1] + mbd_ref[...] * xm) * _silu(zg)
    gw = MB_DIM // MB_GROUPS
    parts = []
    for g in range(MB_GROUPS):
        yg = ym[:, gw * g:gw * (g + 1)]
        parts.append(yg * lax.rsqrt(jnp.mean(yg * yg, axis=-1, keepdims=True) + NORM_EPS))
    o_mb = jnp.concatenate(parts, axis=-1) * mbn_ref[...]
    o = jnp.concatenate([o_rw, o_mb], axis=-1)
    o_ref[...] = h_ref[...] + mod_ref[0, :, 2 * D:3 * D] * _bdot(o, wout_ref[...])


def _finish(z2, yrw, ymb, h2, mods_l, a0, wla, kaw, rk, gup, gnw, gnb, mbd, mbn, wout, nb, nt, nl):
    rows = h2.shape[0]
    full = lambda *shape: pl.BlockSpec(shape, lambda i: (0,) * len(shape))
    return pl.pallas_call(
        _finish_kernel,
        grid=(rows // TM,),
        in_specs=[pl.BlockSpec((TM, RW_PAD), lambda i: (i, 0)),
                  pl.BlockSpec((TM, MB_PAD), lambda i: (i, 1)),
                  pl.BlockSpec((2, TM, RW_DIM), lambda i: (0, i, 0)),
                  pl.BlockSpec((2, TM, MB_DIM), lambda i: (0, i, 0)),
                  pl.BlockSpec((TM, D), lambda i: (i, 0)),
                  pl.BlockSpec((1, 1, 6 * D), lambda i: (_mod_row(i, nt, nl, nb), 0, 0)),
                  full(2, 1, RW_DIM), full(2, 128, RW_DIM), full(1, RW_DIM), full(2, 1, RW_DIM),
                  full(128, RW_DIM), full(1, RW_DIM), full(1, RW_DIM), full(1, MB_DIM), full(1, MB_DIM),
                  full(D, D)],
        out_specs=pl.BlockSpec((TM, D), lambda i: (i, 0)),
        out_shape=jax.ShapeDtypeStruct((rows, D), F32),
        compiler_params=_cparams(("arbitrary",)),
        name="mix_finish",
    )(z2, z2, yrw, ymb, h2, mods_l, a0, wla, kaw, rk, gup, gnw, gnb, mbd, mbn, wout)


def _ffn_kernel(h_ref, mod_ref, modc_ref, g_ref, w1_ref, w3_ref, w2_ref, o_ref, *, ntb, n_lat_last):
    is_ctx = _ctx_rows(ntb, n_lat_last, TMB)
    h = h_ref[...]
    a = _norm_mod(h, g_ref[...], _row_mod(mod_ref, modc_ref, 3 * D, is_ctx),
                  _row_mod(mod_ref, modc_ref, 4 * D, is_ctx)).astype(BF16)
    acc = jnp.zeros((TMB, D), F32)
    for c in range(D_FF // TF):
        sl = slice(TF * c, TF * (c + 1))
        hid = _silu(jnp.dot(a, w1_ref[:, sl], preferred_element_type=F32)) * jnp.dot(
            a, w3_ref[:, sl], preferred_element_type=F32)
        acc = acc + jnp.dot(hid.astype(BF16), w2_ref[sl, :], preferred_element_type=F32)
    o_ref[...] = h + _row_mod(mod_ref, modc_ref, 5 * D, is_ctx) * acc


def _ffn(h2, mods_l, g, w1, w3, w2, nb, ntb, n_lat_last):
    rows = h2.shape[0]
    return pl.pallas_call(
        functools.partial(_ffn_kernel, ntb=ntb, n_lat_last=n_lat_last),
        grid=(rows // TMB,),
        in_specs=_big_tile_specs(nb, ntb) + [_resident((D, D_FF)), _resident((D, D_FF)), _resident((D_FF, D))],
        out_specs=pl.BlockSpec((TMB, D), lambda i: (i, 0)),
        out_shape=jax.ShapeDtypeStruct((rows, D), F32),
        compiler_params=_cparams(("arbitrary",)),
        name="ffn_swiglu",
    )(h2, mods_l, mods_l, g, w1, w3, w2)


def _attn_kernel(q_ref, k_ref, v_ref, kc_ref, vc_ref, bias_ref, o_ref, *, n_rows):
    r = pl.program_id(1)
    rs = jnp.clip(r - NA_KH // 2, 0, n_rows - NA_KH)
    start = pl.multiple_of(rs * GRID_W, GRID_W)
    nk = NA_KH * GRID_W
    for h in range(NA_HEADS):
        sl = slice(HEAD * h, HEAD * (h + 1))
        q = q_ref[0, :, sl]
        kh = k_ref[0, pl.ds(start, nk), sl]
        vh = v_ref[0, pl.ds(start, nk), sl]
        s_lat = _bdot_nt(q, kh) * (HEAD ** -0.5) + bias_ref[0, h]
        s_ctx = _bdot_nt(q, kc_ref[0, :, sl]) * (HEAD ** -0.5)
        m = jnp.maximum(jnp.max(s_lat, axis=-1, keepdims=True), jnp.max(s_ctx, axis=-1, keepdims=True))
        p_lat = jnp.exp(s_lat - m)
        p_ctx = jnp.exp(s_ctx - m)
        den = jnp.sum(p_lat, axis=-1, keepdims=True) + jnp.sum(p_ctx, axis=-1, keepdims=True)
        o = _bdot(p_lat, vh) + _bdot(p_ctx, vc_ref[0, :, sl])
        o_ref[0, :, sl] = (o / den).astype(o_ref.dtype)


def _attention(qkv, bias, seq, ctx_len):
    nb = qkv.shape[0]
    n_rows = seq // GRID_W
    nk = NA_KH * GRID_W

    def cfg(b, r):
        return r - jnp.clip(r - NA_KH // 2, 0, n_rows - NA_KH)

    return pl.pallas_call(
        functools.partial(_attn_kernel, n_rows=n_rows),
        grid=(nb, n_rows),
        in_specs=[pl.BlockSpec((1, GRID_W, D), lambda b, r: (b, r, 0)),
                  pl.BlockSpec((1, seq, D), lambda b, r: (b, 0, 1)),
                  pl.BlockSpec((1, seq, D), lambda b, r: (b, 0, 2)),
                  pl.BlockSpec((1, ctx_len, D), lambda b, r: (b, seq // ctx_len, 1)),
                  pl.BlockSpec((1, ctx_len, D), lambda b, r: (b, seq // ctx_len, 2)),
                  pl.BlockSpec((1, NA_HEADS, GRID_W, nk), lambda b, r: (cfg(b, r), 0, 0, 0))],
        out_specs=pl.BlockSpec((1, GRID_W, D), lambda b, r: (b, r, 0)),
        out_shape=jax.ShapeDtypeStruct((nb, seq, D), BF16),
        compiler_params=_cparams(("arbitrary", "arbitrary")),
        name="nbr_attention",
    )(qkv, qkv, qkv, qkv, qkv, bias)


def _attn_bias_table(rpb):
    q = np.arange(GRID_W)[:, None]
    c = np.arange(GRID_W)[None, :]
    ws = np.clip(q - NA_KW // 2, 0, GRID_W - NA_KW)
    inside = (c >= ws) & (c < ws + NA_KW)
    cidx = np.clip(c - q + NA_KW - 1, 0, 2 * NA_KW - 2)
    ridx = np.arange(NA_KH)[None, :] + (NA_KH - 1) - np.arange(NA_KH)[:, None]
    tab = rpb[:, ridx][:, :, :, cidx]
    tab = jnp.where(inside[None, None, None], tab.astype(F32), -jnp.inf)
    tab = tab.transpose(1, 0, 3, 2, 4)
    return tab.reshape(NA_KH, NA_HEADS, GRID_W, NA_KH * GRID_W)


def _proj_res_kernel(o_ref, h_ref, mod_ref, w_ref, out_ref):
    out_ref[0] = h_ref[0] + mod_ref[0, :, 2 * D:3 * D] * jnp.dot(o_ref[0], w_ref[...],
                                                                 preferred_element_type=F32)


def _proj_res(o3, h3, mods_l, w, seq):
    nb = o3.shape[0]
    tm = 512
    return pl.pallas_call(
        _proj_res_kernel,
        grid=(nb, seq // tm),
        in_specs=[pl.BlockSpec((1, tm, D), lambda b, i: (b, i, 0)),
                  pl.BlockSpec((1, tm, D), lambda b, i: (b, i, 0)),
                  pl.BlockSpec((1, 1, 6 * D), lambda b, i: (b, 0, 0)),
                  pl.BlockSpec((D, D), lambda b, i: (0, 0))],
        out_specs=pl.BlockSpec((1, tm, D), lambda b, i: (b, i, 0)),
        out_shape=jax.ShapeDtypeStruct((nb, seq, D), F32),
        compiler_params=_cparams(("arbitrary", "arbitrary")),
        name="attn_outproj",
    )(o3, h3, mods_l, w)


MOE_TM = 512
FF_CHUNK = 1408
ROW_TILE = 8


def _to_token_tiles(ref, val):
    n = val.shape[0]
    for s in range(ROW_TILE):
        ref[pl.ds(s, n, stride=ROW_TILE), :] = val[:, 128 * s:128 * (s + 1)]


def _from_token_tiles(ref, n):
    return jnp.concatenate([ref[pl.ds(s, n, stride=ROW_TILE), :] for s in range(ROW_TILE)], axis=-1)


def _router_kernel(h_ref, mod_ref, g_ref, rw_ref, rb_ref, a8_ref, idx_ref, gate_ref):
    a = _norm_mod(h_ref[0], g_ref[...], mod_ref[0, :, 3 * D:4 * D], mod_ref[0, :, 4 * D:5 * D])
    _to_token_tiles(a8_ref, a)
    logits = _hdot(a, rw_ref[...]) + rb_ref[...]
    lane = lax.broadcasted_iota(jnp.int32, logits.shape, 1)
    logits = jnp.where(lane < N_EXPERTS, logits, -jnp.inf)
    m1 = jnp.max(logits, axis=-1, keepdims=True)
    i1 = jnp.min(jnp.where(logits == m1, lane, 128), axis=-1, keepdims=True)
    rest = jnp.where(lane == i1, -jnp.inf, logits)
    m2 = jnp.max(rest, axis=-1, keepdims=True)
    i2 = jnp.min(jnp.where(rest == m2, lane, 128), axis=-1, keepdims=True)
    ex = jnp.exp(m2 - m1)
    idx_ref[...] = jnp.where(lane == 0, i1, jnp.where(lane == 1, i2, 0))
    gate_ref[...] = jnp.where(lane == 0, 1.0 / (1.0 + ex), jnp.where(lane == 1, ex / (1.0 + ex), 0.0))


def _router(h3, mods_l, g, rw, rb):
    nb, seq, _ = h3.shape
    tm = 512
    ni = seq // tm
    rows = nb * seq
    return pl.pallas_call(
        _router_kernel,
        grid=(nb, ni),
        in_specs=[pl.BlockSpec((1, tm, D), lambda b, i: (b, i, 0)),
                  pl.BlockSpec((1, 1, 6 * D), lambda b, i: (b, 0, 0)),
                  pl.BlockSpec((1, D), lambda b, i: (0, 0)),
                  pl.BlockSpec((D, 128), lambda b, i: (0, 0)),
                  pl.BlockSpec((1, 128), lambda b, i: (0, 0))],
        out_specs=[pl.BlockSpec((tm * ROW_TILE, 128), lambda b, i: (b * ni + i, 0)),
                   pl.BlockSpec((tm, 128), lambda b, i: (b * ni + i, 0)),
                   pl.BlockSpec((tm, 128), lambda b, i: (b * ni + i, 0))],
        out_shape=[jax.ShapeDtypeStruct((rows * ROW_TILE, 128), F32),
                   jax.ShapeDtypeStruct((rows, 128), jnp.int32),
                   jax.ShapeDtypeStruct((rows, 128), F32)],
        compiler_params=_cparams(("arbitrary", "arbitrary")),
        name="moe_router",
    )(h3, mods_l, g, rw, rb)


def _route_plan(idx, rows):
    n_pairs = 2 * rows
    n_tiles = n_pairs // MOE_TM + N_EXPERTS
    expert = jnp.concatenate([idx[:, 0], idx[:, 1]])
    onehot = (expert[:, None] == jnp.arange(N_EXPERTS)[None, :]).astype(jnp.int32)
    cum = jnp.cumsum(onehot, axis=0)
    counts = cum[-1]
    rank = jnp.sum(onehot * cum, axis=1) - 1
    padded = (counts + MOE_TM - 1) // MOE_TM * MOE_TM
    ends = jnp.cumsum(padded)
    starts = ends - padded
    pos = starts[expert] + rank
    pair = jnp.arange(n_pairs, dtype=jnp.int32)
    dst_of_row = jnp.zeros((n_tiles * MOE_TM,), jnp.int32).at[pos].set(pair)
    tok_of_row = dst_of_row % rows
    tile_start = jnp.arange(n_tiles, dtype=jnp.int32) * MOE_TM
    tile_expert = jnp.minimum(jnp.sum((tile_start[:, None] >= ends[None, :]).astype(jnp.int32), axis=1),
                              N_EXPERTS - 1)
    n_valid = jnp.clip(starts[tile_expert] + counts[tile_expert] - tile_start, 0, MOE_TM)
    n_valid = jnp.where(tile_start < ends[-1], n_valid, 0)
    return tile_expert.astype(jnp.int32), n_valid.astype(jnp.int32), tok_of_row, dst_of_row


def _token_copy(src, src_tok, dst, dst_tok, sem, n_tok=1):
    n = n_tok * ROW_TILE
    return pltpu.make_async_copy(src.at[pl.ds(pl.multiple_of(src_tok * ROW_TILE, ROW_TILE), n)],
                                 dst.at[pl.ds(pl.multiple_of(dst_tok * ROW_TILE, ROW_TILE), n)], sem)


def _expert_kernel(texp_ref, nval_ref, tok_ref, dst_ref, a8_hbm, w1_ref, w3_ref, w2_ref, y8_hbm,
                   xbuf, ybuf, gsem, ssem):
    i = pl.program_id(0)
    nv = nval_ref[i]

    @pl.when(nv > 0)
    def _():
        base = i * MOE_TM

        def gather(r, carry):
            _token_copy(a8_hbm, tok_ref[base + r], xbuf, r, gsem).start()
            return carry

        lax.fori_loop(0, MOE_TM, gather, 0)
        _token_copy(a8_hbm, 0, xbuf, 0, gsem, MOE_TM).wait()
        x = _from_token_tiles(xbuf, MOE_TM).astype(BF16)
        acc = jnp.zeros((MOE_TM, D), F32)
        for c in range(D_FF // FF_CHUNK):
            sl = slice(FF_CHUNK * c, FF_CHUNK * (c + 1))
            hid = _silu(jnp.dot(x, w1_ref[0, :, sl], preferred_element_type=F32)) * jnp.dot(
                x, w3_ref[0, :, sl], preferred_element_type=F32)
            acc = acc + jnp.dot(hid.astype(BF16), w2_ref[0, sl, :], preferred_element_type=F32)
        _to_token_tiles(ybuf, acc)

        def scatter(r, carry):
            _token_copy(ybuf, r, y8_hbm, dst_ref[base + r], ssem).start()
            return carry

        lax.fori_loop(0, nv, scatter, 0)
        pltpu.make_async_copy(ybuf.at[pl.ds(0, nv * ROW_TILE)], y8_hbm.at[pl.ds(0, nv * ROW_TILE)], ssem).wait()


def _experts(tile_expert, n_valid, tok_of_row, dst_of_row, a8, w1, w3, w2, rows):
    n_tiles = tile_expert.shape[0]
    wspec = lambda shape: pl.BlockSpec(shape, lambda i, te, nv, tk, ds: (te[i], 0, 0),
                                       pipeline_mode=pl.Buffered(1))
    return pl.pallas_call(
        _expert_kernel,
        grid_spec=pltpu.PrefetchScalarGridSpec(
            num_scalar_prefetch=4,
            grid=(n_tiles,),
            in_specs=[pl.BlockSpec(memory_space=pl.ANY),
                      wspec((1, D, D_FF)), wspec((1, D, D_FF)), wspec((1, D_FF, D))],
            out_specs=pl.BlockSpec(memory_space=pl.ANY),
            scratch_shapes=[pltpu.VMEM((MOE_TM * ROW_TILE, 128), F32), pltpu.VMEM((MOE_TM * ROW_TILE, 128), F32),
                            pltpu.SemaphoreType.DMA, pltpu.SemaphoreType.DMA]),
        out_shape=jax.ShapeDtypeStruct((2 * rows * ROW_TILE, 128), F32),
        compiler_params=_cparams(("arbitrary",)),
        name="moe_experts",
    )(tile_expert, n_valid, tok_of_row, dst_of_row, a8, w1, w3, w2)


def _combine_kernel(h_ref, mod_ref, gate_ref, y_ref, gf_ref, o_ref):
    tm = h_ref.shape[1]
    gates = gate_ref[...]
    y = gates[:, 0:1] * _from_token_tiles(y_ref.at[0], tm) + gates[:, 1:2] * _from_token_tiles(y_ref.at[1], tm)
    out = h_ref[0] + mod_ref[0, :, 5 * D:6 * D] * y
    out = out * lax.rsqrt(jnp.mean(out * out, axis=-1, keepdims=True) + NORM_EPS)
    o_ref[0] = out * gf_ref[...]


def _combine(h3, mods_l, gates, y8, gf):
    nb, seq, _ = h3.shape
    tm = 512
    ni = seq // tm
    return pl.pallas_call(
        _combine_kernel,
        grid=(nb, ni),
        in_specs=[pl.BlockSpec((1, tm, D), lambda b, i: (b, i, 0)),
                  pl.BlockSpec((1, 1, 6 * D), lambda b, i: (b, 0, 0)),
                  pl.BlockSpec((tm, 128), lambda b, i: (b * ni + i, 0)),
                  pl.BlockSpec((2, tm * ROW_TILE, 128), lambda b, i: (0, b * ni + i, 0)),
                  pl.BlockSpec((1, D), lambda b, i: (0, 0))],
        out_specs=pl.BlockSpec((1, tm, D), lambda b, i: (b, i, 0)),
        out_shape=jax.ShapeDtypeStruct((nb, seq, D), F32),
        compiler_params=_cparams(("arbitrary", "arbitrary")),
        name="moe_combine_norm",
    )(h3, mods_l, gates, y8, gf)


def _moe(h3, mods_l, g, rw, rb, w1, w3, w2, gf):
    nb, seq, _ = h3.shape
    rows = nb * seq
    a8, idx, gates = _router(h3, mods_l, g, rw, rb)
    tile_expert, n_valid, tok_of_row, dst_of_row = _route_plan(idx, rows)
    y8 = _experts(tile_expert, n_valid, tok_of_row, dst_of_row, a8, w1, w3, w2, rows)
    return _combine(h3, mods_l, gates, y8.reshape(2, rows * ROW_TILE, 128), gf)


def _pad_cols(a, n):
    return jnp.pad(a, [(0, 0)] * (a.ndim - 1) + [(0, n - a.shape[-1])])


def _mix_in_layout(w_in, rw_mu, conv_w, conv_b):
    w_rw = _pad_cols(w_in[:, :RW_COLS], RW_PAD)
    mb = w_in[:, RW_COLS:]
    w_mb = _pad_cols(jnp.concatenate([mb[:, MB_DIM:MB_DIM + MB_XBC], mb[:, :MB_DIM], mb[:, MB_DIM + MB_XBC:]],
                                     axis=1), MB_PAD)
    w = jnp.concatenate([w_rw, w_mb], axis=1).astype(BF16)
    mu_p = _pad_cols(rw_mu[0], RW_PAD)
    mu_n = _pad_cols(rw_mu[1], RW_PAD)
    zeros = jnp.zeros((RW_PAD,), F32)
    cf_rw = jnp.stack([zeros, mu_p, 1.0 - mu_p - mu_n, mu_n, zeros, zeros, zeros, zeros])
    ident = jnp.zeros((8, MB_PAD - MB_XBC), F32).at[2].set(1.0)
    cf_xbc = jnp.concatenate([conv_w, conv_b[None], jnp.zeros((2, MB_XBC), F32)], axis=0)
    cf = jnp.concatenate([cf_rw, cf_xbc, ident], axis=1)
    return w, cf


def _lora_pad(up, offset):
    out = jnp.zeros((2, 128, up.shape[-1]), F32)
    for d in range(2):
        out = out.at[d, offset + RW_LORA * d:offset + RW_LORA * (d + 1)].set(up[d])
    return out


def kernel(x, c, ctx, c_ctx, w_ada, b_ada, norm_mix, norm_ffn, norm_final, mix_w_in, mix_w_out, rw_mu, rw_w0,
           rw_w_up, rw_a0, rw_a_up, rw_g_up, rw_k_k, rw_k_a, rw_r_k, rw_gn_w, rw_gn_b, mb_conv_w, mb_conv_b,
           mb_dt_bias, mb_a_log, mb_d, mb_norm_w, ffn_w1, ffn_w3, ffn_w2, na_w_qkv, na_w_out, na_rpb,
           moe_router_w, moe_router_b, moe_w1, moe_w3, moe_w2):
    nb, seq, _ = x.shape
    ctx_len = ctx.shape[1]
    t = seq + ctx_len
    nt, nl = t // TM, seq // TM
    ntb = t // TMB
    n_lat_last = seq - TMB * (ntb - 1)
    assert seq % TM == 0 and ctx_len % TM == 0 and seq % ctx_len == 0 and nb < 16
    assert t % TMB == 0 and 0 < n_lat_last <= TMB

    cvec = jnp.zeros((16, D), F32).at[:nb].set(c).at[nb].set(c_ctx)
    mods = _ada(cvec, w_ada, b_ada).reshape(w_ada.shape[0], 16, 1, 6 * D)

    h = jnp.concatenate([x, ctx], axis=1).reshape(nb * t, D)

    w_in, cf = _mix_in_layout(mix_w_in[0], rw_mu[0], mb_conv_w[0], mb_conv_b[0])
    z = _inproj(h, mods[0], norm_mix[0][None], w_in, cf, nb, nt, nl, RW_PAD // TN, (RW_PAD + MB_XBC) // TN)
    z3 = z.reshape(nb, t, MIX_PAD)
    wlw = _lora_pad(rw_w_up[0], 0)
    wla = _lora_pad(rw_a_up[0], 2 * RW_LORA)
    w0 = rw_w0[0][:, None, :]
    a0 = rw_a0[0][:, None, :]
    yrw = _rwkv_scan(z3, w0, wlw, a0, wla, rw_k_k[0][None], rw_k_a[0][None], seq)
    dtb = _pad_cols(mb_dt_bias[0].reshape(1, 2 * MB_HEADS), 128)
    alog = _pad_cols(mb_a_log[0].reshape(1, 2 * MB_HEADS), 128)
    ymb = _ssd_scan(z3, dtb, alog, seq)
    gup = jnp.pad(rw_g_up[0], ((0, 128 - RW_LORA_G), (0, 0))).astype(BF16)
    h = _finish(z, yrw.reshape(2, nb * t, RW_DIM), ymb.reshape(2, nb * t, MB_DIM), h, mods[0], a0, wla,
                rw_k_a[0][None], rw_r_k[0].reshape(2, 1, RW_DIM), gup, rw_gn_w[0][None], rw_gn_b[0][None],
                jnp.repeat(mb_d[0], HEAD)[None], mb_norm_w[0][None], mix_w_out[0].astype(BF16), nb, nt, nl)
    h = _ffn(h, mods[0], norm_ffn[0][None], ffn_w1[0].astype(BF16), ffn_w3[0].astype(BF16),
             ffn_w2[0].astype(BF16), nb, ntb, n_lat_last)

    qkv = _nmm(h, mods[1], norm_mix[1][None], na_w_qkv[0].astype(BF16), nb, ntb, n_lat_last, BF16)
    o = _attention(qkv.reshape(nb, t, 3 * D), _attn_bias_table(na_rpb[0]), seq, ctx_len)
    mods1 = mods[1].reshape(16, 1, 6 * D)
    h3 = _proj_res(o, h.reshape(nb, t, D), mods1, na_w_out[0].astype(BF16), seq)
    rw = _pad_cols(moe_router_w[0], 128)
    rb = _pad_cols(moe_router_b[0][None], 128)
    return _moe(h3, mods1, norm_ffn[1][None], rw, rb, moe_w1[0].astype(BF16), moe_w3[0].astype(BF16),
                moe_w2[0].astype(BF16), norm_final[None])
```

```python
import functools
import math

import numpy as np
import jax
import jax.numpy as jnp
from jax import lax
from jax.experimental import pallas as pl
from jax.experimental.pallas import tpu as pltpu

F32 = jnp.float32
BF16 = jnp.bfloat16

D = 1024
NORM_EPS = 1e-6
GRID_W = 64

HEAD = 64
RW_DIM = 512
RW_HEADS = 8
RW_LORA = 32
RW_LORA_G = 96
RW_GN_EPS = 64e-5
RW_COLS = 3 * RW_DIM + 4 * RW_LORA + RW_LORA_G
RW_PAD = 1792
RW_CHUNK = 64

MB_DIM = 512
MB_HEADS = 8
MB_GROUPS = 2
MB_STATE = 128
MB_CONV = 5
MB_XBC = MB_DIM + 2 * MB_GROUPS * MB_STATE
MB_PAD = 1792
MB_CHUNK = 128

MIX_PAD = RW_PAD + MB_PAD

NA_HEADS = 16
NA_KH = 8
NA_KW = 16

D_FF = 2816
N_EXPERTS = 8

TM = 256
TMB = 768
HALO = 16
TN = 256
TF = 256
VMEM_LIMIT = 56 * 1024 * 1024


def _cparams(sem):
    return pltpu.CompilerParams(dimension_semantics=sem, vmem_limit_bytes=VMEM_LIMIT)


def _bdot(a, b):
    return jnp.dot(a.astype(BF16), b.astype(BF16), preferred_element_type=F32)


def _bdot_nt(a, b):
    return lax.dot_general(a.astype(BF16), b.astype(BF16), (((1,), (1,)), ((), ())),
                           preferred_element_type=F32)


def _hdot(a, b):
    return jnp.dot(a, b, precision=lax.Precision.HIGHEST, preferred_element_type=F32)


def _split3(x):
    p1 = x.astype(BF16)
    r1 = x - p1.astype(F32)
    p2 = r1.astype(BF16)
    return p1, p2, (r1 - p2.astype(F32)).astype(BF16)


def _dot3(a, b):
    ah, al, _ = _split3(a)
    bh, bl, _ = _split3(b)
    dot = functools.partial(jnp.dot, preferred_element_type=F32)
    return dot(ah, bh) + (dot(ah, bl) + dot(al, bh))


def _tri_cumsum(mask, x):
    tri = jnp.where(mask, 1.0, 0.0).astype(BF16)
    p1, p2, p3 = _split3(x)
    dot = functools.partial(jnp.dot, preferred_element_type=F32)
    return dot(tri, p1) + (dot(tri, p2) + dot(tri, p3))


def _bmm(spec, a, b):
    return jnp.einsum(spec, a.astype(BF16), b.astype(BF16), preferred_element_type=F32)


def _sigmoid(x):
    return 1.0 / (1.0 + jnp.exp(-x))


def _silu(x):
    return x * _sigmoid(x)


def _softplus(x):
    return jnp.maximum(x, 0.0) + jnp.log(1.0 + jnp.exp(-jnp.abs(x)))


def _norm_mod(h, g, shift, scale):
    hn = h * lax.rsqrt(jnp.mean(h * h, axis=-1, keepdims=True) + NORM_EPS)
    return (hn * g) * (1.0 + scale) + shift


def _seg64_sum(x):
    outs = []
    for p in range(x.shape[-1] // 128):
        xp = x[:, 128 * p:128 * (p + 1)]
        lo = lax.broadcasted_iota(jnp.int32, xp.shape, 1) < HEAD
        s_lo = jnp.sum(jnp.where(lo, xp, 0.0), axis=-1, keepdims=True)
        s_hi = jnp.sum(jnp.where(lo, 0.0, xp), axis=-1, keepdims=True)
        outs.append(jnp.where(lo, s_lo, s_hi))
    return jnp.concatenate(outs, axis=-1)


def _heads(x):
    return jnp.stack([x[:, HEAD * h:HEAD * (h + 1)] for h in range(x.shape[-1] // HEAD)], axis=0)


def _unheads(x):
    return jnp.concatenate([x[h] for h in range(x.shape[0])], axis=-1)


def _ada_kernel(c_ref, w_ref, b_ref, o_ref):
    o_ref[0] = _hdot(_silu(c_ref[...]), w_ref[0]) + b_ref[0]


def _ada(cvec, w_ada, b_ada):
    depth = w_ada.shape[0]
    tn = 1536
    return pl.pallas_call(
        _ada_kernel,
        grid=(depth, 6 * D // tn),
        in_specs=[pl.BlockSpec((16, D), lambda l, j: (0, 0)),
                  pl.BlockSpec((1, D, tn), lambda l, j: (l, 0, j)),
                  pl.BlockSpec((1, 1, tn), lambda l, j: (l, 0, j))],
        out_specs=pl.BlockSpec((1, 16, tn), lambda l, j: (l, 0, j)),
        out_shape=jax.ShapeDtypeStruct((depth, 16, 6 * D), F32),
        compiler_params=_cparams(("arbitrary", "arbitrary")),
        name="adaln",
    )(cvec, w_ada, b_ada.reshape(depth, 1, 6 * D))


def _mod_row(i, nt, nl, nb):
    return jnp.where(i % nt >= nl, nb, i // nt)


def _resident(shape):
    return pl.BlockSpec(shape, lambda *_: (0,) * len(shape), pipeline_mode=pl.Buffered(1))


def _inproj_kernel(hp_ref, h_ref, hn_ref, mod_ref, g_ref, w_ref, cf_ref, o_ref, a_scr, z_scr,
                   *, nt, nl, silu_lo, silu_hi):
    t = pl.program_id(0) % nt
    first = (t == 0) | (t == nl)
    last = (t == nl - 1) | (t == nt - 1)
    shift = mod_ref[0, :, 0:D]
    scale = mod_ref[0, :, D:2 * D]
    g = g_ref[...]
    a_scr[0:HALO] = jnp.where(first, 0.0, _norm_mod(hp_ref[...], g, shift, scale)).astype(BF16)
    a_scr[HALO:HALO + TM] = _norm_mod(h_ref[...], g, shift, scale).astype(BF16)
    a_scr[HALO + TM:] = jnp.where(last, 0.0, _norm_mod(hn_ref[...], g, shift, scale)).astype(BF16)
    a = a_scr[...]
    for j in range(w_ref.shape[1] // TN):
        sl = slice(TN * j, TN * (j + 1))
        z = z_scr.at[j % 2]
        z[...] = jnp.dot(a, w_ref[:, sl], preferred_element_type=F32)

        def tap(k):
            return cf_ref[k:k + 1, sl] * z[HALO - 2 + k:HALO - 2 + k + TM, :]

        if j < silu_lo:
            o_ref[:, sl] = tap(1) + tap(2) + tap(3)
        elif j < silu_hi:
            o_ref[:, sl] = _silu(cf_ref[5:6, sl] + tap(0) + tap(1) + tap(2) + tap(3) + tap(4))
        else:
            o_ref[:, sl] = z[HALO:HALO + TM, :]


def _inproj(h2, mods_l, g, w, cf, nb, nt, nl, silu_lo, silu_hi):
    rows = h2.shape[0]
    n = w.shape[1]
    hb = TM // HALO
    return pl.pallas_call(
        functools.partial(_inproj_kernel, nt=nt, nl=nl, silu_lo=silu_lo, silu_hi=silu_hi),
        grid=(rows // TM,),
        in_specs=[pl.BlockSpec((HALO, D), lambda i: (jnp.maximum(i * hb - 1, 0), 0)),
                  pl.BlockSpec((TM, D), lambda i: (i, 0)),
                  pl.BlockSpec((HALO, D), lambda i: (jnp.minimum((i + 1) * hb, rows // HALO - 1), 0)),
                  pl.BlockSpec((1, 1, 6 * D), lambda i: (_mod_row(i, nt, nl, nb), 0, 0)),
                  _resident((1, D)), _resident((D, n)), _resident((8, n))],
        out_specs=pl.BlockSpec((TM, n), lambda i: (i, 0)),
        out_shape=jax.ShapeDtypeStruct((rows, n), F32),
        scratch_shapes=[pltpu.VMEM((TM + 2 * HALO, D), BF16), pltpu.VMEM((2, TM + 2 * HALO, TN), F32)],
        compiler_params=_cparams(("arbitrary",)),
        name="mix_inproj",
    )(h2, h2, h2, mods_l, g, w, cf)


def _ctx_rows(ntb, n_lat_last, tm):
    row = lax.broadcasted_iota(jnp.int32, (tm, 1), 0)
    return (pl.program_id(0) % ntb == ntb - 1) & (row >= n_lat_last)


def _row_mod(mod_ref, modc_ref, lo, is_ctx):
    return jnp.where(is_ctx, modc_ref[0, :, lo:lo + D], mod_ref[0, :, lo:lo + D])


def _big_tile_specs(nb, ntb):
    return [pl.BlockSpec((TMB, D), lambda i: (i, 0)),
            pl.BlockSpec((1, 1, 6 * D), lambda i: (i // ntb, 0, 0)),
            pl.BlockSpec((1, 1, 6 * D), lambda i: (nb, 0, 0)),
            _resident((1, D))]


def _nmm_kernel(h_ref, mod_ref, modc_ref, g_ref, w_ref, o_ref, *, ntb, n_lat_last):
    is_ctx = _ctx_rows(ntb, n_lat_last, TMB)
    a = _norm_mod(h_ref[...], g_ref[...], _row_mod(mod_ref, modc_ref, 0, is_ctx),
                  _row_mod(mod_ref, modc_ref, D, is_ctx)).astype(BF16)
    tn = 2 * TN
    for j in range(w_ref.shape[1] // tn):
        sl = slice(tn * j, tn * (j + 1))
        o_ref[:, sl] = jnp.dot(a, w_ref[:, sl], preferred_element_type=F32).astype(o_ref.dtype)


def _nmm(h2, mods_l, g, w, nb, ntb, n_lat_last, out_dtype):
    rows = h2.shape[0]
    n = w.shape[1]
    return pl.pallas_call(
        functools.partial(_nmm_kernel, ntb=ntb, n_lat_last=n_lat_last),
        grid=(rows // TMB,),
        in_specs=_big_tile_specs(nb, ntb) + [_resident((D, n))],
        out_specs=pl.BlockSpec((TMB, n), lambda i: (i, 0)),
        out_shape=jax.ShapeDtypeStruct((rows, n), out_dtype),
        compiler_params=_cparams(("arbitrary",)),
        name="norm_proj",
    )(h2, mods_l, mods_l, g, w)


def _rwkv_kernel(zf_ref, zb_ref, w0_ref, wlw_ref, a0_ref, wla_ref, kk_ref, ka_ref, yf_ref, yb_ref, s_scr):
    @pl.when(pl.program_id(1) == 0)
    def _():
        s_scr[...] = jnp.zeros_like(s_scr)

    res = [_rwkv_chunk(z_ref[0], w0_ref[d], wlw_ref[d], a0_ref[d], wla_ref[d], kk_ref[...], ka_ref[...],
                       s_scr[d], backward=d == 1) for d, z_ref in enumerate((zf_ref, zb_ref))]
    for d, y_ref in enumerate((yf_ref, yb_ref)):
        y_ref[0] = res[d][0]
        s_scr[d] = res[d][1]


def _rwkv_chunk(z, w0, wlw, a0, wla, kkw, kaw, s0, backward):
    L = RW_CHUNK
    r = z[:, 0:RW_DIM]
    k = z[:, RW_DIM:2 * RW_DIM]
    v = z[:, 2 * RW_DIM:3 * RW_DIM]
    lora = z[:, 3 * RW_DIM:3 * RW_DIM + 128]
    logw = w0 + _dot3(jnp.tanh(lora), wlw)
    logdec = -math.exp(-0.5) * _sigmoid(logw)
    iclr = _sigmoid(a0 + _dot3(lora, wla))
    kk = k * kkw
    kk = kk / jnp.maximum(jnp.sqrt(_seg64_sum(kk * kk)), 1e-12)
    kdir = k * (1.0 + (iclr - 1.0) * kaw)
    bvec = kk * iclr

    row = lax.broadcasted_iota(jnp.int32, (L, L), 0)
    col = lax.broadcasted_iota(jnp.int32, (L, L), 1)
    ahead = col - row if backward else row - col
    incl = ahead >= 0
    strict = ahead > 0
    lc = _tri_cumsum(incl, logdec)
    ltot = jnp.sum(logdec, axis=0, keepdims=True)
    g_in = jnp.exp(lc)
    g_ex = jnp.exp(lc - logdec)
    g_inv = jnp.exp(-lc)
    g_tail = jnp.exp(ltot - lc)

    ar = _heads(jnp.concatenate([-kk * g_ex, r * g_in], axis=0))
    bk = _heads(jnp.concatenate([bvec * g_inv, kdir * g_inv], axis=0))
    bk_tail = _heads(jnp.concatenate([bvec * g_tail, kdir * g_tail], axis=0))
    vh = _heads(v)

    m1 = _bmm('hlk,hsk->hls', ar, bk)
    nmat = jnp.where(strict, m1[:, :L, :L], 0.0)
    a_ak = jnp.where(strict, m1[:, :L, L:], 0.0)
    m_r = jnp.concatenate([jnp.where(incl, m1[:, L:, :L], 0.0), jnp.where(incl, m1[:, L:, L:], 0.0)], axis=2)
    m2 = _bmm('hlk,hvk->hlv', ar, s0)
    x = m2[:, :L] + _bmm('hls,hsv->hlv', a_ak, vh)
    p = nmat
    steps = int(math.log2(L))
    for i in range(steps):
        x = x + _bmm('hls,hsv->hlv', p, x)
        if i < steps - 1:
            p = _bmm('hls,hst->hlt', p, p)
    uv = jnp.concatenate([x, vh], axis=1)
    y = m2[:, L:] + _bmm('hls,hsv->hlv', m_r, uv)
    g_tot = _heads(jnp.exp(ltot))
    return _unheads(y), s0 * g_tot + _bmm('hvl,hlk->hvk', jnp.swapaxes(uv, 1, 2), bk_tail)


def _scan_chunks(nc, ncl, col_block=0):
    return (lambda b, c: (b, (c + ncl) % nc, col_block)), (lambda b, c: (b, nc - 1 - c, col_block))


def _rwkv_scan(z3, w0, wlw, a0, wla, kkw, kaw, seq):
    nb, t, _ = z3.shape
    nc = t // RW_CHUNK
    fwd, bwd = _scan_chunks(nc, seq // RW_CHUNK)
    y_shape = jax.ShapeDtypeStruct((nb, t, RW_DIM), F32)
    return pl.pallas_call(
        _rwkv_kernel,
        grid=(nb, nc),
        in_specs=[pl.BlockSpec((1, RW_CHUNK, RW_PAD), fwd), pl.BlockSpec((1, RW_CHUNK, RW_PAD), bwd),
                  _resident((2, 1, RW_DIM)), _resident((2, 128, RW_DIM)), _resident((2, 1, RW_DIM)),
                  _resident((2, 128, RW_DIM)), _resident((1, RW_DIM)), _resident((1, RW_DIM))],
        out_specs=[pl.BlockSpec((1, RW_CHUNK, RW_DIM), fwd), pl.BlockSpec((1, RW_CHUNK, RW_DIM), bwd)],
        out_shape=[y_shape, y_shape],
        scratch_shapes=[pltpu.VMEM((2, RW_HEADS, HEAD, HEAD), F32)],
        compiler_params=_cparams(("arbitrary", "arbitrary")),
        name="rwkv7_scan",
    )(z3, z3, w0, wlw, a0, wla, kkw, kaw)


def _ssd_kernel(zf_ref, zb_ref, dtb_ref, alog_ref, yf_ref, yb_ref, s_scr):
    @pl.when(pl.program_id(1) == 0)
    def _():
        s_scr[...] = jnp.zeros_like(s_scr)

    res = [_ssd_chunk(z_ref[0], dtb_ref[...], alog_ref[...], s_scr[d], d) for d, z_ref in enumerate((zf_ref, zb_ref))]
    for d, y_ref in enumerate((yf_ref, yb_ref)):
        y_ref[0] = res[d][0]
        s_scr[d] = res[d][1]


def _ssd_chunk(z, dtb, alog, s_all, d):
    L = MB_CHUNK
    xm = z[:, 0:MB_DIM]
    dt_all = _softplus(z[:, 1536:1664] + dtb)
    a_all = dt_all * (-jnp.exp(alog))
    row = lax.broadcasted_iota(jnp.int32, (L, L), 0)
    col = lax.broadcasted_iota(jnp.int32, (L, L), 1)
    incl = (col >= row) if d == 1 else (row >= col)
    cs = _tri_cumsum(incl, a_all)
    cs_t = cs.T
    tot = jnp.sum(a_all, axis=0, keepdims=True)
    gmat = []
    for g in range(MB_GROUPS):
        bg = z[:, 512 + 128 * g:640 + 128 * g]
        cg = z[:, 768 + 128 * g:896 + 128 * g]
        gmat.append((bg, cg, _bdot_nt(cg, bg)))
    outs, states = [], []
    for h in range(MB_HEADS):
        bg, cg, cb = gmat[h // (MB_HEADS // MB_GROUPS)]
        j = MB_HEADS * d + h
        cs_col = cs[:, j:j + 1]
        tot_h = tot[:, j:j + 1]
        lmat = jnp.exp(jnp.where(incl, cs_col - cs_t[j:j + 1, :], -jnp.inf))
        xh = xm[:, HEAD * h:HEAD * (h + 1)] * dt_all[:, j:j + 1]
        s0 = s_all[h]
        outs.append(_bdot(cb * lmat, xh) + jnp.exp(cs_col) * _bdot_nt(cg, s0))
        xd = xh * jnp.exp(tot_h - cs_col)
        states.append(s0 * jnp.exp(tot_h) + _bdot(xd.T, bg))
    return jnp.concatenate(outs, axis=-1), jnp.stack(states, axis=0)


def _ssd_scan(z3, dtb, alog, seq):
    nb, t, _ = z3.shape
    nc = t // MB_CHUNK
    fwd, bwd = _scan_chunks(nc, seq // MB_CHUNK, col_block=1)
    yf, yb = _scan_chunks(nc, seq // MB_CHUNK)
    y_shape = jax.ShapeDtypeStruct((nb, t, MB_DIM), F32)
    return pl.pallas_call(
        _ssd_kernel,
        grid=(nb, nc),
        in_specs=[pl.BlockSpec((1, MB_CHUNK, MB_PAD), fwd), pl.BlockSpec((1, MB_CHUNK, MB_PAD), bwd),
                  _resident((1, 128)), _resident((1, 128))],
        out_specs=[pl.BlockSpec((1, MB_CHUNK, MB_DIM), yf), pl.BlockSpec((1, MB_CHUNK, MB_DIM), yb)],
        out_shape=[y_shape, y_shape],
        scratch_shapes=[pltpu.VMEM((2, MB_HEADS, HEAD, MB_STATE), F32)],
        compiler_params=_cparams(("arbitrary", "arbitrary")),
        name="ssd_scan",
    )(z3, z3, dtb, alog)


def _finish_kernel(zr_ref, zm_ref, yrf_ref, yrb_ref, ymf_ref, ymb_ref, h_ref, mod_ref, a0_ref, wla_ref, ka_ref,
                   rk_ref, gup_ref,
                   gnw_ref, gnb_ref, mbd_ref, mbn_ref, wout_ref, o_ref):
    zr = zr_ref[...]
    r = zr[:, 0:RW_DIM]
    k = zr[:, RW_DIM:2 * RW_DIM]
    v = zr[:, 2 * RW_DIM:3 * RW_DIM]
    lora = zr[:, 3 * RW_DIM:3 * RW_DIM + 128]
    gate = _bdot(_sigmoid(zr[:, 3 * RW_DIM + 128:3 * RW_DIM + 256]), gup_ref[...])
    bonus = jnp.zeros_like(r)
    for dd in range(2):
        iclr = _sigmoid(a0_ref[dd] + _hdot(lora, wla_ref[dd]))
        bonus = bonus + r * (k * (1.0 + (iclr - 1.0) * ka_ref[...])) * rk_ref[dd]
    bonus = _seg64_sum(bonus) * v
    y = yrf_ref[...] + yrb_ref[...]
    mu = _seg64_sum(y) * (1.0 / HEAD)
    yc = y - mu
    var = _seg64_sum(yc * yc) * (1.0 / HEAD)
    y = yc * lax.rsqrt(var + RW_GN_EPS) * gnw_ref[...] + gnb_ref[...]
    o_rw = (y + bonus) * gate

    zm = zm_ref[...]
    xm = zm[:, 0:MB_DIM]
    zg = zm[:, 1024:1536]
    ym = (ymf_ref[...] + ymb_ref[...] + mbd_ref[...] * xm) * _silu(zg)
    gw = MB_DIM // MB_GROUPS
    parts = []
    for g in range(MB_GROUPS):
        yg = ym[:, gw * g:gw * (g + 1)]
        parts.append(yg * lax.rsqrt(jnp.mean(yg * yg, axis=-1, keepdims=True) + NORM_EPS))
    o_mb = jnp.concatenate(parts, axis=-1) * mbn_ref[...]
    o = jnp.concatenate([o_rw, o_mb], axis=-1)
    o_ref[...] = h_ref[...] + mod_ref[0, :, 2 * D:3 * D] * _bdot(o, wout_ref[...])


def _finish(z2, yrw, ymb, h2, mods_l, a0, wla, kaw, rk, gup, gnw, gnb, mbd, mbn, wout, nb, nt, nl):
    rows = h2.shape[0]
    full = lambda *shape: _resident(shape)
    half = pl.BlockSpec((TM, RW_DIM), lambda i: (i, 0))
    return pl.pallas_call(
        _finish_kernel,
        grid=(rows // TM,),
        in_specs=[pl.BlockSpec((TM, RW_PAD), lambda i: (i, 0)),
                  pl.BlockSpec((TM, MB_PAD), lambda i: (i, 1)),
                  half, half, half, half,
                  pl.BlockSpec((TM, D), lambda i: (i, 0)),
                  pl.BlockSpec((1, 1, 6 * D), lambda i: (_mod_row(i, nt, nl, nb), 0, 0)),
                  full(2, 1, RW_DIM), full(2, 128, RW_DIM), full(1, RW_DIM), full(2, 1, RW_DIM),
                  full(128, RW_DIM), full(1, RW_DIM), full(1, RW_DIM), full(1, MB_DIM), full(1, MB_DIM),
                  full(D, D)],
        out_specs=pl.BlockSpec((TM, D), lambda i: (i, 0)),
        out_shape=jax.ShapeDtypeStruct((rows, D), F32),
        compiler_params=_cparams(("arbitrary",)),
        name="mix_finish",
    )(z2, z2, yrw[0], yrw[1], ymb[0], ymb[1], h2, mods_l, a0, wla, kaw, rk, gup, gnw, gnb, mbd, mbn, wout)


def _ffn_kernel(h_ref, mod_ref, modc_ref, g_ref, w1_ref, w3_ref, w2_ref, o_ref, *, ntb, n_lat_last):
    is_ctx = _ctx_rows(ntb, n_lat_last, TMB)
    h = h_ref[...]
    a = _norm_mod(h, g_ref[...], _row_mod(mod_ref, modc_ref, 3 * D, is_ctx),
                  _row_mod(mod_ref, modc_ref, 4 * D, is_ctx)).astype(BF16)
    acc = jnp.zeros((TMB, D), F32)
    for c in range(D_FF // TF):
        sl = slice(TF * c, TF * (c + 1))
        hid = _silu(jnp.dot(a, w1_ref[:, sl], preferred_element_type=F32)) * jnp.dot(
            a, w3_ref[:, sl], preferred_element_type=F32)
        acc = acc + jnp.dot(hid.astype(BF16), w2_ref[sl, :], preferred_element_type=F32)
    o_ref[...] = h + _row_mod(mod_ref, modc_ref, 5 * D, is_ctx) * acc


def _ffn(h2, mods_l, g, w1, w3, w2, nb, ntb, n_lat_last):
    rows = h2.shape[0]
    return pl.pallas_call(
        functools.partial(_ffn_kernel, ntb=ntb, n_lat_last=n_lat_last),
        grid=(rows // TMB,),
        in_specs=_big_tile_specs(nb, ntb) + [_resident((D, D_FF)), _resident((D, D_FF)), _resident((D_FF, D))],
        out_specs=pl.BlockSpec((TMB, D), lambda i: (i, 0)),
        out_shape=jax.ShapeDtypeStruct((rows, D), F32),
        compiler_params=_cparams(("arbitrary",)),
        name="ffn_swiglu",
    )(h2, mods_l, mods_l, g, w1, w3, w2)


def _attn_kernel(q_ref, k_ref, v_ref, kc_ref, vc_ref, bias_ref, o_ref, *, n_rows):
    r = pl.program_id(1)
    rs = jnp.clip(r - NA_KH // 2, 0, n_rows - NA_KH)
    start = pl.multiple_of(rs * GRID_W, GRID_W)
    nk = NA_KH * GRID_W
    for h in range(NA_HEADS):
        sl = slice(HEAD * h, HEAD * (h + 1))
        q = q_ref[0, :, sl]
        kh = k_ref[0, pl.ds(start, nk), sl]
        vh = v_ref[0, pl.ds(start, nk), sl]
        s_lat = _bdot_nt(q, kh) * (HEAD ** -0.5) + bias_ref[0, h]
        s_ctx = _bdot_nt(q, kc_ref[0, :, sl]) * (HEAD ** -0.5)
        m = jnp.maximum(jnp.max(s_lat, axis=-1, keepdims=True), jnp.max(s_ctx, axis=-1, keepdims=True))
        p_lat = jnp.exp(s_lat - m)
        p_ctx = jnp.exp(s_ctx - m)
        den = jnp.sum(p_lat, axis=-1, keepdims=True) + jnp.sum(p_ctx, axis=-1, keepdims=True)
        o = _bdot(p_lat, vh) + _bdot(p_ctx, vc_ref[0, :, sl])
        o_ref[0, :, sl] = (o / den).astype(o_ref.dtype)


def _attention(qkv, bias, seq, ctx_len):
    nb = qkv.shape[0]
    n_rows = seq // GRID_W
    nk = NA_KH * GRID_W

    def cfg(b, r):
        return r - jnp.clip(r - NA_KH // 2, 0, n_rows - NA_KH)

    return pl.pallas_call(
        functools.partial(_attn_kernel, n_rows=n_rows),
        grid=(nb, n_rows),
        in_specs=[pl.BlockSpec((1, GRID_W, D), lambda b, r: (b, r, 0)),
                  pl.BlockSpec((1, seq, D), lambda b, r: (b, 0, 1)),
                  pl.BlockSpec((1, seq, D), lambda b, r: (b, 0, 2)),
                  pl.BlockSpec((1, ctx_len, D), lambda b, r: (b, seq // ctx_len, 1)),
                  pl.BlockSpec((1, ctx_len, D), lambda b, r: (b, seq // ctx_len, 2)),
                  pl.BlockSpec((1, NA_HEADS, GRID_W, nk), lambda b, r: (cfg(b, r), 0, 0, 0))],
        out_specs=pl.BlockSpec((1, GRID_W, D), lambda b, r: (b, r, 0)),
        out_shape=jax.ShapeDtypeStruct((nb, seq, D), BF16),
        compiler_params=_cparams(("arbitrary", "arbitrary")),
        name="nbr_attention",
    )(qkv, qkv, qkv, qkv, qkv, bias)


def _attn_bias_table(rpb):
    q = np.arange(GRID_W)[:, None]
    c = np.arange(GRID_W)[None, :]
    ws = np.clip(q - NA_KW // 2, 0, GRID_W - NA_KW)
    inside = (c >= ws) & (c < ws + NA_KW)
    cidx = np.clip(c - q + NA_KW - 1, 0, 2 * NA_KW - 2)
    ridx = np.arange(NA_KH)[None, :] + (NA_KH - 1) - np.arange(NA_KH)[:, None]
    tab = rpb[:, ridx][:, :, :, cidx]
    tab = jnp.where(inside[None, None, None], tab.astype(F32), -jnp.inf)
    tab = tab.transpose(1, 0, 3, 2, 4)
    return tab.reshape(NA_KH, NA_HEADS, GRID_W, NA_KH * GRID_W)


def _proj_res_kernel(o_ref, h_ref, mod_ref, w_ref, out_ref):
    out_ref[0] = h_ref[0] + mod_ref[0, :, 2 * D:3 * D] * jnp.dot(o_ref[0], w_ref[...],
                                                                 preferred_element_type=F32)


def _proj_res(o3, h3, mods_l, w, seq):
    nb = o3.shape[0]
    tm = 512
    return pl.pallas_call(
        _proj_res_kernel,
        grid=(nb, seq // tm),
        in_specs=[pl.BlockSpec((1, tm, D), lambda b, i: (b, i, 0)),
                  pl.BlockSpec((1, tm, D), lambda b, i: (b, i, 0)),
                  pl.BlockSpec((1, 1, 6 * D), lambda b, i: (b, 0, 0)),
                  pl.BlockSpec((D, D), lambda b, i: (0, 0))],
        out_specs=pl.BlockSpec((1, tm, D), lambda b, i: (b, i, 0)),
        out_shape=jax.ShapeDtypeStruct((nb, seq, D), F32),
        compiler_params=_cparams(("arbitrary", "arbitrary")),
        name="attn_outproj",
    )(o3, h3, mods_l, w)


MOE_TM = 512
FF_CHUNK = 1408
ROW_TILE = 8


def _to_token_tiles(ref, val):
    n = val.shape[0]
    for s in range(ROW_TILE):
        ref[pl.ds(s, n, stride=ROW_TILE), :] = val[:, 128 * s:128 * (s + 1)]


def _from_token_tiles(ref, n):
    return jnp.concatenate([ref[pl.ds(s, n, stride=ROW_TILE), :] for s in range(ROW_TILE)], axis=-1)


def _router_kernel(h_ref, mod_ref, g_ref, rw_ref, rb_ref, a8_ref, idx_ref, gate_ref):
    a = _norm_mod(h_ref[0], g_ref[...], mod_ref[0, :, 3 * D:4 * D], mod_ref[0, :, 4 * D:5 * D])
    _to_token_tiles(a8_ref, a)
    logits = _hdot(a, rw_ref[...]) + rb_ref[...]
    lane = lax.broadcasted_iota(jnp.int32, logits.shape, 1)
    logits = jnp.where(lane < N_EXPERTS, logits, -jnp.inf)
    m1 = jnp.max(logits, axis=-1, keepdims=True)
    i1 = jnp.min(jnp.where(logits == m1, lane, 128), axis=-1, keepdims=True)
    rest = jnp.where(lane == i1, -jnp.inf, logits)
    m2 = jnp.max(rest, axis=-1, keepdims=True)
    i2 = jnp.min(jnp.where(rest == m2, lane, 128), axis=-1, keepdims=True)
    ex = jnp.exp(m2 - m1)
    idx_ref[...] = jnp.where(lane == 0, i1, jnp.where(lane == 1, i2, 0))
    gate_ref[...] = jnp.where(lane == 0, 1.0 / (1.0 + ex), jnp.where(lane == 1, ex / (1.0 + ex), 0.0))


def _router(h3, mods_l, g, rw, rb):
    nb, seq, _ = h3.shape
    tm = 512
    ni = seq // tm
    rows = nb * seq
    return pl.pallas_call(
        _router_kernel,
        grid=(nb, ni),
        in_specs=[pl.BlockSpec((1, tm, D), lambda b, i: (b, i, 0)),
                  pl.BlockSpec((1, 1, 6 * D), lambda b, i: (b, 0, 0)),
                  pl.BlockSpec((1, D), lambda b, i: (0, 0)),
                  pl.BlockSpec((D, 128), lambda b, i: (0, 0)),
                  pl.BlockSpec((1, 128), lambda b, i: (0, 0))],
        out_specs=[pl.BlockSpec((tm * ROW_TILE, 128), lambda b, i: (b * ni + i, 0)),
                   pl.BlockSpec((tm, 128), lambda b, i: (b * ni + i, 0)),
                   pl.BlockSpec((tm, 128), lambda b, i: (b * ni + i, 0))],
        out_shape=[jax.ShapeDtypeStruct((rows * ROW_TILE, 128), F32),
                   jax.ShapeDtypeStruct((rows, 128), jnp.int32),
                   jax.ShapeDtypeStruct((rows, 128), F32)],
        compiler_params=_cparams(("arbitrary", "arbitrary")),
        name="moe_router",
    )(h3, mods_l, g, rw, rb)


def _route_plan(idx, rows):
    n_pairs = 2 * rows
    n_tiles = n_pairs // MOE_TM + N_EXPERTS
    expert = jnp.concatenate([idx[:, 0], idx[:, 1]])
    onehot = (expert[:, None] == jnp.arange(N_EXPERTS)[None, :]).astype(jnp.int32)
    cum = jnp.cumsum(onehot, axis=0)
    counts = cum[-1]
    rank = jnp.sum(onehot * cum, axis=1) - 1
    padded = (counts + MOE_TM - 1) // MOE_TM * MOE_TM
    ends = jnp.cumsum(padded)
    starts = ends - padded
    pos = starts[expert] + rank
    pair = jnp.arange(n_pairs, dtype=jnp.int32)
    dst_of_row = jnp.zeros((n_tiles * MOE_TM,), jnp.int32).at[pos].set(pair)
    tok_of_row = dst_of_row % rows
    tile_start = jnp.arange(n_tiles, dtype=jnp.int32) * MOE_TM
    tile_expert = jnp.minimum(jnp.sum((tile_start[:, None] >= ends[None, :]).astype(jnp.int32), axis=1),
                              N_EXPERTS - 1)
    n_valid = jnp.clip(starts[tile_expert] + counts[tile_expert] - tile_start, 0, MOE_TM)
    n_valid = jnp.where(tile_start < ends[-1], n_valid, 0)
    return tile_expert.astype(jnp.int32), n_valid.astype(jnp.int32), tok_of_row, dst_of_row


def _token_copy(src, src_tok, dst, dst_tok, sem, n_tok=1):
    n = n_tok * ROW_TILE
    return pltpu.make_async_copy(src.at[pl.ds(pl.multiple_of(src_tok * ROW_TILE, ROW_TILE), n)],
                                 dst.at[pl.ds(pl.multiple_of(dst_tok * ROW_TILE, ROW_TILE), n)], sem)


def _expert_kernel(texp_ref, nval_ref, tok_ref, dst_ref, a8_hbm, w1_ref, w3_ref, w2_ref, y8_hbm,
                   xbuf, ybuf, gsem, ssem):
    i = pl.program_id(0)
    nv = nval_ref[i]
    nxt = jnp.minimum(i + 1, pl.num_programs(0) - 1)
    has_next = (i + 1 < pl.num_programs(0)) & (nval_ref[nxt] > 0)

    def start_gather(tile, slot):
        def body(r, carry):
            _token_copy(a8_hbm, tok_ref[tile * MOE_TM + r], xbuf.at[slot], r, gsem.at[slot]).start()
            return carry

        lax.fori_loop(0, MOE_TM, body, 0)

    def wait_scatter(n):
        pltpu.make_async_copy(ybuf.at[pl.ds(0, n * ROW_TILE)], y8_hbm.at[pl.ds(0, n * ROW_TILE)], ssem).wait()

    @pl.when(nv > 0)
    def _():
        slot = i % 2

        @pl.when(i == 0)
        def _():
            start_gather(0, 0)

        _token_copy(a8_hbm, 0, xbuf.at[slot], 0, gsem.at[slot], MOE_TM).wait()

        @pl.when(has_next)
        def _():
            start_gather(nxt, 1 - slot)

        x = _from_token_tiles(xbuf.at[slot], MOE_TM).astype(BF16)
        acc = jnp.zeros((MOE_TM, D), F32)
        for c in range(D_FF // FF_CHUNK):
            sl = slice(FF_CHUNK * c, FF_CHUNK * (c + 1))
            hid = _silu(jnp.dot(x, w1_ref[0, :, sl], preferred_element_type=F32)) * jnp.dot(
                x, w3_ref[0, :, sl], preferred_element_type=F32)
            acc = acc + jnp.dot(hid.astype(BF16), w2_ref[0, sl, :], preferred_element_type=F32)

        @pl.when(i > 0)
        def _():
            wait_scatter(nval_ref[jnp.maximum(i - 1, 0)])

        _to_token_tiles(ybuf, acc)

        def scatter(r, carry):
            _token_copy(ybuf, r, y8_hbm, dst_ref[i * MOE_TM + r], ssem).start()
            return carry

        lax.fori_loop(0, nv, scatter, 0)

        @pl.when(jnp.logical_not(has_next))
        def _():
            wait_scatter(nv)


def _experts(tile_expert, n_valid, tok_of_row, dst_of_row, a8, w1, w3, w2, rows):
    n_tiles = tile_expert.shape[0]
    wspec = lambda shape: pl.BlockSpec(shape, lambda i, te, nv, tk, ds: (te[i], 0, 0),
                                       pipeline_mode=pl.Buffered(1))
    return pl.pallas_call(
        _expert_kernel,
        grid_spec=pltpu.PrefetchScalarGridSpec(
            num_scalar_prefetch=4,
            grid=(n_tiles,),
            in_specs=[pl.BlockSpec(memory_space=pl.ANY),
                      wspec((1, D, D_FF)), wspec((1, D, D_FF)), wspec((1, D_FF, D))],
            out_specs=pl.BlockSpec(memory_space=pl.ANY),
            scratch_shapes=[pltpu.VMEM((2, MOE_TM * ROW_TILE, 128), F32), pltpu.VMEM((MOE_TM * ROW_TILE, 128), F32),
                            pltpu.SemaphoreType.DMA((2,)), pltpu.SemaphoreType.DMA]),
        out_shape=jax.ShapeDtypeStruct((2 * rows * ROW_TILE, 128), F32),
        compiler_params=_cparams(("arbitrary",)),
        name="moe_experts",
    )(tile_expert, n_valid, tok_of_row, dst_of_row, a8, w1, w3, w2)


def _combine_kernel(h_ref, mod_ref, gate_ref, y_ref, gf_ref, o_ref):
    tm = h_ref.shape[1]
    gates = gate_ref[...]
    y = gates[:, 0:1] * _from_token_tiles(y_ref.at[0], tm) + gates[:, 1:2] * _from_token_tiles(y_ref.at[1], tm)
    out = h_ref[0] + mod_ref[0, :, 5 * D:6 * D] * y
    out = out * lax.rsqrt(jnp.mean(out * out, axis=-1, keepdims=True) + NORM_EPS)
    o_ref[0] = out * gf_ref[...]


def _combine(h3, mods_l, gates, y8, gf):
    nb, seq, _ = h3.shape
    tm = 512
    ni = seq // tm
    return pl.pallas_call(
        _combine_kernel,
        grid=(nb, ni),
        in_specs=[pl.BlockSpec((1, tm, D), lambda b, i: (b, i, 0)),
                  pl.BlockSpec((1, 1, 6 * D), lambda b, i: (b, 0, 0)),
                  pl.BlockSpec((tm, 128), lambda b, i: (b * ni + i, 0)),
                  pl.BlockSpec((2, tm * ROW_TILE, 128), lambda b, i: (0, b * ni + i, 0)),
                  pl.BlockSpec((1, D), lambda b, i: (0, 0))],
        out_specs=pl.BlockSpec((1, tm, D), lambda b, i: (b, i, 0)),
        out_shape=jax.ShapeDtypeStruct((nb, seq, D), F32),
        compiler_params=_cparams(("arbitrary", "arbitrary")),
        name="moe_combine_norm",
    )(h3, mods_l, gates, y8, gf)


def _moe(h3, mods_l, g, rw, rb, w1, w3, w2, gf):
    nb, seq, _ = h3.shape
    rows = nb * seq
    a8, idx, gates = _router(h3, mods_l, g, rw, rb)
    tile_expert, n_valid, tok_of_row, dst_of_row = _route_plan(idx, rows)
    y8 = _experts(tile_expert, n_valid, tok_of_row, dst_of_row, a8, w1, w3, w2, rows)
    return _combine(h3, mods_l, gates, y8.reshape(2, rows * ROW_TILE, 128), gf)


def _pad_cols(a, n):
    return jnp.pad(a, [(0, 0)] * (a.ndim - 1) + [(0, n - a.shape[-1])])


def _mix_in_layout(w_in, rw_mu, conv_w, conv_b):
    w_rw = _pad_cols(w_in[:, :RW_COLS], RW_PAD)
    mb = w_in[:, RW_COLS:]
    w_mb = _pad_cols(jnp.concatenate([mb[:, MB_DIM:MB_DIM + MB_XBC], mb[:, :MB_DIM], mb[:, MB_DIM + MB_XBC:]],
                                     axis=1), MB_PAD)
    w = jnp.concatenate([w_rw, w_mb], axis=1).astype(BF16)
    mu_p = _pad_cols(rw_mu[0], RW_PAD)
    mu_n = _pad_cols(rw_mu[1], RW_PAD)
    zeros = jnp.zeros((RW_PAD,), F32)
    cf_rw = jnp.stack([zeros, mu_p, 1.0 - mu_p - mu_n, mu_n, zeros, zeros, zeros, zeros])
    ident = jnp.zeros((8, MB_PAD - MB_XBC), F32).at[2].set(1.0)
    cf_xbc = jnp.concatenate([conv_w, conv_b[None], jnp.zeros((2, MB_XBC), F32)], axis=0)
    cf = jnp.concatenate([cf_rw, cf_xbc, ident], axis=1)
    return w, cf


def _lora_pad(up, offset):
    out = jnp.zeros((2, 128, up.shape[-1]), F32)
    for d in range(2):
        out = out.at[d, offset + RW_LORA * d:offset + RW_LORA * (d + 1)].set(up[d])
    return out


def kernel(x, c, ctx, c_ctx, w_ada, b_ada, norm_mix, norm_ffn, norm_final, mix_w_in, mix_w_out, rw_mu, rw_w0,
           rw_w_up, rw_a0, rw_a_up, rw_g_up, rw_k_k, rw_k_a, rw_r_k, rw_gn_w, rw_gn_b, mb_conv_w, mb_conv_b,
           mb_dt_bias, mb_a_log, mb_d, mb_norm_w, ffn_w1, ffn_w3, ffn_w2, na_w_qkv, na_w_out, na_rpb,
           moe_router_w, moe_router_b, moe_w1, moe_w3, moe_w2):
    nb, seq, _ = x.shape
    ctx_len = ctx.shape[1]
    t = seq + ctx_len
    nt, nl = t // TM, seq // TM
    ntb = t // TMB
    n_lat_last = seq - TMB * (ntb - 1)
    assert seq % TM == 0 and ctx_len % TM == 0 and seq % ctx_len == 0 and nb < 16
    assert t % TMB == 0 and 0 < n_lat_last <= TMB

    cvec = jnp.zeros((16, D), F32).at[:nb].set(c).at[nb].set(c_ctx)
    mods = _ada(cvec, w_ada, b_ada).reshape(w_ada.shape[0], 16, 1, 6 * D)

    h = jnp.concatenate([x, ctx], axis=1).reshape(nb * t, D)

    w_in, cf = _mix_in_layout(mix_w_in[0], rw_mu[0], mb_conv_w[0], mb_conv_b[0])
    z = _inproj(h, mods[0], norm_mix[0][None], w_in, cf, nb, nt, nl, RW_PAD // TN, (RW_PAD + MB_XBC) // TN)
    z3 = z.reshape(nb, t, MIX_PAD)
    wlw = _lora_pad(rw_w_up[0], 0)
    wla = _lora_pad(rw_a_up[0], 2 * RW_LORA)
    w0 = rw_w0[0][:, None, :]
    a0 = rw_a0[0][:, None, :]
    yrw = _rwkv_scan(z3, w0, wlw, a0, wla, rw_k_k[0][None], rw_k_a[0][None], seq)
    dtb = _pad_cols(mb_dt_bias[0].reshape(1, 2 * MB_HEADS), 128)
    alog = _pad_cols(mb_a_log[0].reshape(1, 2 * MB_HEADS), 128)
    ymb = _ssd_scan(z3, dtb, alog, seq)
    gup = jnp.pad(rw_g_up[0], ((0, 128 - RW_LORA_G), (0, 0))).astype(BF16)
    yrw = [y.reshape(nb * t, RW_DIM) for y in yrw]
    ymb = [y.reshape(nb * t, MB_DIM) for y in ymb]
    h = _finish(z, yrw, ymb, h, mods[0], a0, wla,
                rw_k_a[0][None], rw_r_k[0].reshape(2, 1, RW_DIM), gup, rw_gn_w[0][None], rw_gn_b[0][None],
                jnp.repeat(mb_d[0], HEAD)[None], mb_norm_w[0][None], mix_w_out[0].astype(BF16), nb, nt, nl)
    h = _ffn(h, mods[0], norm_ffn[0][None], ffn_w1[0].astype(BF16), ffn_w3[0].astype(BF16),
             ffn_w2[0].astype(BF16), nb, ntb, n_lat_last)

    qkv = _nmm(h, mods[1], norm_mix[1][None], na_w_qkv[0].astype(BF16), nb, ntb, n_lat_last, BF16)
    o = _attention(qkv.reshape(nb, t, 3 * D), _attn_bias_table(na_rpb[0]), seq, ctx_len)
    mods1 = mods[1].reshape(16, 1, 6 * D)
    h3 = _proj_res(o, h.reshape(nb, t, D), mods1, na_w_out[0].astype(BF16), seq)
    rw = _pad_cols(moe_router_w[0], 128)
    rb = _pad_cols(moe_router_b[0][None], 128)
    return _moe(h3, mods1, norm_ffn[1][None], rw, rb, moe_w1[0].astype(BF16), moe_w3[0].astype(BF16),
                moe_w2[0].astype(BF16), norm_final[None])
```

```python
import functools
import math

import numpy as np
import jax
import jax.numpy as jnp
from jax import lax
from jax.experimental import pallas as pl
from jax.experimental.pallas import tpu as pltpu

F32 = jnp.float32
BF16 = jnp.bfloat16

D = 1024
NORM_EPS = 1e-6
GRID_W = 64

HEAD = 64
RW_DIM = 512
RW_HEADS = 8
RW_LORA = 32
RW_LORA_G = 96
RW_GN_EPS = 64e-5
RW_COLS = 3 * RW_DIM + 4 * RW_LORA + RW_LORA_G
RW_PAD = 1792
RW_CHUNK = 64

MB_DIM = 512
MB_HEADS = 8
MB_GROUPS = 2
MB_STATE = 128
MB_CONV = 5
MB_XBC = MB_DIM + 2 * MB_GROUPS * MB_STATE
MB_PAD = 1792
MB_CHUNK = 128

MIX_PAD = RW_PAD + MB_PAD

NA_HEADS = 16
NA_KH = 8
NA_KW = 16

D_FF = 2816
N_EXPERTS = 8

TM = 256
TMB = 768
HALO = 16
TN = 256
TF = 256
VMEM_LIMIT = 56 * 1024 * 1024


def _cparams(sem):
    return pltpu.CompilerParams(dimension_semantics=sem, vmem_limit_bytes=VMEM_LIMIT)


def _bdot(a, b):
    return jnp.dot(a.astype(BF16), b.astype(BF16), preferred_element_type=F32)


def _bdot_nt(a, b):
    return lax.dot_general(a.astype(BF16), b.astype(BF16), (((1,), (1,)), ((), ())),
                           preferred_element_type=F32)


def _hdot(a, b):
    return jnp.dot(a, b, precision=lax.Precision.HIGHEST, preferred_element_type=F32)


def _split3(x):
    p1 = x.astype(BF16)
    r1 = x - p1.astype(F32)
    p2 = r1.astype(BF16)
    return p1, p2, (r1 - p2.astype(F32)).astype(BF16)


def _dot3(a, b):
    ah, al, _ = _split3(a)
    bh, bl, _ = _split3(b)
    dot = functools.partial(jnp.dot, preferred_element_type=F32)
    return dot(ah, bh) + (dot(ah, bl) + dot(al, bh))


def _tri_cumsum(mask, x):
    tri = jnp.where(mask, 1.0, 0.0).astype(BF16)
    p1, p2, p3 = _split3(x)
    dot = functools.partial(jnp.dot, preferred_element_type=F32)
    return dot(tri, p1) + (dot(tri, p2) + dot(tri, p3))


def _bmm(spec, a, b):
    return jnp.einsum(spec, a.astype(BF16), b.astype(BF16), preferred_element_type=F32)


def _sigmoid(x):
    return 1.0 / (1.0 + jnp.exp(-x))


def _silu(x):
    return x * _sigmoid(x)


def _softplus(x):
    return jnp.maximum(x, 0.0) + jnp.log(1.0 + jnp.exp(-jnp.abs(x)))


def _norm_mod(h, g, shift, scale):
    hn = h * lax.rsqrt(jnp.mean(h * h, axis=-1, keepdims=True) + NORM_EPS)
    return (hn * g) * (1.0 + scale) + shift


def _seg64_sum(x):
    outs = []
    for p in range(x.shape[-1] // 128):
        xp = x[:, 128 * p:128 * (p + 1)]
        lo = lax.broadcasted_iota(jnp.int32, xp.shape, 1) < HEAD
        s_lo = jnp.sum(jnp.where(lo, xp, 0.0), axis=-1, keepdims=True)
        s_hi = jnp.sum(jnp.where(lo, 0.0, xp), axis=-1, keepdims=True)
        outs.append(jnp.where(lo, s_lo, s_hi))
    return jnp.concatenate(outs, axis=-1)


def _heads(x):
    return jnp.stack([x[:, HEAD * h:HEAD * (h + 1)] for h in range(x.shape[-1] // HEAD)], axis=0)


def _unheads(x):
    return jnp.concatenate([x[h] for h in range(x.shape[0])], axis=-1)


def _ada_kernel(c_ref, w_ref, b_ref, o_ref):
    o_ref[0] = _hdot(_silu(c_ref[...]), w_ref[0]) + b_ref[0]


def _ada(cvec, w_ada, b_ada):
    depth = w_ada.shape[0]
    tn = 1536
    return pl.pallas_call(
        _ada_kernel,
        grid=(depth, 6 * D // tn),
        in_specs=[pl.BlockSpec((16, D), lambda l, j: (0, 0)),
                  pl.BlockSpec((1, D, tn), lambda l, j: (l, 0, j)),
                  pl.BlockSpec((1, 1, tn), lambda l, j: (l, 0, j))],
        out_specs=pl.BlockSpec((1, 16, tn), lambda l, j: (l, 0, j)),
        out_shape=jax.ShapeDtypeStruct((depth, 16, 6 * D), F32),
        compiler_params=_cparams(("arbitrary", "arbitrary")),
        name="adaln",
    )(cvec, w_ada, b_ada.reshape(depth, 1, 6 * D))


def _mod_row(i, nt, nl, nb):
    return jnp.where(i % nt >= nl, nb, i // nt)


def _resident(shape):
    return pl.BlockSpec(shape, lambda *_: (0,) * len(shape), pipeline_mode=pl.Buffered(1))


def _inproj_kernel(hp_ref, h_ref, hn_ref, mod_ref, g_ref, w_ref, cf_ref, o_ref, a_scr, z_scr,
                   *, nt, nl, silu_lo, silu_hi):
    t = pl.program_id(0) % nt
    first = (t == 0) | (t == nl)
    last = (t == nl - 1) | (t == nt - 1)
    shift = mod_ref[0, :, 0:D]
    scale = mod_ref[0, :, D:2 * D]
    g = g_ref[...]
    a_scr[0:HALO] = jnp.where(first, 0.0, _norm_mod(hp_ref[...], g, shift, scale)).astype(BF16)
    a_scr[HALO:HALO + TM] = _norm_mod(h_ref[...], g, shift, scale).astype(BF16)
    a_scr[HALO + TM:] = jnp.where(last, 0.0, _norm_mod(hn_ref[...], g, shift, scale)).astype(BF16)
    a = a_scr[...]
    for j in range(w_ref.shape[1] // TN):
        sl = slice(TN * j, TN * (j + 1))
        z = z_scr.at[j % 2]
        z[...] = jnp.dot(a, w_ref[:, sl], preferred_element_type=F32)

        def tap(k):
            return cf_ref[k:k + 1, sl] * z[HALO - 2 + k:HALO - 2 + k + TM, :]

        if j < silu_lo:
            o_ref[:, sl] = tap(1) + tap(2) + tap(3)
        elif j < silu_hi:
            o_ref[:, sl] = _silu(cf_ref[5:6, sl] + tap(0) + tap(1) + tap(2) + tap(3) + tap(4))
        else:
            o_ref[:, sl] = z[HALO:HALO + TM, :]


def _inproj(h2, mods_l, g, w, cf, nb, nt, nl, silu_lo, silu_hi):
    rows = h2.shape[0]
    n = w.shape[1]
    hb = TM // HALO
    return pl.pallas_call(
        functools.partial(_inproj_kernel, nt=nt, nl=nl, silu_lo=silu_lo, silu_hi=silu_hi),
        grid=(rows // TM,),
        in_specs=[pl.BlockSpec((HALO, D), lambda i: (jnp.maximum(i * hb - 1, 0), 0)),
                  pl.BlockSpec((TM, D), lambda i: (i, 0)),
                  pl.BlockSpec((HALO, D), lambda i: (jnp.minimum((i + 1) * hb, rows // HALO - 1), 0)),
                  pl.BlockSpec((1, 1, 6 * D), lambda i: (_mod_row(i, nt, nl, nb), 0, 0)),
                  _resident((1, D)), _resident((D, n)), _resident((8, n))],
        out_specs=pl.BlockSpec((TM, n), lambda i: (i, 0)),
        out_shape=jax.ShapeDtypeStruct((rows, n), F32),
        scratch_shapes=[pltpu.VMEM((TM + 2 * HALO, D), BF16), pltpu.VMEM((2, TM + 2 * HALO, TN), F32)],
        compiler_params=_cparams(("arbitrary",)),
        name="mix_inproj",
    )(h2, h2, h2, mods_l, g, w, cf)


def _ctx_rows(ntb, n_lat_last, tm):
    row = lax.broadcasted_iota(jnp.int32, (tm, 1), 0)
    return (pl.program_id(0) % ntb == ntb - 1) & (row >= n_lat_last)


def _row_mod(mod_ref, modc_ref, lo, is_ctx):
    return jnp.where(is_ctx, modc_ref[0, :, lo:lo + D], mod_ref[0, :, lo:lo + D])


def _big_tile_specs(nb, ntb):
    return [pl.BlockSpec((TMB, D), lambda i: (i, 0)),
            pl.BlockSpec((1, 1, 6 * D), lambda i: (i // ntb, 0, 0)),
            pl.BlockSpec((1, 1, 6 * D), lambda i: (nb, 0, 0)),
            _resident((1, D))]


def _nmm_kernel(h_ref, mod_ref, modc_ref, g_ref, w_ref, o_ref, *, ntb, n_lat_last):
    is_ctx = _ctx_rows(ntb, n_lat_last, TMB)
    a = _norm_mod(h_ref[...], g_ref[...], _row_mod(mod_ref, modc_ref, 0, is_ctx),
                  _row_mod(mod_ref, modc_ref, D, is_ctx)).astype(BF16)
    tn = 2 * TN
    for j in range(w_ref.shape[1] // tn):
        sl = slice(tn * j, tn * (j + 1))
        o_ref[:, sl] = jnp.dot(a, w_ref[:, sl], preferred_element_type=F32).astype(o_ref.dtype)


def _nmm(h2, mods_l, g, w, nb, ntb, n_lat_last, out_dtype):
    rows = h2.shape[0]
    n = w.shape[1]
    return pl.pallas_call(
        functools.partial(_nmm_kernel, ntb=ntb, n_lat_last=n_lat_last),
        grid=(rows // TMB,),
        in_specs=_big_tile_specs(nb, ntb) + [_resident((D, n))],
        out_specs=pl.BlockSpec((TMB, n), lambda i: (i, 0)),
        out_shape=jax.ShapeDtypeStruct((rows, n), out_dtype),
        compiler_params=_cparams(("arbitrary",)),
        name="norm_proj",
    )(h2, mods_l, mods_l, g, w)


def _rwkv_kernel(zf_ref, zb_ref, w0_ref, wlw_ref, a0_ref, wla_ref, kk_ref, ka_ref, yf_ref, yb_ref, s_scr):
    @pl.when(pl.program_id(1) == 0)
    def _():
        s_scr[...] = jnp.zeros_like(s_scr)

    prep = [_rwkv_prep(z_ref[0], w0_ref[d], wlw_ref[d], a0_ref[d], wla_ref[d], kk_ref[...], ka_ref[...],
                       backward=d == 1) for d, z_ref in enumerate((zf_ref, zb_ref))]
    y, s_new = _rwkv_chain(*[jnp.concatenate(parts, axis=0) for parts in zip(*prep)], s_scr[...])
    yf_ref[0] = _unheads(y[:RW_HEADS])
    yb_ref[0] = _unheads(y[RW_HEADS:])
    s_scr[...] = s_new


def _rwkv_prep(z, w0, wlw, a0, wla, kkw, kaw, backward):
    L = RW_CHUNK
    r = z[:, 0:RW_DIM]
    k = z[:, RW_DIM:2 * RW_DIM]
    v = z[:, 2 * RW_DIM:3 * RW_DIM]
    lora = z[:, 3 * RW_DIM:3 * RW_DIM + 128]
    logw = w0 + _dot3(jnp.tanh(lora), wlw)
    logdec = -math.exp(-0.5) * _sigmoid(logw)
    iclr = _sigmoid(a0 + _dot3(lora, wla))
    kk = k * kkw
    kk = kk / jnp.maximum(jnp.sqrt(_seg64_sum(kk * kk)), 1e-12)
    kdir = k * (1.0 + (iclr - 1.0) * kaw)
    bvec = kk * iclr

    row = lax.broadcasted_iota(jnp.int32, (L, L), 0)
    col = lax.broadcasted_iota(jnp.int32, (L, L), 1)
    ahead = col - row if backward else row - col
    incl = ahead >= 0
    strict = ahead > 0
    lc = _tri_cumsum(incl, logdec)
    ltot = jnp.sum(logdec, axis=0, keepdims=True)
    g_in = jnp.exp(lc)
    g_ex = jnp.exp(lc - logdec)
    g_inv = jnp.exp(-lc)
    g_tail = jnp.exp(ltot - lc)

    ar = _heads(jnp.concatenate([-kk * g_ex, r * g_in], axis=0))
    bk = _heads(jnp.concatenate([bvec * g_inv, kdir * g_inv], axis=0))
    bk_tail = _heads(jnp.concatenate([bvec * g_tail, kdir * g_tail], axis=0))
    vh = _heads(v)
    g_tot = _heads(jnp.exp(ltot))
    tri = lambda m: jnp.broadcast_to(jnp.where(m, 1.0, 0.0)[None], (RW_HEADS, L, L))
    return ar, bk, bk_tail, vh, g_tot, tri(incl), tri(strict)


def _rwkv_chain(ar, bk, bk_tail, vh, g_tot, incl, strict, s0):
    L = RW_CHUNK
    m1 = _bmm('hlk,hsk->hls', ar, bk)
    keep = lambda mask, blk: jnp.where(mask > 0.5, blk, 0.0)
    nmat = keep(strict, m1[:, :L, :L])
    a_ak = keep(strict, m1[:, :L, L:])
    m_r = jnp.concatenate([keep(incl, m1[:, L:, :L]), keep(incl, m1[:, L:, L:])], axis=2)
    m2 = _bmm('hlk,hvk->hlv', ar, s0)
    x = m2[:, :L] + _bmm('hls,hsv->hlv', a_ak, vh)
    p = nmat
    steps = int(math.log2(L))
    for i in range(steps):
        x = x + _bmm('hls,hsv->hlv', p, x)
        if i < steps - 1:
            p = _bmm('hls,hst->hlt', p, p)
    uv = jnp.concatenate([x, vh], axis=1)
    y = m2[:, L:] + _bmm('hls,hsv->hlv', m_r, uv)
    return y, s0 * g_tot + _bmm('hvl,hlk->hvk', jnp.swapaxes(uv, 1, 2), bk_tail)


def _scan_chunks(nc, ncl, col_block=0):
    return (lambda b, c: (b, (c + ncl) % nc, col_block)), (lambda b, c: (b, nc - 1 - c, col_block))


def _rwkv_scan(z3, w0, wlw, a0, wla, kkw, kaw, seq):
    nb, t, _ = z3.shape
    nc = t // RW_CHUNK
    fwd, bwd = _scan_chunks(nc, seq // RW_CHUNK)
    y_shape = jax.ShapeDtypeStruct((nb, t, RW_DIM), F32)
    return pl.pallas_call(
        _rwkv_kernel,
        grid=(nb, nc),
        in_specs=[pl.BlockSpec((1, RW_CHUNK, RW_PAD), fwd), pl.BlockSpec((1, RW_CHUNK, RW_PAD), bwd),
                  _resident((2, 1, RW_DIM)), _resident((2, 128, RW_DIM)), _resident((2, 1, RW_DIM)),
                  _resident((2, 128, RW_DIM)), _resident((1, RW_DIM)), _resident((1, RW_DIM))],
        out_specs=[pl.BlockSpec((1, RW_CHUNK, RW_DIM), fwd), pl.BlockSpec((1, RW_CHUNK, RW_DIM), bwd)],
        out_shape=[y_shape, y_shape],
        scratch_shapes=[pltpu.VMEM((2 * RW_HEADS, HEAD, HEAD), F32)],
        compiler_params=_cparams(("arbitrary", "arbitrary")),
        name="rwkv7_scan",
    )(z3, z3, w0, wlw, a0, wla, kkw, kaw)


def _ssd_kernel(zf_ref, zb_ref, dtb_ref, alog_ref, yf_ref, yb_ref, s_scr):
    @pl.when(pl.program_id(1) == 0)
    def _():
        s_scr[...] = jnp.zeros_like(s_scr)

    res = [_ssd_chunk(z_ref[0], dtb_ref[...], alog_ref[...], s_scr[d], d) for d, z_ref in enumerate((zf_ref, zb_ref))]
    for d, y_ref in enumerate((yf_ref, yb_ref)):
        y_ref[0] = res[d][0]
        s_scr[d] = res[d][1]


def _ssd_chunk(z, dtb, alog, s_all, d):
    L = MB_CHUNK
    xm = z[:, 0:MB_DIM]
    dt_all = _softplus(z[:, 1536:1664] + dtb)
    a_all = dt_all * (-jnp.exp(alog))
    row = lax.broadcasted_iota(jnp.int32, (L, L), 0)
    col = lax.broadcasted_iota(jnp.int32, (L, L), 1)
    incl = (col >= row) if d == 1 else (row >= col)
    cs = _tri_cumsum(incl, a_all)
    cs_t = cs.T
    tot = jnp.sum(a_all, axis=0, keepdims=True)
    gmat = []
    for g in range(MB_GROUPS):
        bg = z[:, 512 + 128 * g:640 + 128 * g]
        cg = z[:, 768 + 128 * g:896 + 128 * g]
        gmat.append((bg, cg, _bdot_nt(cg, bg)))
    outs, states = [], []
    for h in range(MB_HEADS):
        bg, cg, cb = gmat[h // (MB_HEADS // MB_GROUPS)]
        j = MB_HEADS * d + h
        cs_col = cs[:, j:j + 1]
        tot_h = tot[:, j:j + 1]
        lmat = jnp.exp(jnp.where(incl, cs_col - cs_t[j:j + 1, :], -jnp.inf))
        xh = xm[:, HEAD * h:HEAD * (h + 1)] * dt_all[:, j:j + 1]
        s0 = s_all[h]
        outs.append(_bdot(cb * lmat, xh) + jnp.exp(cs_col) * _bdot_nt(cg, s0))
        xd = xh * jnp.exp(tot_h - cs_col)
        states.append(s0 * jnp.exp(tot_h) + _bdot(xd.T, bg))
    return jnp.concatenate(outs, axis=-1), jnp.stack(states, axis=0)


def _ssd_scan(z3, dtb, alog, seq):
    nb, t, _ = z3.shape
    nc = t // MB_CHUNK
    fwd, bwd = _scan_chunks(nc, seq // MB_CHUNK, col_block=1)
    yf, yb = _scan_chunks(nc, seq // MB_CHUNK)
    y_shape = jax.ShapeDtypeStruct((nb, t, MB_DIM), F32)
    return pl.pallas_call(
        _ssd_kernel,
        grid=(nb, nc),
        in_specs=[pl.BlockSpec((1, MB_CHUNK, MB_PAD), fwd), pl.BlockSpec((1, MB_CHUNK, MB_PAD), bwd),
                  _resident((1, 128)), _resident((1, 128))],
        out_specs=[pl.BlockSpec((1, MB_CHUNK, MB_DIM), yf), pl.BlockSpec((1, MB_CHUNK, MB_DIM), yb)],
        out_shape=[y_shape, y_shape],
        scratch_shapes=[pltpu.VMEM((2, MB_HEADS, HEAD, MB_STATE), F32)],
        compiler_params=_cparams(("arbitrary", "arbitrary")),
        name="ssd_scan",
    )(z3, z3, dtb, alog)


def _finish_kernel(zr_ref, zm_ref, yrf_ref, yrb_ref, ymf_ref, ymb_ref, h_ref, mod_ref, a0_ref, wla_ref, ka_ref,
                   rk_ref, gup_ref,
                   gnw_ref, gnb_ref, mbd_ref, mbn_ref, wout_ref, o_ref):
    zr = zr_ref[...]
    r = zr[:, 0:RW_DIM]
    k = zr[:, RW_DIM:2 * RW_DIM]
    v = zr[:, 2 * RW_DIM:3 * RW_DIM]
    lora = zr[:, 3 * RW_DIM:3 * RW_DIM + 128]
    gate = _bdot(_sigmoid(zr[:, 3 * RW_DIM + 128:3 * RW_DIM + 256]), gup_ref[...])
    bonus = jnp.zeros_like(r)
    for dd in range(2):
        iclr = _sigmoid(a0_ref[dd] + _hdot(lora, wla_ref[dd]))
        bonus = bonus + r * (k * (1.0 + (iclr - 1.0) * ka_ref[...])) * rk_ref[dd]
    bonus = _seg64_sum(bonus) * v
    y = yrf_ref[...] + yrb_ref[...]
    mu = _seg64_sum(y) * (1.0 / HEAD)
    yc = y - mu
    var = _seg64_sum(yc * yc) * (1.0 / HEAD)
    y = yc * lax.rsqrt(var + RW_GN_EPS) * gnw_ref[...] + gnb_ref[...]
    o_rw = (y + bonus) * gate

    zm = zm_ref[...]
    xm = zm[:, 0:MB_DIM]
    zg = zm[:, 1024:1536]
    ym = (ymf_ref[...] + ymb_ref[...] + mbd_ref[...] * xm) * _silu(zg)
    gw = MB_DIM // MB_GROUPS
    parts = []
    for g in range(MB_GROUPS):
        yg = ym[:, gw * g:gw * (g + 1)]
        parts.append(yg * lax.rsqrt(jnp.mean(yg * yg, axis=-1, keepdims=True) + NORM_EPS))
    o_mb = jnp.concatenate(parts, axis=-1) * mbn_ref[...]
    o = jnp.concatenate([o_rw, o_mb], axis=-1)
    o_ref[...] = h_ref[...] + mod_ref[0, :, 2 * D:3 * D] * _bdot(o, wout_ref[...])


def _finish(z2, yrw, ymb, h2, mods_l, a0, wla, kaw, rk, gup, gnw, gnb, mbd, mbn, wout, nb, nt, nl):
    rows = h2.shape[0]
    full = lambda *shape: _resident(shape)
    half = pl.BlockSpec((TM, RW_DIM), lambda i: (i, 0))
    return pl.pallas_call(
        _finish_kernel,
        grid=(rows // TM,),
        in_specs=[pl.BlockSpec((TM, RW_PAD), lambda i: (i, 0)),
                  pl.BlockSpec((TM, MB_PAD), lambda i: (i, 1)),
                  half, half, half, half,
                  pl.BlockSpec((TM, D), lambda i: (i, 0)),
                  pl.BlockSpec((1, 1, 6 * D), lambda i: (_mod_row(i, nt, nl, nb), 0, 0)),
                  full(2, 1, RW_DIM), full(2, 128, RW_DIM), full(1, RW_DIM), full(2, 1, RW_DIM),
                  full(128, RW_DIM), full(1, RW_DIM), full(1, RW_DIM), full(1, MB_DIM), full(1, MB_DIM),
                  full(D, D)],
        out_specs=pl.BlockSpec((TM, D), lambda i: (i, 0)),
        out_shape=jax.ShapeDtypeStruct((rows, D), F32),
        compiler_params=_cparams(("arbitrary",)),
        name="mix_finish",
    )(z2, z2, yrw[0], yrw[1], ymb[0], ymb[1], h2, mods_l, a0, wla, kaw, rk, gup, gnw, gnb, mbd, mbn, wout)


def _ffn_kernel(h_ref, mod_ref, modc_ref, g_ref, w1_ref, w3_ref, w2_ref, o_ref, *, ntb, n_lat_last):
    is_ctx = _ctx_rows(ntb, n_lat_last, TMB)
    h = h_ref[...]
    a = _norm_mod(h, g_ref[...], _row_mod(mod_ref, modc_ref, 3 * D, is_ctx),
                  _row_mod(mod_ref, modc_ref, 4 * D, is_ctx)).astype(BF16)
    acc = jnp.zeros((TMB, D), F32)
    for c in range(D_FF // TF):
        sl = slice(TF * c, TF * (c + 1))
        hid = _silu(jnp.dot(a, w1_ref[:, sl], preferred_element_type=F32)) * jnp.dot(
            a, w3_ref[:, sl], preferred_element_type=F32)
        acc = acc + jnp.dot(hid.astype(BF16), w2_ref[sl, :], preferred_element_type=F32)
    o_ref[...] = h + _row_mod(mod_ref, modc_ref, 5 * D, is_ctx) * acc


def _ffn(h2, mods_l, g, w1, w3, w2, nb, ntb, n_lat_last):
    rows = h2.shape[0]
    return pl.pallas_call(
        functools.partial(_ffn_kernel, ntb=ntb, n_lat_last=n_lat_last),
        grid=(rows // TMB,),
        in_specs=_big_tile_specs(nb, ntb) + [_resident((D, D_FF)), _resident((D, D_FF)), _resident((D_FF, D))],
        out_specs=pl.BlockSpec((TMB, D), lambda i: (i, 0)),
        out_shape=jax.ShapeDtypeStruct((rows, D), F32),
        compiler_params=_cparams(("arbitrary",)),
        name="ffn_swiglu",
    )(h2, mods_l, mods_l, g, w1, w3, w2)


def _attn_kernel(q_ref, k_ref, v_ref, kc_ref, vc_ref, bias_ref, o_ref, *, n_rows):
    r = pl.program_id(1)
    rs = jnp.clip(r - NA_KH // 2, 0, n_rows - NA_KH)
    start = pl.multiple_of(rs * GRID_W, GRID_W)
    nk = NA_KH * GRID_W
    nc = kc_ref.shape[1]
    low = lax.broadcasted_iota(jnp.int32, (GRID_W, 128), 1) < HEAD
    for pair in range(NA_HEADS // 2):
        sl = slice(128 * pair, 128 * (pair + 1))
        q = q_ref[0, :, sl] * (HEAD ** -0.5)
        kp = k_ref[0, pl.ds(start, nk), sl]
        kcp = kc_ref[0, :, sl]
        v1 = jnp.concatenate([v_ref[0, pl.ds(start, nk), sl], jnp.ones((nk, 128), BF16)], axis=-1)
        vc1 = jnp.concatenate([vc_ref[0, :, sl], jnp.ones((nc, 128), BF16)], axis=-1)
        halves = []
        for half in range(2):
            qh = jnp.where(low if half == 0 else jnp.logical_not(low), q, jnp.zeros_like(q))
            s_lat = _bdot_nt(qh, kp) + bias_ref[0, 2 * pair + half]
            s_ctx = _bdot_nt(qh, kcp)
            m = s_ctx
            for j in range(nk // nc):
                m = jnp.maximum(m, s_lat[:, nc * j:nc * (j + 1)])
            m = jnp.max(m, axis=-1, keepdims=True)
            ov = _bdot(jnp.exp(s_lat - m), v1) + _bdot(jnp.exp(s_ctx - m), vc1)
            halves.append(ov[:, :128] / ov[:, 128:])
        o_ref[0, :, sl] = jnp.where(low, halves[0], halves[1]).astype(o_ref.dtype)


def _attention(qkv, bias, seq, ctx_len):
    nb = qkv.shape[0]
    n_rows = seq // GRID_W
    nk = NA_KH * GRID_W

    def cfg(b, r):
        return r - jnp.clip(r - NA_KH // 2, 0, n_rows - NA_KH)

    return pl.pallas_call(
        functools.partial(_attn_kernel, n_rows=n_rows),
        grid=(nb, n_rows),
        in_specs=[pl.BlockSpec((1, GRID_W, D), lambda b, r: (b, r, 0)),
                  pl.BlockSpec((1, seq, D), lambda b, r: (b, 0, 1)),
                  pl.BlockSpec((1, seq, D), lambda b, r: (b, 0, 2)),
                  pl.BlockSpec((1, ctx_len, D), lambda b, r: (b, seq // ctx_len, 1)),
                  pl.BlockSpec((1, ctx_len, D), lambda b, r: (b, seq // ctx_len, 2)),
                  pl.BlockSpec((1, NA_HEADS, GRID_W, nk), lambda b, r: (cfg(b, r), 0, 0, 0))],
        out_specs=pl.BlockSpec((1, GRID_W, D), lambda b, r: (b, r, 0)),
        out_shape=jax.ShapeDtypeStruct((nb, seq, D), BF16),
        compiler_params=_cparams(("arbitrary", "arbitrary")),
        name="nbr_attention",
    )(qkv, qkv, qkv, qkv, qkv, bias)


def _attn_bias_table(rpb):
    q = np.arange(GRID_W)[:, None]
    c = np.arange(GRID_W)[None, :]
    ws = np.clip(q - NA_KW // 2, 0, GRID_W - NA_KW)
    inside = (c >= ws) & (c < ws + NA_KW)
    cidx = np.clip(c - q + NA_KW - 1, 0, 2 * NA_KW - 2)
    ridx = np.arange(NA_KH)[None, :] + (NA_KH - 1) - np.arange(NA_KH)[:, None]
    tab = rpb[:, ridx][:, :, :, cidx]
    tab = jnp.where(inside[None, None, None], tab.astype(F32), -jnp.inf)
    tab = tab.transpose(1, 0, 3, 2, 4)
    return tab.reshape(NA_KH, NA_HEADS, GRID_W, NA_KH * GRID_W)


def _proj_res_kernel(o_ref, h_ref, mod_ref, w_ref, out_ref):
    out_ref[0] = h_ref[0] + mod_ref[0, :, 2 * D:3 * D] * jnp.dot(o_ref[0], w_ref[...],
                                                                 preferred_element_type=F32)


def _proj_res(o3, h3, mods_l, w, seq):
    nb = o3.shape[0]
    tm = 512
    return pl.pallas_call(
        _proj_res_kernel,
        grid=(nb, seq // tm),
        in_specs=[pl.BlockSpec((1, tm, D), lambda b, i: (b, i, 0)),
                  pl.BlockSpec((1, tm, D), lambda b, i: (b, i, 0)),
                  pl.BlockSpec((1, 1, 6 * D), lambda b, i: (b, 0, 0)),
                  pl.BlockSpec((D, D), lambda b, i: (0, 0))],
        out_specs=pl.BlockSpec((1, tm, D), lambda b, i: (b, i, 0)),
        out_shape=jax.ShapeDtypeStruct((nb, seq, D), F32),
        compiler_params=_cparams(("arbitrary", "arbitrary")),
        name="attn_outproj",
    )(o3, h3, mods_l, w)


MOE_TM = 512
FF_CHUNK = 1408
ROW_TILE = 8


def _to_token_tiles(ref, val):
    n = val.shape[0]
    for s in range(ROW_TILE):
        ref[pl.ds(s, n, stride=ROW_TILE), :] = val[:, 128 * s:128 * (s + 1)]


def _from_token_tiles(ref, n):
    return jnp.concatenate([ref[pl.ds(s, n, stride=ROW_TILE), :] for s in range(ROW_TILE)], axis=-1)


def _router_kernel(h_ref, mod_ref, g_ref, rw_ref, rb_ref, a8_ref, idx_ref, gate_ref):
    a = _norm_mod(h_ref[0], g_ref[...], mod_ref[0, :, 3 * D:4 * D], mod_ref[0, :, 4 * D:5 * D])
    _to_token_tiles(a8_ref, a)
    logits = _hdot(a, rw_ref[...]) + rb_ref[...]
    lane = lax.broadcasted_iota(jnp.int32, logits.shape, 1)
    logits = jnp.where(lane < N_EXPERTS, logits, -jnp.inf)
    m1 = jnp.max(logits, axis=-1, keepdims=True)
    i1 = jnp.min(jnp.where(logits == m1, lane, 128), axis=-1, keepdims=True)
    rest = jnp.where(lane == i1, -jnp.inf, logits)
    m2 = jnp.max(rest, axis=-1, keepdims=True)
    i2 = jnp.min(jnp.where(rest == m2, lane, 128), axis=-1, keepdims=True)
    ex = jnp.exp(m2 - m1)
    idx_ref[...] = jnp.where(lane == 0, i1, jnp.where(lane == 1, i2, 0))
    gate_ref[...] = jnp.where(lane == 0, 1.0 / (1.0 + ex), jnp.where(lane == 1, ex / (1.0 + ex), 0.0))


def _router(h3, mods_l, g, rw, rb):
    nb, seq, _ = h3.shape
    tm = 512
    ni = seq // tm
    rows = nb * seq
    return pl.pallas_call(
        _router_kernel,
        grid=(nb, ni),
        in_specs=[pl.BlockSpec((1, tm, D), lambda b, i: (b, i, 0)),
                  pl.BlockSpec((1, 1, 6 * D), lambda b, i: (b, 0, 0)),
                  pl.BlockSpec((1, D), lambda b, i: (0, 0)),
                  pl.BlockSpec((D, 128), lambda b, i: (0, 0)),
                  pl.BlockSpec((1, 128), lambda b, i: (0, 0))],
        out_specs=[pl.BlockSpec((tm * ROW_TILE, 128), lambda b, i: (b * ni + i, 0)),
                   pl.BlockSpec((tm, 128), lambda b, i: (b * ni + i, 0)),
                   pl.BlockSpec((tm, 128), lambda b, i: (b * ni + i, 0))],
        out_shape=[jax.ShapeDtypeStruct((rows * ROW_TILE, 128), F32),
                   jax.ShapeDtypeStruct((rows, 128), jnp.int32),
                   jax.ShapeDtypeStruct((rows, 128), F32)],
        compiler_params=_cparams(("arbitrary", "arbitrary")),
        name="moe_router",
    )(h3, mods_l, g, rw, rb)


def _route_plan(idx, rows):
    n_pairs = 2 * rows
    n_tiles = n_pairs // MOE_TM + N_EXPERTS
    expert = jnp.concatenate([idx[:, 0], idx[:, 1]])
    onehot = (expert[:, None] == jnp.arange(N_EXPERTS)[None, :]).astype(jnp.int32)
    cum = jnp.cumsum(onehot, axis=0)
    counts = cum[-1]
    rank = jnp.sum(onehot * cum, axis=1) - 1
    padded = (counts + MOE_TM - 1) // MOE_TM * MOE_TM
    ends = jnp.cumsum(padded)
    starts = ends - padded
    pos = starts[expert] + rank
    pair = jnp.arange(n_pairs, dtype=jnp.int32)
    dst_of_row = jnp.zeros((n_tiles * MOE_TM,), jnp.int32).at[pos].set(pair)
    tok_of_row = dst_of_row % rows
    tile_start = jnp.arange(n_tiles, dtype=jnp.int32) * MOE_TM
    tile_expert = jnp.minimum(jnp.sum((tile_start[:, None] >= ends[None, :]).astype(jnp.int32), axis=1),
                              N_EXPERTS - 1)
    n_valid = jnp.clip(starts[tile_expert] + counts[tile_expert] - tile_start, 0, MOE_TM)
    n_valid = jnp.where(tile_start < ends[-1], n_valid, 0)
    return tile_expert.astype(jnp.int32), n_valid.astype(jnp.int32), tok_of_row, dst_of_row


def _token_copy(src, src_tok, dst, dst_tok, sem, n_tok=1):
    n = n_tok * ROW_TILE
    return pltpu.make_async_copy(src.at[pl.ds(pl.multiple_of(src_tok * ROW_TILE, ROW_TILE), n)],
                                 dst.at[pl.ds(pl.multiple_of(dst_tok * ROW_TILE, ROW_TILE), n)], sem)


def _expert_kernel(texp_ref, nval_ref, tok_ref, dst_ref, a8_hbm, w1_ref, w3_ref, w2_ref, y8_hbm,
                   xbuf, ybuf, gsem, ssem):
    i = pl.program_id(0)
    nv = nval_ref[i]
    nxt = jnp.minimum(i + 1, pl.num_programs(0) - 1)
    has_next = (i + 1 < pl.num_programs(0)) & (nval_ref[nxt] > 0)

    def start_gather(tile, slot):
        def body(r, carry):
            _token_copy(a8_hbm, tok_ref[tile * MOE_TM + r], xbuf.at[slot], r, gsem.at[slot]).start()
            return carry

        lax.fori_loop(0, MOE_TM, body, 0)

    def wait_scatter(n):
        pltpu.make_async_copy(ybuf.at[pl.ds(0, n * ROW_TILE)], y8_hbm.at[pl.ds(0, n * ROW_TILE)], ssem).wait()

    @pl.when(nv > 0)
    def _():
        slot = i % 2

        @pl.when(i == 0)
        def _():
            start_gather(0, 0)

        _token_copy(a8_hbm, 0, xbuf.at[slot], 0, gsem.at[slot], MOE_TM).wait()

        @pl.when(has_next)
        def _():
            start_gather(nxt, 1 - slot)

        x = _from_token_tiles(xbuf.at[slot], MOE_TM).astype(BF16)
        acc = jnp.zeros((MOE_TM, D), F32)
        for c in range(D_FF // FF_CHUNK):
            sl = slice(FF_CHUNK * c, FF_CHUNK * (c + 1))
            hid = _silu(jnp.dot(x, w1_ref[0, :, sl], preferred_element_type=F32)) * jnp.dot(
                x, w3_ref[0, :, sl], preferred_element_type=F32)
            acc = acc + jnp.dot(hid.astype(BF16), w2_ref[0, sl, :], preferred_element_type=F32)

        @pl.when(i > 0)
        def _():
            wait_scatter(nval_ref[jnp.maximum(i - 1, 0)])

        _to_token_tiles(ybuf, acc)

        def scatter(r, carry):
            _token_copy(ybuf, r, y8_hbm, dst_ref[i * MOE_TM + r], ssem).start()
            return carry

        lax.fori_loop(0, nv, scatter, 0)

        @pl.when(jnp.logical_not(has_next))
        def _():
            wait_scatter(nv)


def _experts(tile_expert, n_valid, tok_of_row, dst_of_row, a8, w1, w3, w2, rows):
    n_tiles = tile_expert.shape[0]
    wspec = lambda shape: pl.BlockSpec(shape, lambda i, te, nv, tk, ds: (te[i], 0, 0),
                                       pipeline_mode=pl.Buffered(1))
    return pl.pallas_call(
        _expert_kernel,
        grid_spec=pltpu.PrefetchScalarGridSpec(
            num_scalar_prefetch=4,
            grid=(n_tiles,),
            in_specs=[pl.BlockSpec(memory_space=pl.ANY),
                      wspec((1, D, D_FF)), wspec((1, D, D_FF)), wspec((1, D_FF, D))],
            out_specs=pl.BlockSpec(memory_space=pl.ANY),
            scratch_shapes=[pltpu.VMEM((2, MOE_TM * ROW_TILE, 128), F32), pltpu.VMEM((MOE_TM * ROW_TILE, 128), F32),
                            pltpu.SemaphoreType.DMA((2,)), pltpu.SemaphoreType.DMA]),
        out_shape=jax.ShapeDtypeStruct((2 * rows * ROW_TILE, 128), F32),
        compiler_params=_cparams(("arbitrary",)),
        name="moe_experts",
    )(tile_expert, n_valid, tok_of_row, dst_of_row, a8, w1, w3, w2)


def _combine_kernel(h_ref, mod_ref, gate_ref, y_ref, gf_ref, o_ref):
    tm = h_ref.shape[1]
    gates = gate_ref[...]
    y = gates[:, 0:1] * _from_token_tiles(y_ref.at[0], tm) + gates[:, 1:2] * _from_token_tiles(y_ref.at[1], tm)
    out = h_ref[0] + mod_ref[0, :, 5 * D:6 * D] * y
    out = out * lax.rsqrt(jnp.mean(out * out, axis=-1, keepdims=True) + NORM_EPS)
    o_ref[0] = out * gf_ref[...]


def _combine(h3, mods_l, gates, y8, gf):
    nb, seq, _ = h3.shape
    tm = 512
    ni = seq // tm
    return pl.pallas_call(
        _combine_kernel,
        grid=(nb, ni),
        in_specs=[pl.BlockSpec((1, tm, D), lambda b, i: (b, i, 0)),
                  pl.BlockSpec((1, 1, 6 * D), lambda b, i: (b, 0, 0)),
                  pl.BlockSpec((tm, 128), lambda b, i: (b * ni + i, 0)),
                  pl.BlockSpec((2, tm * ROW_TILE, 128), lambda b, i: (0, b * ni + i, 0)),
                  pl.BlockSpec((1, D), lambda b, i: (0, 0))],
        out_specs=pl.BlockSpec((1, tm, D), lambda b, i: (b, i, 0)),
        out_shape=jax.ShapeDtypeStruct((nb, seq, D), F32),
        compiler_params=_cparams(("arbitrary", "arbitrary")),
        name="moe_combine_norm",
    )(h3, mods_l, gates, y8, gf)


def _moe(h3, mods_l, g, rw, rb, w1, w3, w2, gf):
    nb, seq, _ = h3.shape
    rows = nb * seq
    a8, idx, gates = _router(h3, mods_l, g, rw, rb)
    tile_expert, n_valid, tok_of_row, dst_of_row = _route_plan(idx, rows)
    y8 = _experts(tile_expert, n_valid, tok_of_row, dst_of_row, a8, w1, w3, w2, rows)
    return _combine(h3, mods_l, gates, y8.reshape(2, rows * ROW_TILE, 128), gf)


def _pad_cols(a, n):
    return jnp.pad(a, [(0, 0)] * (a.ndim - 1) + [(0, n - a.shape[-1])])


def _mix_in_layout(w_in, rw_mu, conv_w, conv_b):
    w_rw = _pad_cols(w_in[:, :RW_COLS], RW_PAD)
    mb = w_in[:, RW_COLS:]
    w_mb = _pad_cols(jnp.concatenate([mb[:, MB_DIM:MB_DIM + MB_XBC], mb[:, :MB_DIM], mb[:, MB_DIM + MB_XBC:]],
                                     axis=1), MB_PAD)
    w = jnp.concatenate([w_rw, w_mb], axis=1).astype(BF16)
    mu_p = _pad_cols(rw_mu[0], RW_PAD)
    mu_n = _pad_cols(rw_mu[1], RW_PAD)
    zeros = jnp.zeros((RW_PAD,), F32)
    cf_rw = jnp.stack([zeros, mu_p, 1.0 - mu_p - mu_n, mu_n, zeros, zeros, zeros, zeros])
    ident = jnp.zeros((8, MB_PAD - MB_XBC), F32).at[2].set(1.0)
    cf_xbc = jnp.concatenate([conv_w, conv_b[None], jnp.zeros((2, MB_XBC), F32)], axis=0)
    cf = jnp.concatenate([cf_rw, cf_xbc, ident], axis=1)
    return w, cf


def _lora_pad(up, offset):
    out = jnp.zeros((2, 128, up.shape[-1]), F32)
    for d in range(2):
        out = out.at[d, offset + RW_LORA * d:offset + RW_LORA * (d + 1)].set(up[d])
    return out


def kernel(x, c, ctx, c_ctx, w_ada, b_ada, norm_mix, norm_ffn, norm_final, mix_w_in, mix_w_out, rw_mu, rw_w0,
           rw_w_up, rw_a0, rw_a_up, rw_g_up, rw_k_k, rw_k_a, rw_r_k, rw_gn_w, rw_gn_b, mb_conv_w, mb_conv_b,
           mb_dt_bias, mb_a_log, mb_d, mb_norm_w, ffn_w1, ffn_w3, ffn_w2, na_w_qkv, na_w_out, na_rpb,
           moe_router_w, moe_router_b, moe_w1, moe_w3, moe_w2):
    nb, seq, _ = x.shape
    ctx_len = ctx.shape[1]
    t = seq + ctx_len
    nt, nl = t // TM, seq // TM
    ntb = t // TMB
    n_lat_last = seq - TMB * (ntb - 1)
    assert seq % TM == 0 and ctx_len % TM == 0 and seq % ctx_len == 0 and nb < 16
    assert t % TMB == 0 and 0 < n_lat_last <= TMB

    cvec = jnp.zeros((16, D), F32).at[:nb].set(c).at[nb].set(c_ctx)
    mods = _ada(cvec, w_ada, b_ada).reshape(w_ada.shape[0], 16, 1, 6 * D)

    h = jnp.concatenate([x, ctx], axis=1).reshape(nb * t, D)

    w_in, cf = _mix_in_layout(mix_w_in[0], rw_mu[0], mb_conv_w[0], mb_conv_b[0])
    z = _inproj(h, mods[0], norm_mix[0][None], w_in, cf, nb, nt, nl, RW_PAD // TN, (RW_PAD + MB_XBC) // TN)
    z3 = z.reshape(nb, t, MIX_PAD)
    wlw = _lora_pad(rw_w_up[0], 0)
    wla = _lora_pad(rw_a_up[0], 2 * RW_LORA)
    w0 = rw_w0[0][:, None, :]
    a0 = rw_a0[0][:, None, :]
    yrw = _rwkv_scan(z3, w0, wlw, a0, wla, rw_k_k[0][None], rw_k_a[0][None], seq)
    dtb = _pad_cols(mb_dt_bias[0].reshape(1, 2 * MB_HEADS), 128)
    alog = _pad_cols(mb_a_log[0].reshape(1, 2 * MB_HEADS), 128)
    ymb = _ssd_scan(z3, dtb, alog, seq)
    gup = jnp.pad(rw_g_up[0], ((0, 128 - RW_LORA_G), (0, 0))).astype(BF16)
    yrw = [y.reshape(nb * t, RW_DIM) for y in yrw]
    ymb = [y.reshape(nb * t, MB_DIM) for y in ymb]
    h = _finish(z, yrw, ymb, h, mods[0], a0, wla,
                rw_k_a[0][None], rw_r_k[0].reshape(2, 1, RW_DIM), gup, rw_gn_w[0][None], rw_gn_b[0][None],
                jnp.repeat(mb_d[0], HEAD)[None], mb_norm_w[0][None], mix_w_out[0].astype(BF16), nb, nt, nl)
    h = _ffn(h, mods[0], norm_ffn[0][None], ffn_w1[0].astype(BF16), ffn_w3[0].astype(BF16),
             ffn_w2[0].astype(BF16), nb, ntb, n_lat_last)

    qkv = _nmm(h, mods[1], norm_mix[1][None], na_w_qkv[0].astype(BF16), nb, ntb, n_lat_last, BF16)
    o = _attention(qkv.reshape(nb, t, 3 * D), _attn_bias_table(na_rpb[0]), seq, ctx_len)
    mods1 = mods[1].reshape(16, 1, 6 * D)
    h3 = _proj_res(o, h.reshape(nb, t, D), mods1, na_w_out[0].astype(BF16), seq)
    rw = _pad_cols(moe_router_w[0], 128)
    rb = _pad_cols(moe_router_b[0][None], 128)
    return _moe(h3, mods1, norm_ffn[1][None], rw, rb, moe_w1[0].astype(BF16), moe_w3[0].astype(BF16),
                moe_w2[0].astype(BF16), norm_final[None])
```

```python
import functools
import math

import numpy as np
import jax
import jax.numpy as jnp
from jax import lax
from jax.experimental import pallas as pl
from jax.experimental.pallas import tpu as pltpu

F32 = jnp.float32
BF16 = jnp.bfloat16

D = 1024
NORM_EPS = 1e-6
GRID_W = 64

HEAD = 64
RW_DIM = 512
RW_HEADS = 8
RW_LORA = 32
RW_LORA_G = 96
RW_GN_EPS = 64e-5
RW_COLS = 3 * RW_DIM + 4 * RW_LORA + RW_LORA_G
RW_PAD = 1792
RW_CHUNK = 64

MB_DIM = 512
MB_HEADS = 8
MB_GROUPS = 2
MB_STATE = 128
MB_CONV = 5
MB_XBC = MB_DIM + 2 * MB_GROUPS * MB_STATE
MB_PAD = 1792
MB_CHUNK = 128

MIX_PAD = RW_PAD + MB_PAD

NA_HEADS = 16
NA_KH = 8
NA_KW = 16

D_FF = 2816
N_EXPERTS = 8

TM = 256
TMB = 768
HALO = 16
TN = 256
TF = 256
VMEM_LIMIT = 56 * 1024 * 1024


def _cparams(sem):
    return pltpu.CompilerParams(dimension_semantics=sem, vmem_limit_bytes=VMEM_LIMIT)


def _bdot(a, b):
    return jnp.dot(a.astype(BF16), b.astype(BF16), preferred_element_type=F32)


def _bdot_nt(a, b):
    return lax.dot_general(a.astype(BF16), b.astype(BF16), (((1,), (1,)), ((), ())),
                           preferred_element_type=F32)


def _hdot(a, b):
    return jnp.dot(a, b, precision=lax.Precision.HIGHEST, preferred_element_type=F32)


def _split3(x):
    p1 = x.astype(BF16)
    r1 = x - p1.astype(F32)
    p2 = r1.astype(BF16)
    return p1, p2, (r1 - p2.astype(F32)).astype(BF16)


def _dot3(a, b):
    ah, al, _ = _split3(a)
    bh, bl, _ = _split3(b)
    dot = functools.partial(jnp.dot, preferred_element_type=F32)
    return dot(ah, bh) + (dot(ah, bl) + dot(al, bh))


def _tri_cumsum(mask, x):
    tri = jnp.where(mask, 1.0, 0.0).astype(BF16)
    p1, p2, p3 = _split3(x)
    dot = functools.partial(jnp.dot, preferred_element_type=F32)
    return dot(tri, p1) + (dot(tri, p2) + dot(tri, p3))


def _bmm(spec, a, b):
    return jnp.einsum(spec, a.astype(BF16), b.astype(BF16), preferred_element_type=F32)


def _sigmoid(x):
    return 1.0 / (1.0 + jnp.exp(-x))


def _silu(x):
    return x * _sigmoid(x)


def _softplus(x):
    return jnp.maximum(x, 0.0) + jnp.log(1.0 + jnp.exp(-jnp.abs(x)))


def _norm_mod(h, g, shift, scale):
    hn = h * lax.rsqrt(jnp.mean(h * h, axis=-1, keepdims=True) + NORM_EPS)
    return (hn * g) * (1.0 + scale) + shift


def _seg64_sum(x):
    outs = []
    for p in range(x.shape[-1] // 128):
        xp = x[:, 128 * p:128 * (p + 1)]
        lo = lax.broadcasted_iota(jnp.int32, xp.shape, 1) < HEAD
        s_lo = jnp.sum(jnp.where(lo, xp, 0.0), axis=-1, keepdims=True)
        s_hi = jnp.sum(jnp.where(lo, 0.0, xp), axis=-1, keepdims=True)
        outs.append(jnp.where(lo, s_lo, s_hi))
    return jnp.concatenate(outs, axis=-1)


def _heads(x):
    return jnp.stack([x[:, HEAD * h:HEAD * (h + 1)] for h in range(x.shape[-1] // HEAD)], axis=0)


def _unheads(x):
    return jnp.concatenate([x[h] for h in range(x.shape[0])], axis=-1)


def _ada_kernel(c_ref, w_ref, b_ref, o_ref):
    o_ref[0] = _hdot(_silu(c_ref[...]), w_ref[0]) + b_ref[0]


def _ada(cvec, w_ada, b_ada):
    depth = w_ada.shape[0]
    tn = 1536
    return pl.pallas_call(
        _ada_kernel,
        grid=(depth, 6 * D // tn),
        in_specs=[pl.BlockSpec((16, D), lambda l, j: (0, 0)),
                  pl.BlockSpec((1, D, tn), lambda l, j: (l, 0, j)),
                  pl.BlockSpec((1, 1, tn), lambda l, j: (l, 0, j))],
        out_specs=pl.BlockSpec((1, 16, tn), lambda l, j: (l, 0, j)),
        out_shape=jax.ShapeDtypeStruct((depth, 16, 6 * D), F32),
        compiler_params=_cparams(("arbitrary", "arbitrary")),
        name="adaln",
    )(cvec, w_ada, b_ada.reshape(depth, 1, 6 * D))


def _mod_row(i, nt, nl, nb):
    return jnp.where(i % nt >= nl, nb, i // nt)


def _resident(shape):
    return pl.BlockSpec(shape, lambda *_: (0,) * len(shape), pipeline_mode=pl.Buffered(1))


def _inproj_kernel(hp_ref, h_ref, hn_ref, mod_ref, g_ref, w_ref, cf_ref, o_ref, a_scr, z_scr,
                   *, nt, nl, silu_lo, silu_hi):
    t = pl.program_id(0) % nt
    first = (t == 0) | (t == nl)
    last = (t == nl - 1) | (t == nt - 1)
    shift = mod_ref[0, :, 0:D]
    scale = mod_ref[0, :, D:2 * D]
    g = g_ref[...]
    a_scr[0:HALO] = jnp.where(first, 0.0, _norm_mod(hp_ref[...], g, shift, scale)).astype(BF16)
    a_scr[HALO:HALO + TM] = _norm_mod(h_ref[...], g, shift, scale).astype(BF16)
    a_scr[HALO + TM:] = jnp.where(last, 0.0, _norm_mod(hn_ref[...], g, shift, scale)).astype(BF16)
    a = a_scr[...]
    for j in range(w_ref.shape[1] // TN):
        sl = slice(TN * j, TN * (j + 1))
        z = z_scr.at[j % 2]
        z[...] = jnp.dot(a, w_ref[:, sl], preferred_element_type=F32)

        def tap(k):
            return cf_ref[k:k + 1, sl] * z[HALO - 2 + k:HALO - 2 + k + TM, :]

        if j < silu_lo:
            o_ref[:, sl] = tap(1) + tap(2) + tap(3)
        elif j < silu_hi:
            o_ref[:, sl] = _silu(cf_ref[5:6, sl] + tap(0) + tap(1) + tap(2) + tap(3) + tap(4))
        else:
            o_ref[:, sl] = z[HALO:HALO + TM, :]


def _inproj(h2, mods_l, g, w, cf, nb, nt, nl, silu_lo, silu_hi):
    rows = h2.shape[0]
    n = w.shape[1]
    hb = TM // HALO
    return pl.pallas_call(
        functools.partial(_inproj_kernel, nt=nt, nl=nl, silu_lo=silu_lo, silu_hi=silu_hi),
        grid=(rows // TM,),
        in_specs=[pl.BlockSpec((HALO, D), lambda i: (jnp.maximum(i * hb - 1, 0), 0)),
                  pl.BlockSpec((TM, D), lambda i: (i, 0)),
                  pl.BlockSpec((HALO, D), lambda i: (jnp.minimum((i + 1) * hb, rows // HALO - 1), 0)),
                  pl.BlockSpec((1, 1, 6 * D), lambda i: (_mod_row(i, nt, nl, nb), 0, 0)),
                  _resident((1, D)), _resident((D, n)), _resident((8, n))],
        out_specs=pl.BlockSpec((TM, n), lambda i: (i, 0)),
        out_shape=jax.ShapeDtypeStruct((rows, n), F32),
        scratch_shapes=[pltpu.VMEM((TM + 2 * HALO, D), BF16), pltpu.VMEM((2, TM + 2 * HALO, TN), F32)],
        compiler_params=_cparams(("arbitrary",)),
        name="mix_inproj",
    )(h2, h2, h2, mods_l, g, w, cf)


def _ctx_rows(ntb, n_lat_last, tm):
    row = lax.broadcasted_iota(jnp.int32, (tm, 1), 0)
    return (pl.program_id(0) % ntb == ntb - 1) & (row >= n_lat_last)


def _row_mod(mod_ref, modc_ref, lo, is_ctx):
    return jnp.where(is_ctx, modc_ref[0, :, lo:lo + D], mod_ref[0, :, lo:lo + D])


def _big_tile_specs(nb, ntb):
    return [pl.BlockSpec((TMB, D), lambda i: (i, 0)),
            pl.BlockSpec((1, 1, 6 * D), lambda i: (i // ntb, 0, 0)),
            pl.BlockSpec((1, 1, 6 * D), lambda i: (nb, 0, 0)),
            _resident((1, D))]


def _nmm_kernel(h_ref, mod_ref, modc_ref, g_ref, w_ref, o_ref, *, ntb, n_lat_last):
    is_ctx = _ctx_rows(ntb, n_lat_last, TMB)
    a = _norm_mod(h_ref[...], g_ref[...], _row_mod(mod_ref, modc_ref, 0, is_ctx),
                  _row_mod(mod_ref, modc_ref, D, is_ctx)).astype(BF16)
    tn = 2 * TN
    for j in range(w_ref.shape[1] // tn):
        sl = slice(tn * j, tn * (j + 1))
        o_ref[:, sl] = jnp.dot(a, w_ref[:, sl], preferred_element_type=F32).astype(o_ref.dtype)


def _nmm(h2, mods_l, g, w, nb, ntb, n_lat_last, out_dtype):
    rows = h2.shape[0]
    n = w.shape[1]
    return pl.pallas_call(
        functools.partial(_nmm_kernel, ntb=ntb, n_lat_last=n_lat_last),
        grid=(rows // TMB,),
        in_specs=_big_tile_specs(nb, ntb) + [_resident((D, n))],
        out_specs=pl.BlockSpec((TMB, n), lambda i: (i, 0)),
        out_shape=jax.ShapeDtypeStruct((rows, n), out_dtype),
        compiler_params=_cparams(("arbitrary",)),
        name="norm_proj",
    )(h2, mods_l, mods_l, g, w)


def _rwkv_kernel(zf_ref, zb_ref, w0_ref, wlw_ref, a0_ref, wla_ref, kk_ref, ka_ref, yf_ref, yb_ref, s_scr):
    @pl.when(pl.program_id(1) == 0)
    def _():
        s_scr[...] = jnp.zeros_like(s_scr)

    prep = [_rwkv_prep(z_ref[0], w0_ref[d], wlw_ref[d], a0_ref[d], wla_ref[d], kk_ref[...], ka_ref[...],
                       backward=d == 1) for d, z_ref in enumerate((zf_ref, zb_ref))]
    y, s_new = _rwkv_chain(*[jnp.concatenate(parts, axis=0) for parts in zip(*prep)], s_scr[...])
    yf_ref[0] = _unheads(y[:RW_HEADS])
    yb_ref[0] = _unheads(y[RW_HEADS:])
    s_scr[...] = s_new


def _rwkv_prep(z, w0, wlw, a0, wla, kkw, kaw, backward):
    L = RW_CHUNK
    r = z[:, 0:RW_DIM]
    k = z[:, RW_DIM:2 * RW_DIM]
    v = z[:, 2 * RW_DIM:3 * RW_DIM]
    lora = z[:, 3 * RW_DIM:3 * RW_DIM + 128]
    logw = w0 + _dot3(jnp.tanh(lora), wlw)
    logdec = -math.exp(-0.5) * _sigmoid(logw)
    iclr = _sigmoid(a0 + _dot3(lora, wla))
    kk = k * kkw
    kk = kk / jnp.maximum(jnp.sqrt(_seg64_sum(kk * kk)), 1e-12)
    kdir = k * (1.0 + (iclr - 1.0) * kaw)
    bvec = kk * iclr

    row = lax.broadcasted_iota(jnp.int32, (L, L), 0)
    col = lax.broadcasted_iota(jnp.int32, (L, L), 1)
    ahead = col - row if backward else row - col
    incl = ahead >= 0
    strict = ahead > 0
    lc = _tri_cumsum(incl, logdec)
    ltot = jnp.sum(logdec, axis=0, keepdims=True)
    g_in = jnp.exp(lc)
    g_ex = jnp.exp(lc - logdec)
    g_inv = jnp.exp(-lc)
    g_tail = jnp.exp(ltot - lc)

    ar = _heads(jnp.concatenate([-kk * g_ex, r * g_in], axis=0))
    bk = _heads(jnp.concatenate([bvec * g_inv, kdir * g_inv], axis=0))
    bk_tail = _heads(jnp.concatenate([bvec * g_tail, kdir * g_tail], axis=0))
    vh = _heads(v)
    g_tot = _heads(jnp.exp(ltot))
    tri = lambda m: jnp.broadcast_to(jnp.where(m, 1.0, 0.0)[None], (RW_HEADS, L, L))
    return ar, bk, bk_tail, vh, g_tot, tri(incl), tri(strict)


def _rwkv_chain(ar, bk, bk_tail, vh, g_tot, incl, strict, s0):
    L = RW_CHUNK
    m1 = _bmm('hlk,hsk->hls', ar, bk)
    keep = lambda mask, blk: jnp.where(mask > 0.5, blk, 0.0)
    nmat = keep(strict, m1[:, :L, :L])
    a_ak = keep(strict, m1[:, :L, L:])
    m_r = jnp.concatenate([keep(incl, m1[:, L:, :L]), keep(incl, m1[:, L:, L:])], axis=2)
    m2 = _bmm('hlk,hvk->hlv', ar, s0)
    x = m2[:, :L] + _bmm('hls,hsv->hlv', a_ak, vh)
    p = nmat
    steps = int(math.log2(L))
    for i in range(steps):
        x = x + _bmm('hls,hsv->hlv', p, x)
        if i < steps - 1:
            p = _bmm('hls,hst->hlt', p, p)
    uv = jnp.concatenate([x, vh], axis=1)
    y = m2[:, L:] + _bmm('hls,hsv->hlv', m_r, uv)
    return y, s0 * g_tot + _bmm('hvl,hlk->hvk', jnp.swapaxes(uv, 1, 2), bk_tail)


def _scan_chunks(nc, ncl, col_block=0):
    return (lambda b, c: (b, (c + ncl) % nc, col_block)), (lambda b, c: (b, nc - 1 - c, col_block))


def _rwkv_scan(z3, w0, wlw, a0, wla, kkw, kaw, seq):
    nb, t, _ = z3.shape
    nc = t // RW_CHUNK
    fwd, bwd = _scan_chunks(nc, seq // RW_CHUNK)
    y_shape = jax.ShapeDtypeStruct((nb, t, RW_DIM), F32)
    return pl.pallas_call(
        _rwkv_kernel,
        grid=(nb, nc),
        in_specs=[pl.BlockSpec((1, RW_CHUNK, RW_PAD), fwd), pl.BlockSpec((1, RW_CHUNK, RW_PAD), bwd),
                  _resident((2, 1, RW_DIM)), _resident((2, 128, RW_DIM)), _resident((2, 1, RW_DIM)),
                  _resident((2, 128, RW_DIM)), _resident((1, RW_DIM)), _resident((1, RW_DIM))],
        out_specs=[pl.BlockSpec((1, RW_CHUNK, RW_DIM), fwd), pl.BlockSpec((1, RW_CHUNK, RW_DIM), bwd)],
        out_shape=[y_shape, y_shape],
        scratch_shapes=[pltpu.VMEM((2 * RW_HEADS, HEAD, HEAD), F32)],
        compiler_params=_cparams(("arbitrary", "arbitrary")),
        name="rwkv7_scan",
    )(z3, z3, w0, wlw, a0, wla, kkw, kaw)


def _ssd_kernel(zf_ref, zb_ref, dtb_ref, alog_ref, sel_ref, yf_ref, yb_ref, s_scr):
    @pl.when(pl.program_id(1) == 0)
    def _():
        s_scr[...] = jnp.zeros_like(s_scr)

    res = [_ssd_chunk(z_ref[0], dtb_ref[...], alog_ref[...], sel_ref[d], s_scr[d], d)
           for d, z_ref in enumerate((zf_ref, zb_ref))]
    for d, y_ref in enumerate((yf_ref, yb_ref)):
        y_ref[0] = res[d][0]
        s_scr[d] = res[d][1]


def _lane_bcast(x, sel):
    p1, p2, p3 = _split3(x)
    dot = functools.partial(jnp.dot, preferred_element_type=F32)
    return dot(p1, sel) + (dot(p2, sel) + dot(p3, sel))


def _ssd_chunk(z, dtb, alog, sel, s_all, d):
    L = MB_CHUNK
    xm = z[:, 0:MB_DIM]
    dt_all = _softplus(z[:, 1536:1664] + dtb)
    a_all = dt_all * (-jnp.exp(alog))
    row = lax.broadcasted_iota(jnp.int32, (L, L), 0)
    col = lax.broadcasted_iota(jnp.int32, (L, L), 1)
    incl = (col >= row) if d == 1 else (row >= col)
    cs = _tri_cumsum(incl, a_all)
    cs_t = cs.T
    tot = jnp.sum(a_all, axis=0, keepdims=True)
    cs_b = _lane_bcast(cs, sel)
    dt_b = _lane_bcast(dt_all, sel)
    gmat = []
    for g in range(MB_GROUPS):
        bg = z[:, 512 + 128 * g:640 + 128 * g]
        cg = z[:, 768 + 128 * g:896 + 128 * g]
        gmat.append((bg, cg, _bdot_nt(cg, bg)))
    outs, states = [], []
    for h in range(MB_HEADS):
        bg, cg, cb = gmat[h // (MB_HEADS // MB_GROUPS)]
        j = MB_HEADS * d + h
        cs_h = cs_b[:, 128 * h:128 * (h + 1)]
        cs_col = cs_h[:, :HEAD]
        tot_h = tot[:, j:j + 1]
        lmat = jnp.exp(jnp.where(incl, cs_h - cs_t[j:j + 1, :], -jnp.inf))
        xh = xm[:, HEAD * h:HEAD * (h + 1)] * dt_b[:, 128 * h:128 * h + HEAD]
        s0 = s_all[h]
        outs.append(_bdot(cb * lmat, xh) + jnp.exp(cs_col) * _bdot_nt(cg, s0))
        xd = xh * jnp.exp(tot_h - cs_col)
        states.append(s0 * jnp.exp(tot_h) + _bdot(xd.T, bg))
    return jnp.concatenate(outs, axis=-1), jnp.stack(states, axis=0)


def _ssd_scan(z3, dtb, alog, seq):
    nb, t, _ = z3.shape
    nc = t // MB_CHUNK
    fwd, bwd = _scan_chunks(nc, seq // MB_CHUNK, col_block=1)
    yf, yb = _scan_chunks(nc, seq // MB_CHUNK)
    y_shape = jax.ShapeDtypeStruct((nb, t, MB_DIM), F32)
    return pl.pallas_call(
        _ssd_kernel,
        grid=(nb, nc),
        in_specs=[pl.BlockSpec((1, MB_CHUNK, MB_PAD), fwd), pl.BlockSpec((1, MB_CHUNK, MB_PAD), bwd),
                  _resident((1, 128)), _resident((1, 128)), _resident((2, 128, MB_HEADS * 128))],
        out_specs=[pl.BlockSpec((1, MB_CHUNK, MB_DIM), yf), pl.BlockSpec((1, MB_CHUNK, MB_DIM), yb)],
        out_shape=[y_shape, y_shape],
        scratch_shapes=[pltpu.VMEM((2, MB_HEADS, HEAD, MB_STATE), F32)],
        compiler_params=_cparams(("arbitrary", "arbitrary")),
        name="ssd_scan",
    )(z3, z3, dtb, alog, jnp.asarray(_head_selector(), BF16))


def _head_selector():
    sel = np.zeros((2, 128, MB_HEADS * 128), np.float32)
    for d in range(2):
        for h in range(MB_HEADS):
            sel[d, MB_HEADS * d + h, 128 * h:128 * (h + 1)] = 1.0
    return sel


def _finish_kernel(zr_ref, zm_ref, yrf_ref, yrb_ref, ymf_ref, ymb_ref, h_ref, mod_ref, a0_ref, wla_ref, ka_ref,
                   rk_ref, gup_ref,
                   gnw_ref, gnb_ref, mbd_ref, mbn_ref, wout_ref, o_ref):
    zr = zr_ref[...]
    r = zr[:, 0:RW_DIM]
    k = zr[:, RW_DIM:2 * RW_DIM]
    v = zr[:, 2 * RW_DIM:3 * RW_DIM]
    lora = zr[:, 3 * RW_DIM:3 * RW_DIM + 128]
    gate = _bdot(_sigmoid(zr[:, 3 * RW_DIM + 128:3 * RW_DIM + 256]), gup_ref[...])
    bonus = jnp.zeros_like(r)
    for dd in range(2):
        iclr = _sigmoid(a0_ref[dd] + _hdot(lora, wla_ref[dd]))
        bonus = bonus + r * (k * (1.0 + (iclr - 1.0) * ka_ref[...])) * rk_ref[dd]
    bonus = _seg64_sum(bonus) * v
    y = yrf_ref[...] + yrb_ref[...]
    mu = _seg64_sum(y) * (1.0 / HEAD)
    yc = y - mu
    var = _seg64_sum(yc * yc) * (1.0 / HEAD)
    y = yc * lax.rsqrt(var + RW_GN_EPS) * gnw_ref[...] + gnb_ref[...]
    o_rw = (y + bonus) * gate

    zm = zm_ref[...]
    xm = zm[:, 0:MB_DIM]
    zg = zm[:, 1024:1536]
    ym = (ymf_ref[...] + ymb_ref[...] + mbd_ref[...] * xm) * _silu(zg)
    gw = MB_DIM // MB_GROUPS
    parts = []
    for g in range(MB_GROUPS):
        yg = ym[:, gw * g:gw * (g + 1)]
        parts.append(yg * lax.rsqrt(jnp.mean(yg * yg, axis=-1, keepdims=True) + NORM_EPS))
    o_mb = jnp.concatenate(parts, axis=-1) * mbn_ref[...]
    o = jnp.concatenate([o_rw, o_mb], axis=-1)
    o_ref[...] = h_ref[...] + mod_ref[0, :, 2 * D:3 * D] * _bdot(o, wout_ref[...])


def _finish(z2, yrw, ymb, h2, mods_l, a0, wla, kaw, rk, gup, gnw, gnb, mbd, mbn, wout, nb, nt, nl):
    rows = h2.shape[0]
    full = lambda *shape: _resident(shape)
    half = pl.BlockSpec((TM, RW_DIM), lambda i: (i, 0))
    return pl.pallas_call(
        _finish_kernel,
        grid=(rows // TM,),
        in_specs=[pl.BlockSpec((TM, RW_PAD), lambda i: (i, 0)),
                  pl.BlockSpec((TM, MB_PAD), lambda i: (i, 1)),
                  half, half, half, half,
                  pl.BlockSpec((TM, D), lambda i: (i, 0)),
                  pl.BlockSpec((1, 1, 6 * D), lambda i: (_mod_row(i, nt, nl, nb), 0, 0)),
                  full(2, 1, RW_DIM), full(2, 128, RW_DIM), full(1, RW_DIM), full(2, 1, RW_DIM),
                  full(128, RW_DIM), full(1, RW_DIM), full(1, RW_DIM), full(1, MB_DIM), full(1, MB_DIM),
                  full(D, D)],
        out_specs=pl.BlockSpec((TM, D), lambda i: (i, 0)),
        out_shape=jax.ShapeDtypeStruct((rows, D), F32),
        compiler_params=_cparams(("arbitrary",)),
        name="mix_finish",
    )(z2, z2, yrw[0], yrw[1], ymb[0], ymb[1], h2, mods_l, a0, wla, kaw, rk, gup, gnw, gnb, mbd, mbn, wout)


def _ffn_kernel(h_ref, mod_ref, modc_ref, g_ref, w1_ref, w3_ref, w2_ref, o_ref, *, ntb, n_lat_last):
    is_ctx = _ctx_rows(ntb, n_lat_last, TMB)
    h = h_ref[...]
    a = _norm_mod(h, g_ref[...], _row_mod(mod_ref, modc_ref, 3 * D, is_ctx),
                  _row_mod(mod_ref, modc_ref, 4 * D, is_ctx)).astype(BF16)
    acc = jnp.zeros((TMB, D), F32)
    for c in range(D_FF // TF):
        sl = slice(TF * c, TF * (c + 1))
        hid = _silu(jnp.dot(a, w1_ref[:, sl], preferred_element_type=F32)) * jnp.dot(
            a, w3_ref[:, sl], preferred_element_type=F32)
        acc = acc + jnp.dot(hid.astype(BF16), w2_ref[sl, :], preferred_element_type=F32)
    o_ref[...] = h + _row_mod(mod_ref, modc_ref, 5 * D, is_ctx) * acc


def _ffn(h2, mods_l, g, w1, w3, w2, nb, ntb, n_lat_last):
    rows = h2.shape[0]
    return pl.pallas_call(
        functools.partial(_ffn_kernel, ntb=ntb, n_lat_last=n_lat_last),
        grid=(rows // TMB,),
        in_specs=_big_tile_specs(nb, ntb) + [_resident((D, D_FF)), _resident((D, D_FF)), _resident((D_FF, D))],
        out_specs=pl.BlockSpec((TMB, D), lambda i: (i, 0)),
        out_shape=jax.ShapeDtypeStruct((rows, D), F32),
        compiler_params=_cparams(("arbitrary",)),
        name="ffn_swiglu",
    )(h2, mods_l, mods_l, g, w1, w3, w2)


def _attn_kernel(q_ref, k_ref, v_ref, kc_ref, vc_ref, bias_ref, o_ref, *, n_rows):
    r = pl.program_id(1)
    rs = jnp.clip(r - NA_KH // 2, 0, n_rows - NA_KH)
    start = pl.multiple_of(rs * GRID_W, GRID_W)
    nk = NA_KH * GRID_W
    nc = kc_ref.shape[1]
    low = lax.broadcasted_iota(jnp.int32, (GRID_W, 128), 1) < HEAD
    for pair in range(NA_HEADS // 2):
        sl = slice(128 * pair, 128 * (pair + 1))
        q = q_ref[0, :, sl] * (HEAD ** -0.5)
        kp = k_ref[0, pl.ds(start, nk), sl]
        kcp = kc_ref[0, :, sl]
        v1 = jnp.concatenate([v_ref[0, pl.ds(start, nk), sl], jnp.ones((nk, 128), BF16)], axis=-1)
        vc1 = jnp.concatenate([vc_ref[0, :, sl], jnp.ones((nc, 128), BF16)], axis=-1)
        halves = []
        for half in range(2):
            qh = jnp.where(low if half == 0 else jnp.logical_not(low), q, jnp.zeros_like(q))
            s_lat = _bdot_nt(qh, kp) + bias_ref[0, 2 * pair + half]
            s_ctx = _bdot_nt(qh, kcp)
            m = s_ctx
            for j in range(nk // nc):
                m = jnp.maximum(m, s_lat[:, nc * j:nc * (j + 1)])
            m = jnp.max(m, axis=-1, keepdims=True)
            ov = _bdot(jnp.exp(s_lat - m), v1) + _bdot(jnp.exp(s_ctx - m), vc1)
            halves.append(ov[:, :128] / ov[:, 128:])
        o_ref[0, :, sl] = jnp.where(low, halves[0], halves[1]).astype(o_ref.dtype)


def _attention(qkv, bias, seq, ctx_len):
    nb = qkv.shape[0]
    n_rows = seq // GRID_W
    nk = NA_KH * GRID_W

    def cfg(b, r):
        return r - jnp.clip(r - NA_KH // 2, 0, n_rows - NA_KH)

    return pl.pallas_call(
        functools.partial(_attn_kernel, n_rows=n_rows),
        grid=(nb, n_rows),
        in_specs=[pl.BlockSpec((1, GRID_W, D), lambda b, r: (b, r, 0)),
                  pl.BlockSpec((1, seq, D), lambda b, r: (b, 0, 1)),
                  pl.BlockSpec((1, seq, D), lambda b, r: (b, 0, 2)),
                  pl.BlockSpec((1, ctx_len, D), lambda b, r: (b, seq // ctx_len, 1)),
                  pl.BlockSpec((1, ctx_len, D), lambda b, r: (b, seq // ctx_len, 2)),
                  pl.BlockSpec((1, NA_HEADS, GRID_W, nk), lambda b, r: (cfg(b, r), 0, 0, 0))],
        out_specs=pl.BlockSpec((1, GRID_W, D), lambda b, r: (b, r, 0)),
        out_shape=jax.ShapeDtypeStruct((nb, seq, D), BF16),
        compiler_params=_cparams(("arbitrary", "arbitrary")),
        name="nbr_attention",
    )(qkv, qkv, qkv, qkv, qkv, bias)


def _attn_bias_table(rpb):
    q = np.arange(GRID_W)[:, None]
    c = np.arange(GRID_W)[None, :]
    ws = np.clip(q - NA_KW // 2, 0, GRID_W - NA_KW)
    inside = (c >= ws) & (c < ws + NA_KW)
    cidx = np.clip(c - q + NA_KW - 1, 0, 2 * NA_KW - 2)
    pick = np.zeros((2 * NA_KW - 1, GRID_W * GRID_W), np.float32)
    pick[cidx.reshape(-1), np.arange(GRID_W * GRID_W)] = 1.0
    n_dr = 2 * NA_KH - 1
    cols = jnp.dot(rpb.astype(F32).reshape(NA_HEADS * n_dr, 2 * NA_KW - 1), pick,
                   precision=lax.Precision.HIGHEST).reshape(NA_HEADS, n_dr, GRID_W, GRID_W)
    cols = jnp.where(inside[None, None], cols, -jnp.inf)
    tab = jnp.stack([cols[:, NA_KH - 1 - cfg:2 * NA_KH - 1 - cfg] for cfg in range(NA_KH)])
    return tab.transpose(0, 1, 3, 2, 4).reshape(NA_KH, NA_HEADS, GRID_W, NA_KH * GRID_W)


def _proj_res_kernel(o_ref, h_ref, mod_ref, w_ref, out_ref):
    out_ref[0] = h_ref[0] + mod_ref[0, :, 2 * D:3 * D] * jnp.dot(o_ref[0], w_ref[...],
                                                                 preferred_element_type=F32)


def _proj_res(o3, h3, mods_l, w, seq):
    nb = o3.shape[0]
    tm = 512
    return pl.pallas_call(
        _proj_res_kernel,
        grid=(nb, seq // tm),
        in_specs=[pl.BlockSpec((1, tm, D), lambda b, i: (b, i, 0)),
                  pl.BlockSpec((1, tm, D), lambda b, i: (b, i, 0)),
                  pl.BlockSpec((1, 1, 6 * D), lambda b, i: (b, 0, 0)),
                  pl.BlockSpec((D, D), lambda b, i: (0, 0))],
        out_specs=pl.BlockSpec((1, tm, D), lambda b, i: (b, i, 0)),
        out_shape=jax.ShapeDtypeStruct((nb, seq, D), F32),
        compiler_params=_cparams(("arbitrary", "arbitrary")),
        name="attn_outproj",
    )(o3, h3, mods_l, w)


MOE_TM = 512
FF_CHUNK = 1408
ROW_TILE = 8


def _to_token_tiles(ref, val):
    n = val.shape[0]
    for s in range(ROW_TILE):
        ref[pl.ds(s, n, stride=ROW_TILE), :] = val[:, 128 * s:128 * (s + 1)]


def _from_token_tiles(ref, n):
    return jnp.concatenate([ref[pl.ds(s, n, stride=ROW_TILE), :] for s in range(ROW_TILE)], axis=-1)


def _router_kernel(h_ref, mod_ref, g_ref, rw_ref, rb_ref, a8_ref, idx_ref, gate_ref):
    a = _norm_mod(h_ref[0], g_ref[...], mod_ref[0, :, 3 * D:4 * D], mod_ref[0, :, 4 * D:5 * D])
    _to_token_tiles(a8_ref, a)
    logits = _hdot(a, rw_ref[...]) + rb_ref[...]
    lane = lax.broadcasted_iota(jnp.int32, logits.shape, 1)
    logits = jnp.where(lane < N_EXPERTS, logits, -jnp.inf)
    m1 = jnp.max(logits, axis=-1, keepdims=True)
    i1 = jnp.min(jnp.where(logits == m1, lane, 128), axis=-1, keepdims=True)
    rest = jnp.where(lane == i1, -jnp.inf, logits)
    m2 = jnp.max(rest, axis=-1, keepdims=True)
    i2 = jnp.min(jnp.where(rest == m2, lane, 128), axis=-1, keepdims=True)
    ex = jnp.exp(m2 - m1)
    idx_ref[...] = jnp.where(lane == 0, i1, jnp.where(lane == 1, i2, 0))
    gate_ref[...] = jnp.where(lane == 0, 1.0 / (1.0 + ex), jnp.where(lane == 1, ex / (1.0 + ex), 0.0))


def _router(h3, mods_l, g, rw, rb):
    nb, seq, _ = h3.shape
    tm = 512
    ni = seq // tm
    rows = nb * seq
    return pl.pallas_call(
        _router_kernel,
        grid=(nb, ni),
        in_specs=[pl.BlockSpec((1, tm, D), lambda b, i: (b, i, 0)),
                  pl.BlockSpec((1, 1, 6 * D), lambda b, i: (b, 0, 0)),
                  pl.BlockSpec((1, D), lambda b, i: (0, 0)),
                  pl.BlockSpec((D, 128), lambda b, i: (0, 0)),
                  pl.BlockSpec((1, 128), lambda b, i: (0, 0))],
        out_specs=[pl.BlockSpec((tm * ROW_TILE, 128), lambda b, i: (b * ni + i, 0)),
                   pl.BlockSpec((tm, 128), lambda b, i: (b * ni + i, 0)),
                   pl.BlockSpec((tm, 128), lambda b, i: (b * ni + i, 0))],
        out_shape=[jax.ShapeDtypeStruct((rows * ROW_TILE, 128), F32),
                   jax.ShapeDtypeStruct((rows, 128), jnp.int32),
                   jax.ShapeDtypeStruct((rows, 128), F32)],
        compiler_params=_cparams(("arbitrary", "arbitrary")),
        name="moe_router",
    )(h3, mods_l, g, rw, rb)


def _route_plan(idx, rows):
    n_pairs = 2 * rows
    n_tiles = n_pairs // MOE_TM + N_EXPERTS
    expert =jnp.concatenate([idx[:, 0], idx[:, 1]])
    onehot = (expert[:, None] == jnp.arange(N_EXPERTS)[None, :]).astype(jnp.int32)
    cum = jnp.cumsum(onehot, axis=0)
    counts = cum[-1]
    rank = jnp.sum(onehot * cum, axis=1) - 1
    padded = (counts + MOE_TM - 1) // MOE_TM * MOE_TM
    ends = jnp.cumsum(padded)
    starts = ends - padded
    pos = starts[expert] + rank
    pair = jnp.arange(n_pairs, dtype=jnp.int32)
    dst_of_row = jnp.zeros((n_tiles * MOE_TM,), jnp.int32).at[pos].set(pair)
    tok_of_row = dst_of_row % rows
    tile_start = jnp.arange(n_tiles, dtype=jnp.int32) * MOE_TM
    tile_expert = jnp.minimum(jnp.sum((tile_start[:, None] >= ends[None, :]).astype(jnp.int32), axis=1),
                              N_EXPERTS - 1)
    n_valid = jnp.clip(starts[tile_expert] + counts[tile_expert] - tile_start, 0, MOE_TM)
    n_valid = jnp.where(tile_start < ends[-1], n_valid, 0)
    return tile_expert.astype(jnp.int32), n_valid.astype(jnp.int32), tok_of_row, dst_of_row


def _token_copy(src, src_tok, dst, dst_tok, sem, n_tok=1):
    n = n_tok * ROW_TILE
    return pltpu.make_async_copy(src.at[pl.ds(pl.multiple_of(src_tok * ROW_TILE, ROW_TILE), n)],
                                 dst.at[pl.ds(pl.multiple_of(dst_tok * ROW_TILE, ROW_TILE), n)], sem)


def _expert_kernel(texp_ref, nval_ref, tok_ref, dst_ref, a8_hbm, w1_ref, w3_ref, w2_ref, y8_hbm,
                   xbuf, ybuf, gsem, ssem):
    i = pl.program_id(0)
    nv = nval_ref[i]
    nxt = jnp.minimum(i + 1, pl.num_programs(0) - 1)
    has_next = (i + 1 < pl.num_programs(0)) & (nval_ref[nxt] > 0)

    def start_gather(tile, slot):
        for r in range(MOE_TM):
            _token_copy(a8_hbm, tok_ref[tile * MOE_TM + r], xbuf.at[slot], r, gsem.at[slot]).start()

    def wait_gather(slot):
        _token_copy(a8_hbm, 0, xbuf.at[slot], 0, gsem.at[slot], MOE_TM).wait()

    def wait_scatter(n):
        pltpu.make_async_copy(ybuf.at[pl.ds(0, n * ROW_TILE)], y8_hbm.at[pl.ds(0, n * ROW_TILE)], ssem).wait()

    @pl.when(nv > 0)
    def _():
        slot = i % 2

        @pl.when(i == 0)
        def _():
            start_gather(0, 0)

        wait_gather(slot)
        x = _from_token_tiles(xbuf.at[slot], MOE_TM).astype(BF16)
        start_gather(i + 1, 1 - slot)
        acc = jnp.zeros((MOE_TM, D), F32)
        for c in range(D_FF // FF_CHUNK):
            sl = slice(FF_CHUNK * c, FF_CHUNK * (c + 1))
            hid = _silu(jnp.dot(x, w1_ref[0, :, sl], preferred_element_type=F32)) * jnp.dot(
                x, w3_ref[0, :, sl], preferred_element_type=F32)
            acc = acc + jnp.dot(hid.astype(BF16), w2_ref[0, sl, :], preferred_element_type=F32)

        @pl.when(i > 0)
        def _():
            wait_scatter(nval_ref[jnp.maximum(i - 1, 0)])

        _to_token_tiles(ybuf, acc)

        def scatter(r, carry):
            _token_copy(ybuf, r, y8_hbm, dst_ref[i * MOE_TM + r], ssem).start()
            return carry

        lax.fori_loop(0, nv, scatter, 0)

        @pl.when(jnp.logical_not(has_next))
        def _():
            wait_scatter(nv)

    @pl.when((nv == 0) & (nval_ref[jnp.maximum(i - 1, 0)] > 0) & (i > 0))
    def _():
        wait_gather(i % 2)


def _experts(tile_expert, n_valid, tok_of_row, dst_of_row, a8, w1, w3, w2, rows):
    n_tiles = tile_expert.shape[0]
    wspec = lambda shape: pl.BlockSpec(shape, lambda i, te, nv, tk, ds: (te[i], 0, 0),
                                       pipeline_mode=pl.Buffered(1))
    return pl.pallas_call(
        _expert_kernel,
        grid_spec=pltpu.PrefetchScalarGridSpec(
            num_scalar_prefetch=4,
            grid=(n_tiles,),
            in_specs=[pl.BlockSpec(memory_space=pl.ANY),
                      wspec((1, D, D_FF)), wspec((1, D, D_FF)), wspec((1, D_FF, D))],
            out_specs=pl.BlockSpec(memory_space=pl.ANY),
            scratch_shapes=[pltpu.VMEM((2, MOE_TM * ROW_TILE, 128), F32), pltpu.VMEM((MOE_TM * ROW_TILE, 128), F32),
                            pltpu.SemaphoreType.DMA((2,)), pltpu.SemaphoreType.DMA]),
        out_shape=jax.ShapeDtypeStruct((2 * rows * ROW_TILE, 128), F32),
        compiler_params=_cparams(("arbitrary",)),
        name="moe_experts",
    )(tile_expert, n_valid, tok_of_row, dst_of_row, a8, w1, w3, w2)


def _combine_kernel(h_ref, mod_ref, gate_ref, y_ref, gf_ref, o_ref):
    tm = h_ref.shape[1]
    gates = gate_ref[...]
    y = gates[:, 0:1] * _from_token_tiles(y_ref.at[0], tm) + gates[:, 1:2] * _from_token_tiles(y_ref.at[1], tm)
    out = h_ref[0] + mod_ref[0, :, 5 * D:6 * D] * y
    out = out * lax.rsqrt(jnp.mean(out * out, axis=-1, keepdims=True) + NORM_EPS)
    o_ref[0] = out * gf_ref[...]


def _combine(h3, mods_l, gates, y8, gf):
    nb, seq, _ = h3.shape
    tm = 512
    ni = seq // tm
    return pl.pallas_call(
        _combine_kernel,
        grid=(nb, ni),
        in_specs=[pl.BlockSpec((1, tm, D), lambda b, i: (b, i, 0)),
                  pl.BlockSpec((1, 1, 6 * D), lambda b, i: (b, 0, 0)),
                  pl.BlockSpec((tm, 128), lambda b, i: (b * ni + i, 0)),
                  pl.BlockSpec((2, tm * ROW_TILE, 128), lambda b, i: (0, b * ni + i, 0)),
                  pl.BlockSpec((1, D), lambda b, i: (0, 0))],
        out_specs=pl.BlockSpec((1, tm, D), lambda b, i: (b, i, 0)),
        out_shape=jax.ShapeDtypeStruct((nb, seq, D), F32),
        compiler_params=_cparams(("arbitrary", "arbitrary")),
        name="moe_combine_norm",
    )(h3, mods_l, gates, y8, gf)


def _moe(h3, mods_l, g, rw, rb, w1, w3, w2, gf):
    nb, seq, _ = h3.shape
    rows = nb * seq
    a8, idx, gates = _router(h3, mods_l, g, rw, rb)
    tile_expert, n_valid, tok_of_row, dst_of_row = _route_plan(idx, rows)
    y8 = _experts(tile_expert, n_valid, tok_of_row, dst_of_row, a8, w1, w3, w2, rows)
    return _combine(h3, mods_l, gates, y8.reshape(2, rows * ROW_TILE, 128), gf)


def _pad_cols(a, n):
    return jnp.pad(a, [(0, 0)] * (a.ndim - 1) + [(0, n - a.shape[-1])])


def _mix_in_layout(w_in, rw_mu, conv_w, conv_b):
    w_rw = _pad_cols(w_in[:, :RW_COLS], RW_PAD)
    mb = w_in[:, RW_COLS:]
    w_mb = _pad_cols(jnp.concatenate([mb[:, MB_DIM:MB_DIM + MB_XBC], mb[:, :MB_DIM], mb[:, MB_DIM + MB_XBC:]],
                                     axis=1), MB_PAD)
    w = jnp.concatenate([w_rw, w_mb], axis=1).astype(BF16)
    mu_p = _pad_cols(rw_mu[0], RW_PAD)
    mu_n = _pad_cols(rw_mu[1], RW_PAD)
    zeros = jnp.zeros((RW_PAD,), F32)
    cf_rw = jnp.stack([zeros, mu_p, 1.0 - mu_p - mu_n, mu_n, zeros, zeros, zeros, zeros])
    ident = jnp.zeros((8, MB_PAD - MB_XBC), F32).at[2].set(1.0)
    cf_xbc = jnp.concatenate([conv_w, conv_b[None], jnp.zeros((2, MB_XBC), F32)], axis=0)
    cf = jnp.concatenate([cf_rw, cf_xbc, ident], axis=1)
    return w, cf


def _lora_pad(up, offset):
    out = jnp.zeros((2, 128, up.shape[-1]), F32)
    for d in range(2):
        out = out.at[d, offset + RW_LORA * d:offset + RW_LORA * (d + 1)].set(up[d])
    return out


def kernel(x, c, ctx, c_ctx, w_ada, b_ada, norm_mix, norm_ffn, norm_final, mix_w_in, mix_w_out, rw_mu, rw_w0,
           rw_w_up, rw_a0, rw_a_up, rw_g_up, rw_k_k, rw_k_a, rw_r_k, rw_gn_w, rw_gn_b, mb_conv_w, mb_conv_b,
           mb_dt_bias, mb_a_log, mb_d, mb_norm_w, ffn_w1, ffn_w3, ffn_w2, na_w_qkv, na_w_out, na_rpb,
           moe_router_w, moe_router_b, moe_w1, moe_w3, moe_w2):
    nb, seq, _ = x.shape
    ctx_len = ctx.shape[1]
    t = seq + ctx_len
    nt, nl = t // TM, seq // TM
    ntb = t // TMB
    n_lat_last = seq - TMB * (ntb - 1)
    assert seq % TM == 0 and ctx_len % TM == 0 and seq % ctx_len == 0 and nb < 16
    assert t % TMB == 0 and 0 < n_lat_last <= TMB

    cvec = jnp.zeros((16, D), F32).at[:nb].set(c).at[nb].set(c_ctx)
    mods = _ada(cvec, w_ada, b_ada).reshape(w_ada.shape[0], 16, 1, 6 * D)

    h = jnp.concatenate([x, ctx], axis=1).reshape(nb * t, D)

    w_in, cf = _mix_in_layout(mix_w_in[0], rw_mu[0], mb_conv_w[0], mb_conv_b[0])
    z = _inproj(h, mods[0], norm_mix[0][None], w_in, cf, nb, nt, nl, RW_PAD // TN, (RW_PAD + MB_XBC) // TN)
    z3 = z.reshape(nb, t, MIX_PAD)
    wlw = _lora_pad(rw_w_up[0], 0)
    wla = _lora_pad(rw_a_up[0], 2 * RW_LORA)
    w0 = rw_w0[0][:, None, :]
    a0 = rw_a0[0][:, None, :]
    yrw = _rwkv_scan(z3, w0, wlw, a0, wla, rw_k_k[0][None], rw_k_a[0][None], seq)
    dtb = _pad_cols(mb_dt_bias[0].reshape(1, 2 * MB_HEADS), 128)
    alog = _pad_cols(mb_a_log[0].reshape(1, 2 * MB_HEADS), 128)
    ymb = _ssd_scan(z3, dtb, alog, seq)
    gup = jnp.pad(rw_g_up[0], ((0, 128 - RW_LORA_G), (0, 0))).astype(BF16)
    yrw = [y.reshape(nb * t, RW_DIM) for y in yrw]
    ymb = [y.reshape(nb * t, MB_DIM) for y in ymb]
    h = _finish(z, yrw, ymb, h, mods[0], a0, wla,
                rw_k_a[0][None], rw_r_k[0].reshape(2, 1, RW_DIM), gup, rw_gn_w[0][None], rw_gn_b[0][None],
                jnp.repeat(mb_d[0], HEAD)[None], mb_norm_w[0][None], mix_w_out[0].astype(BF16), nb, nt, nl)
    h = _ffn(h, mods[0], norm_ffn[0][None], ffn_w1[0].astype(BF16), ffn_w3[0].astype(BF16),
             ffn_w2[0].astype(BF16), nb, ntb, n_lat_last)

    qkv = _nmm(h, mods[1], norm_mix[1][None], na_w_qkv[0].astype(BF16), nb, ntb, n_lat_last, BF16)
    o = _attention(qkv.reshape(nb, t, 3 * D), _attn_bias_table(na_rpb[0]), seq, ctx_len)
    mods1 = mods[1].reshape(16, 1, 6 * D)
    h3 = _proj_res(o, h.reshape(nb, t, D), mods1, na_w_out[0].astype(BF16), seq)
    rw = _pad_cols(moe_router_w[0], 128)
    rb = _pad_cols(moe_router_b[0][None], 128)
    return _moe(h3, mods1, norm_ffn[1][None], rw, rb, moe_w1[0].astype(BF16), moe_w3[0].astype(BF16),
                moe_w2[0].astype(BF16), norm_final[None])
```

```python
import functools
import math

import numpy as np
import jax
import jax.numpy as jnp
from jax import lax
from jax.experimental import pallas as pl
from jax.experimental.pallas import tpu as pltpu

F32 = jnp.float32
BF16 = jnp.bfloat16

D = 1024
NORM_EPS = 1e-6
GRID_W = 64

HEAD = 64
RW_DIM = 512
RW_HEADS = 8
RW_LORA = 32
RW_LORA_G = 96
RW_GN_EPS = 64e-5
RW_COLS = 3 * RW_DIM + 4 * RW_LORA + RW_LORA_G
RW_PAD = 1792
RW_CHUNK = 64

MB_DIM = 512
MB_HEADS = 8
MB_GROUPS = 2
MB_STATE = 128
MB_CONV = 5
MB_XBC = MB_DIM + 2 * MB_GROUPS * MB_STATE
MB_PAD = 1792
MB_CHUNK = 128

MIX_PAD = RW_PAD + MB_PAD

NA_HEADS = 16
NA_KH = 8
NA_KW = 16

D_FF = 2816
N_EXPERTS = 8

TM = 256
TMB = 768
HALO = 16
TN = 256
TF = 256
VMEM_LIMIT = 56 * 1024 * 1024


def _cparams(sem):
    return pltpu.CompilerParams(dimension_semantics=sem, vmem_limit_bytes=VMEM_LIMIT)


def _bdot(a, b):
    return jnp.dot(a.astype(BF16), b.astype(BF16), preferred_element_type=F32)


def _bdot_nt(a, b):
    return lax.dot_general(a.astype(BF16), b.astype(BF16), (((1,), (1,)), ((), ())),
                           preferred_element_type=F32)


def _hdot(a, b):
    return jnp.dot(a, b, precision=lax.Precision.HIGHEST, preferred_element_type=F32)


def _split3(x):
    p1 = x.astype(BF16)
    r1 = x - p1.astype(F32)
    p2 = r1.astype(BF16)
    return p1, p2, (r1 - p2.astype(F32)).astype(BF16)


def _dot3(a, b):
    ah, al, _ = _split3(a)
    bh, bl, _ = _split3(b)
    dot = functools.partial(jnp.dot, preferred_element_type=F32)
    return dot(ah, bh) + (dot(ah, bl) + dot(al, bh))


def _tri_cumsum(mask, x):
    tri = jnp.where(mask, 1.0, 0.0).astype(BF16)
    p1, p2, p3 = _split3(x)
    dot = functools.partial(jnp.dot, preferred_element_type=F32)
    return dot(tri, p1) + (dot(tri, p2) + dot(tri, p3))


def _bmm(spec, a, b):
    return jnp.einsum(spec, a.astype(BF16), b.astype(BF16), preferred_element_type=F32)


def _sigmoid(x):
    return 1.0 / (1.0 + jnp.exp(-x))


def _silu(x):
    return x * _sigmoid(x)


def _softplus(x):
    return jnp.maximum(x, 0.0) + jnp.log(1.0 + jnp.exp(-jnp.abs(x)))


def _norm_mod(h, g, shift, scale):
    hn = h * lax.rsqrt(jnp.mean(h * h, axis=-1, keepdims=True) + NORM_EPS)
    return (hn * g) * (1.0 + scale) + shift


def _seg64_sum(x):
    outs = []
    for p in range(x.shape[-1] // 128):
        xp = x[:, 128 * p:128 * (p + 1)]
        lo = lax.broadcasted_iota(jnp.int32, xp.shape, 1) < HEAD
        s_lo = jnp.sum(jnp.where(lo, xp, 0.0), axis=-1, keepdims=True)
        s_hi = jnp.sum(jnp.where(lo, 0.0, xp), axis=-1, keepdims=True)
        outs.append(jnp.where(lo, s_lo, s_hi))
    return jnp.concatenate(outs, axis=-1)


def _heads(x):
    return jnp.stack([x[:, HEAD * h:HEAD * (h + 1)] for h in range(x.shape[-1] // HEAD)], axis=0)


def _unheads(x):
    return jnp.concatenate([x[h] for h in range(x.shape[0])], axis=-1)


def _ada_kernel(c_ref, w_ref, b_ref, o_ref):
    o_ref[0] = _hdot(_silu(c_ref[...]), w_ref[0]) + b_ref[0]


def _ada(cvec, w_ada, b_ada):
    depth = w_ada.shape[0]
    tn = 1536
    return pl.pallas_call(
        _ada_kernel,
        grid=(depth, 6 * D // tn),
        in_specs=[pl.BlockSpec((16, D), lambda l, j: (0, 0)),
                  pl.BlockSpec((1, D, tn), lambda l, j: (l, 0, j)),
                  pl.BlockSpec((1, 1, tn), lambda l, j: (l, 0, j))],
        out_specs=pl.BlockSpec((1, 16, tn), lambda l, j: (l, 0, j)),
        out_shape=jax.ShapeDtypeStruct((depth, 16, 6 * D), F32),
        compiler_params=_cparams(("arbitrary", "arbitrary")),
        name="adaln",
    )(cvec, w_ada, b_ada.reshape(depth, 1, 6 * D))


def _mod_row(i, nt, nl, nb):
    return jnp.where(i % nt >= nl, nb, i // nt)


def _resident(shape):
    return pl.BlockSpec(shape, lambda *_: (0,) * len(shape), pipeline_mode=pl.Buffered(1))


def _inproj_kernel(hp_ref, h_ref, hn_ref, mod_ref, g_ref, w_ref, cf_ref, o_ref, a_scr, z_scr,
                   *, nt, nl, silu_lo, silu_hi):
    t = pl.program_id(0) % nt
    first = (t == 0) | (t == nl)
    last = (t == nl - 1) | (t == nt - 1)
    shift = mod_ref[0, :, 0:D]
    scale = mod_ref[0, :, D:2 * D]
    g = g_ref[...]
    a_scr[0:HALO] = jnp.where(first, 0.0, _norm_mod(hp_ref[...], g, shift, scale)).astype(BF16)
    a_scr[HALO:HALO + TM] = _norm_mod(h_ref[...], g, shift, scale).astype(BF16)
    a_scr[HALO + TM:] = jnp.where(last, 0.0, _norm_mod(hn_ref[...], g, shift, scale)).astype(BF16)
    a = a_scr[...]
    for j in range(w_ref.shape[1] // TN):
        sl = slice(TN * j, TN * (j + 1))
        z = z_scr.at[j % 2]
        z[...] = jnp.dot(a, w_ref[:, sl], preferred_element_type=F32)

        def tap(k):
            return cf_ref[k:k + 1, sl] * z[HALO - 2 + k:HALO - 2 + k + TM, :]

        if j < silu_lo:
            o_ref[:, sl] = tap(1) + tap(2) + tap(3)
        elif j < silu_hi:
            o_ref[:, sl] = _silu(cf_ref[5:6, sl] + tap(0) + tap(1) + tap(2) + tap(3) + tap(4))
        else:
            o_ref[:, sl] = z[HALO:HALO + TM, :]


def _inproj(h2, mods_l, g, w, cf, nb, nt, nl, silu_lo, silu_hi):
    rows = h2.shape[0]
    n = w.shape[1]
    hb = TM // HALO
    return pl.pallas_call(
        functools.partial(_inproj_kernel, nt=nt, nl=nl, silu_lo=silu_lo, silu_hi=silu_hi),
        grid=(rows // TM,),
        in_specs=[pl.BlockSpec((HALO, D), lambda i: (jnp.maximum(i * hb - 1, 0), 0)),
                  pl.BlockSpec((TM, D), lambda i: (i, 0)),
                  pl.BlockSpec((HALO, D), lambda i: (jnp.minimum((i + 1) * hb, rows // HALO - 1), 0)),
                  pl.BlockSpec((1, 1, 6 * D), lambda i: (_mod_row(i, nt, nl, nb), 0, 0)),
                  _resident((1, D)), _resident((D, n)), _resident((8, n))],
        out_specs=pl.BlockSpec((TM, n), lambda i: (i, 0)),
        out_shape=jax.ShapeDtypeStruct((rows, n), F32),
        scratch_shapes=[pltpu.VMEM((TM + 2 * HALO, D), BF16), pltpu.VMEM((2, TM + 2 * HALO, TN), F32)],
        compiler_params=_cparams(("arbitrary",)),
        name="mix_inproj",
    )(h2, h2, h2, mods_l, g, w, cf)


def _ctx_rows(ntb, n_lat_last, tm):
    row = lax.broadcasted_iota(jnp.int32, (tm, 1), 0)
    return (pl.program_id(0) % ntb == ntb - 1) & (row >= n_lat_last)


def _row_mod(mod_ref, modc_ref, lo, is_ctx):
    return jnp.where(is_ctx, modc_ref[0, :, lo:lo + D], mod_ref[0, :, lo:lo + D])


def _big_tile_specs(nb, ntb):
    return [pl.BlockSpec((TMB, D), lambda i: (i, 0)),
            pl.BlockSpec((1, 1, 6 * D), lambda i: (i // ntb, 0, 0)),
            pl.BlockSpec((1, 1, 6 * D), lambda i: (nb, 0, 0)),
            _resident((1, D))]


def _nmm_kernel(h_ref, mod_ref, modc_ref, g_ref, w_ref, o_ref, *, ntb, n_lat_last):
    is_ctx = _ctx_rows(ntb, n_lat_last, TMB)
    a = _norm_mod(h_ref[...], g_ref[...], _row_mod(mod_ref, modc_ref, 0, is_ctx),
                  _row_mod(mod_ref, modc_ref, D, is_ctx)).astype(BF16)
    tn = 2 * TN
    for j in range(w_ref.shape[1] // tn):
        sl = slice(tn * j, tn * (j + 1))
        o_ref[:, sl] = jnp.dot(a, w_ref[:, sl], preferred_element_type=F32).astype(o_ref.dtype)


def _nmm(h2, mods_l, g, w, nb, ntb, n_lat_last, out_dtype):
    rows = h2.shape[0]
    n = w.shape[1]
    return pl.pallas_call(
        functools.partial(_nmm_kernel, ntb=ntb, n_lat_last=n_lat_last),
        grid=(rows // TMB,),
        in_specs=_big_tile_specs(nb, ntb) + [_resident((D, n))],
        out_specs=pl.BlockSpec((TMB, n), lambda i: (i, 0)),
        out_shape=jax.ShapeDtypeStruct((rows, n), out_dtype),
        compiler_params=_cparams(("arbitrary",)),
        name="norm_proj",
    )(h2, mods_l, mods_l, g, w)


def _rwkv_kernel(zf_ref, zb_ref, w0_ref, wlw_ref, a0_ref, wla_ref, kk_ref, ka_ref, yf_ref, yb_ref, s_scr):
    @pl.when(pl.program_id(1) == 0)
    def _():
        s_scr[...] = jnp.zeros_like(s_scr)

    prep = [_rwkv_prep(z_ref[0], w0_ref[d], wlw_ref[d], a0_ref[d], wla_ref[d], kk_ref[...], ka_ref[...],
                       backward=d == 1) for d, z_ref in enumerate((zf_ref, zb_ref))]
    y, s_new = _rwkv_chain(*[jnp.concatenate(parts, axis=0) for parts in zip(*prep)], s_scr[...])
    yf_ref[0] = _unheads(y[:RW_HEADS])
    yb_ref[0] = _unheads(y[RW_HEADS:])
    s_scr[...] = s_new


def _rwkv_prep(z, w0, wlw, a0, wla, kkw, kaw, backward):
    L = RW_CHUNK
    r = z[:, 0:RW_DIM]
    k = z[:, RW_DIM:2 * RW_DIM]
    v = z[:, 2 * RW_DIM:3 * RW_DIM]
    lora = z[:, 3 * RW_DIM:3 * RW_DIM + 128]
    logw = w0 + _dot3(jnp.tanh(lora), wlw)
    logdec = -math.exp(-0.5) * _sigmoid(logw)
    iclr = _sigmoid(a0 + _dot3(lora, wla))
    kk = k * kkw
    kk = kk / jnp.maximum(jnp.sqrt(_seg64_sum(kk * kk)), 1e-12)
    kdir = k * (1.0 + (iclr - 1.0) * kaw)
    bvec = kk * iclr

    row = lax.broadcasted_iota(jnp.int32, (L, L), 0)
    col = lax.broadcasted_iota(jnp.int32, (L, L), 1)
    ahead = col - row if backward else row - col
    incl = ahead >= 0
    strict = ahead > 0
    lc = _tri_cumsum(incl, logdec)
    ltot = jnp.sum(logdec, axis=0, keepdims=True)
    g_in = jnp.exp(lc)
    g_ex = jnp.exp(lc - logdec)
    g_inv = jnp.exp(-lc)
    g_tail = jnp.exp(ltot - lc)

    ar = _heads(jnp.concatenate([-kk * g_ex, r * g_in], axis=0))
    bk = _heads(jnp.concatenate([bvec * g_inv, kdir * g_inv], axis=0))
    bk_tail = _heads(jnp.concatenate([bvec * g_tail, kdir * g_tail], axis=0))
    vh = _heads(v)
    g_tot = _heads(jnp.exp(ltot))
    tri = lambda m: jnp.broadcast_to(jnp.where(m, 1.0, 0.0)[None], (RW_HEADS, L, L))
    return ar, bk, bk_tail, vh, g_tot, tri(incl), tri(strict)


def _rwkv_chain(ar, bk, bk_tail, vh, g_tot, incl, strict, s0):
    L = RW_CHUNK
    m1 = _bmm('hlk,hsk->hls', ar, bk)
    keep = lambda mask, blk: jnp.where(mask > 0.5, blk, 0.0)
    nmat = keep(strict, m1[:, :L, :L])
    a_ak = keep(strict, m1[:, :L, L:])
    m_r = jnp.concatenate([keep(incl, m1[:, L:, :L]), keep(incl, m1[:, L:, L:])], axis=2)
    m2 = _bmm('hlk,hvk->hlv', ar, s0)
    x = m2[:, :L] + _bmm('hls,hsv->hlv', a_ak, vh)
    p = nmat
    steps = int(math.log2(L))
    for i in range(steps):
        x = x + _bmm('hls,hsv->hlv', p, x)
        if i < steps - 1:
            p = _bmm('hls,hst->hlt', p, p)
    uv = jnp.concatenate([x, vh], axis=1)
    y = m2[:, L:] + _bmm('hls,hsv->hlv', m_r, uv)
    return y, s0 * g_tot + _bmm('hvl,hlk->hvk', jnp.swapaxes(uv, 1, 2), bk_tail)


def _scan_chunks(nc, ncl, col_block=0):
    return (lambda b, c: (b, (c + ncl) % nc, col_block)), (lambda b, c: (b, nc - 1 - c, col_block))


def _rwkv_scan(z3, w0, wlw, a0, wla, kkw, kaw, seq):
    nb, t, _ = z3.shape
    nc = t // RW_CHUNK
    fwd, bwd = _scan_chunks(nc, seq // RW_CHUNK)
    y_shape = jax.ShapeDtypeStruct((nb, t, RW_DIM), F32)
    return pl.pallas_call(
        _rwkv_kernel,
        grid=(nb, nc),
        in_specs=[pl.BlockSpec((1, RW_CHUNK, RW_PAD), fwd), pl.BlockSpec((1, RW_CHUNK, RW_PAD), bwd),
                  _resident((2, 1, RW_DIM)), _resident((2, 128, RW_DIM)), _resident((2, 1, RW_DIM)),
                  _resident((2, 128, RW_DIM)), _resident((1, RW_DIM)), _resident((1, RW_DIM))],
        out_specs=[pl.BlockSpec((1, RW_CHUNK, RW_DIM), fwd), pl.BlockSpec((1, RW_CHUNK, RW_DIM), bwd)],
        out_shape=[y_shape, y_shape],
        scratch_shapes=[pltpu.VMEM((2 * RW_HEADS, HEAD, HEAD), F32)],
        compiler_params=_cparams(("arbitrary", "arbitrary")),
        name="rwkv7_scan",
    )(z3, z3, w0, wlw, a0, wla, kkw, kaw)


def _ssd_kernel(zf_ref, zb_ref, dtb_ref, alog_ref, sel_ref, yf_ref, yb_ref, s_scr):
    @pl.when(pl.program_id(1) == 0)
    def _():
        s_scr[...] = jnp.zeros_like(s_scr)

    res = [_ssd_chunk(z_ref[0], dtb_ref[...], alog_ref[...], sel_ref[d], s_scr[d], d)
           for d, z_ref in enumerate((zf_ref, zb_ref))]
    for d, y_ref in enumerate((yf_ref, yb_ref)):
        y_ref[0] = res[d][0]
        s_scr[d] = res[d][1]


def _lane_bcast(x, sel):
    p1, p2, p3 = _split3(x)
    dot = functools.partial(jnp.dot, preferred_element_type=F32)
    return dot(p1, sel) + (dot(p2, sel) + dot(p3, sel))


def _ssd_chunk(z, dtb, alog, sel, s_all, d):
    L = MB_CHUNK
    xm = z[:, 0:MB_DIM]
    dt_all = _softplus(z[:, 1536:1664] + dtb)
    a_all = dt_all * (-jnp.exp(alog))
    row = lax.broadcasted_iota(jnp.int32, (L, L), 0)
    col = lax.broadcasted_iota(jnp.int32, (L, L), 1)
    incl = (col >= row) if d == 1 else (row >= col)
    cs = _tri_cumsum(incl, a_all)
    cs_t = cs.T
    tot = jnp.sum(a_all, axis=0, keepdims=True)
    cs_b = _lane_bcast(cs, sel)
    dt_b = _lane_bcast(dt_all, sel)
    gmat = []
    for g in range(MB_GROUPS):
        bg = z[:, 512 + 128 * g:640 + 128 * g]
        cg = z[:, 768 + 128 * g:896 + 128 * g]
        gmat.append((bg, cg, _bdot_nt(cg, bg)))
    outs, states = [], []
    for h in range(MB_HEADS):
        bg, cg, cb = gmat[h // (MB_HEADS // MB_GROUPS)]
        j = MB_HEADS * d + h
        cs_h = cs_b[:, 128 * h:128 * (h + 1)]
        cs_col = cs_h[:, :HEAD]
        tot_h = tot[:, j:j + 1]
        lmat = jnp.exp(jnp.where(incl, cs_h - cs_t[j:j + 1, :], -jnp.inf))
        xh = xm[:, HEAD * h:HEAD * (h + 1)] * dt_b[:, 128 * h:128 * h + HEAD]
        s0 = s_all[h]
        outs.append(_bdot(cb * lmat, xh) + jnp.exp(cs_col) * _bdot_nt(cg, s0))
        xd = xh * jnp.exp(tot_h - cs_col)
        states.append(s0 * jnp.exp(tot_h) + _bdot(xd.T, bg))
    return jnp.concatenate(outs, axis=-1), jnp.stack(states, axis=0)


def _ssd_scan(z3, dtb, alog, seq):
    nb, t, _ = z3.shape
    nc = t // MB_CHUNK
    fwd, bwd = _scan_chunks(nc, seq // MB_CHUNK, col_block=1)
    yf, yb = _scan_chunks(nc, seq // MB_CHUNK)
    y_shape = jax.ShapeDtypeStruct((nb, t, MB_DIM), F32)
    return pl.pallas_call(
        _ssd_kernel,
        grid=(nb, nc),
        in_specs=[pl.BlockSpec((1, MB_CHUNK, MB_PAD), fwd), pl.BlockSpec((1, MB_CHUNK, MB_PAD), bwd),
                  _resident((1, 128)), _resident((1, 128)), _resident((2, 128, MB_HEADS * 128))],
        out_specs=[pl.BlockSpec((1, MB_CHUNK, MB_DIM), yf), pl.BlockSpec((1, MB_CHUNK, MB_DIM), yb)],
        out_shape=[y_shape, y_shape],
        scratch_shapes=[pltpu.VMEM((2, MB_HEADS, HEAD, MB_STATE), F32)],
        compiler_params=_cparams(("arbitrary", "arbitrary")),
        name="ssd_scan",
    )(z3, z3, dtb, alog, jnp.asarray(_head_selector(), BF16))


def _head_selector():
    sel = np.zeros((2, 128, MB_HEADS * 128), np.float32)
    for d in range(2):
        for h in range(MB_HEADS):
            sel[d, MB_HEADS * d + h, 128 * h:128 * (h + 1)] = 1.0
    return sel


def _finish_kernel(zr_ref, zm_ref, yrf_ref, yrb_ref, ymf_ref, ymb_ref, h_ref, mod_ref, a0_ref, wla_ref, ka_ref,
                   rk_ref, gup_ref,
                   gnw_ref, gnb_ref, mbd_ref, mbn_ref, wout_ref, o_ref):
    zr = zr_ref[...]
    r = zr[:, 0:RW_DIM]
    k = zr[:, RW_DIM:2 * RW_DIM]
    v = zr[:, 2 * RW_DIM:3 * RW_DIM]
    lora = zr[:, 3 * RW_DIM:3 * RW_DIM + 128]
    gate = _bdot(_sigmoid(zr[:, 3 * RW_DIM + 128:3 * RW_DIM + 256]), gup_ref[...])
    bonus = jnp.zeros_like(r)
    for dd in range(2):
        iclr = _sigmoid(a0_ref[dd] + _hdot(lora, wla_ref[dd]))
        bonus = bonus + r * (k * (1.0 + (iclr - 1.0) * ka_ref[...])) * rk_ref[dd]
    bonus = _seg64_sum(bonus) * v
    y = yrf_ref[...] + yrb_ref[...]
    mu = _seg64_sum(y) * (1.0 / HEAD)
    yc = y - mu
    var = _seg64_sum(yc * yc) * (1.0 / HEAD)
    y = yc * lax.rsqrt(var + RW_GN_EPS) * gnw_ref[...] + gnb_ref[...]
    o_rw = (y + bonus) * gate

    zm = zm_ref[...]
    xm = zm[:, 0:MB_DIM]
    zg = zm[:, 1024:1536]
    ym = (ymf_ref[...] + ymb_ref[...] + mbd_ref[...] * xm) * _silu(zg)
    gw = MB_DIM // MB_GROUPS
    parts = []
    for g in range(MB_GROUPS):
        yg = ym[:, gw * g:gw * (g + 1)]
        parts.append(yg * lax.rsqrt(jnp.mean(yg * yg, axis=-1, keepdims=True) + NORM_EPS))
    o_mb = jnp.concatenate(parts, axis=-1) * mbn_ref[...]
    o = jnp.concatenate([o_rw, o_mb], axis=-1)
    o_ref[...] = h_ref[...] + mod_ref[0, :, 2 * D:3 * D] * _bdot(o, wout_ref[...])


def _finish(z2, yrw, ymb, h2, mods_l, a0, wla, kaw, rk, gup, gnw, gnb, mbd, mbn, wout, nb, nt, nl):
    rows = h2.shape[0]
    full = lambda *shape: _resident(shape)
    half = pl.BlockSpec((TM, RW_DIM), lambda i: (i, 0))
    return pl.pallas_call(
        _finish_kernel,
        grid=(rows // TM,),
        in_specs=[pl.BlockSpec((TM, RW_PAD), lambda i: (i, 0)),
                  pl.BlockSpec((TM, MB_PAD), lambda i: (i, 1)),
                  half, half, half, half,
                  pl.BlockSpec((TM, D), lambda i: (i, 0)),
                  pl.BlockSpec((1, 1, 6 * D), lambda i: (_mod_row(i, nt, nl, nb), 0, 0)),
                  full(2, 1, RW_DIM), full(2, 128, RW_DIM), full(1, RW_DIM), full(2, 1, RW_DIM),
                  full(128, RW_DIM), full(1, RW_DIM), full(1, RW_DIM), full(1, MB_DIM), full(1, MB_DIM),
                  full(D, D)],
        out_specs=pl.BlockSpec((TM, D), lambda i: (i, 0)),
        out_shape=jax.ShapeDtypeStruct((rows, D), F32),
        compiler_params=_cparams(("arbitrary",)),
        name="mix_finish",
    )(z2, z2, yrw[0], yrw[1], ymb[0], ymb[1], h2, mods_l, a0, wla, kaw, rk, gup, gnw, gnb, mbd, mbn, wout)


def _ffn_kernel(h_ref, mod_ref, modc_ref, g_ref, w1_ref, w3_ref, w2_ref, o_ref, *, ntb, n_lat_last):
    is_ctx = _ctx_rows(ntb, n_lat_last, TMB)
    h = h_ref[...]
    a = _norm_mod(h, g_ref[...], _row_mod(mod_ref, modc_ref, 3 * D, is_ctx),
                  _row_mod(mod_ref, modc_ref, 4 * D, is_ctx)).astype(BF16)
    acc = jnp.zeros((TMB, D), F32)
    for c in range(D_FF // TF):
        sl = slice(TF * c, TF * (c + 1))
        hid = _silu(jnp.dot(a, w1_ref[:, sl], preferred_element_type=F32)) * jnp.dot(
            a, w3_ref[:, sl], preferred_element_type=F32)
        acc = acc + jnp.dot(hid.astype(BF16), w2_ref[sl, :], preferred_element_type=F32)
    o_ref[...] = h + _row_mod(mod_ref, modc_ref, 5 * D, is_ctx) * acc


def _ffn(h2, mods_l, g, w1, w3, w2, nb, ntb, n_lat_last):
    rows = h2.shape[0]
    return pl.pallas_call(
        functools.partial(_ffn_kernel, ntb=ntb, n_lat_last=n_lat_last),
        grid=(rows // TMB,),
        in_specs=_big_tile_specs(nb, ntb) + [_resident((D, D_FF)), _resident((D, D_FF)), _resident((D_FF, D))],
        out_specs=pl.BlockSpec((TMB, D), lambda i: (i, 0)),
        out_shape=jax.ShapeDtypeStruct((rows, D), F32),
        compiler_params=_cparams(("arbitrary",)),
        name="ffn_swiglu",
    )(h2, mods_l, mods_l, g, w1, w3, w2)


def _attn_kernel(q_ref, k_ref, v_ref, kc_ref, vc_ref, bias_ref, o_ref, *, n_rows):
    r = pl.program_id(1)
    rs = jnp.clip(r - NA_KH // 2, 0, n_rows - NA_KH)
    start = pl.multiple_of(rs * GRID_W, GRID_W)
    nk = NA_KH * GRID_W
    nc = kc_ref.shape[1]
    low = lax.broadcasted_iota(jnp.int32, (GRID_W, 128), 1) < HEAD
    for pair in range(NA_HEADS // 2):
        sl = slice(128 * pair, 128 * (pair + 1))
        q = q_ref[0, :, sl] * (HEAD ** -0.5)
        kp = k_ref[0, pl.ds(start, nk), sl]
        kcp = kc_ref[0, :, sl]
        v1 = jnp.concatenate([v_ref[0, pl.ds(start, nk), sl], jnp.ones((nk, 128), BF16)], axis=-1)
        vc1 = jnp.concatenate([vc_ref[0, :, sl], jnp.ones((nc, 128), BF16)], axis=-1)
        halves = []
        for half in range(2):
            qh = jnp.where(low if half == 0 else jnp.logical_not(low), q, jnp.zeros_like(q))
            s_lat = _bdot_nt(qh, kp) + bias_ref[0, 2 * pair + half]
            s_ctx = _bdot_nt(qh, kcp)
            m = s_ctx
            for j in range(nk // nc):
                m = jnp.maximum(m, s_lat[:, nc * j:nc * (j + 1)])
            m = jnp.max(m, axis=-1, keepdims=True)
            ov = _bdot(jnp.exp(s_lat - m), v1) + _bdot(jnp.exp(s_ctx - m), vc1)
            halves.append(ov[:, :128] / ov[:, 128:])
        o_ref[0, :, sl] = jnp.where(low, halves[0], halves[1]).astype(o_ref.dtype)


def _attention(qkv, bias, seq, ctx_len):
    nb = qkv.shape[0]
    n_rows = seq // GRID_W
    nk = NA_KH * GRID_W

    def cfg(b, r):
        return r - jnp.clip(r - NA_KH // 2, 0, n_rows - NA_KH)

    return pl.pallas_call(
        functools.partial(_attn_kernel, n_rows=n_rows),
        grid=(nb, n_rows),
        in_specs=[pl.BlockSpec((1, GRID_W, D), lambda b, r: (b, r, 0)),
                  pl.BlockSpec((1, seq, D), lambda b, r: (b, 0, 1)),
                  pl.BlockSpec((1, seq, D), lambda b, r: (b, 0, 2)),
                  pl.BlockSpec((1, ctx_len, D), lambda b, r: (b, seq // ctx_len, 1)),
                  pl.BlockSpec((1, ctx_len, D), lambda b, r: (b, seq // ctx_len, 2)),
                  pl.BlockSpec((1, NA_HEADS, GRID_W, nk), lambda b, r: (cfg(b, r), 0, 0, 0))],
        out_specs=pl.BlockSpec((1, GRID_W, D), lambda b, r: (b, r, 0)),
        out_shape=jax.ShapeDtypeStruct((nb, seq, D), BF16),
        compiler_params=_cparams(("arbitrary", "arbitrary")),
        name="nbr_attention",
    )(qkv, qkv, qkv, qkv, qkv, bias)


def _attn_bias_table(rpb):
    q = np.arange(GRID_W)[:, None]
    c = np.arange(GRID_W)[None, :]
    ws = np.clip(q - NA_KW // 2, 0, GRID_W - NA_KW)
    inside = (c >= ws) & (c < ws + NA_KW)
    cidx = np.clip(c - q + NA_KW - 1, 0, 2 * NA_KW - 2)
    pick = np.zeros((2 * NA_KW - 1, GRID_W * GRID_W), np.float32)
    pick[cidx.reshape(-1), np.arange(GRID_W * GRID_W)] = 1.0
    n_dr = 2 * NA_KH - 1
    cols = jnp.dot(rpb.astype(F32).reshape(NA_HEADS * n_dr, 2 * NA_KW - 1), pick,
                   precision=lax.Precision.HIGHEST).reshape(NA_HEADS, n_dr, GRID_W, GRID_W)
    cols = jnp.where(inside[None, None], cols, -jnp.inf)
    tab = jnp.stack([cols[:, NA_KH - 1 - cfg:2 * NA_KH - 1 - cfg] for cfg in range(NA_KH)])
    return tab.transpose(0, 1, 3, 2, 4).reshape(NA_KH, NA_HEADS, GRID_W, NA_KH * GRID_W)


def _proj_res_kernel(o_ref, h_ref, mod_ref, w_ref, out_ref):
    out_ref[0] = h_ref[0] + mod_ref[0, :, 2 * D:3 * D] * jnp.dot(o_ref[0], w_ref[...],
                                                                 preferred_element_type=F32)


def _proj_res(o3, h3, mods_l, w, seq):
    nb = o3.shape[0]
    tm = 512
    return pl.pallas_call(
        _proj_res_kernel,
        grid=(nb, seq // tm),
        in_specs=[pl.BlockSpec((1, tm, D), lambda b, i: (b, i, 0)),
                  pl.BlockSpec((1, tm, D), lambda b, i: (b, i, 0)),
                  pl.BlockSpec((1, 1, 6 * D), lambda b, i: (b, 0, 0)),
                  pl.BlockSpec((D, D), lambda b, i: (0, 0))],
        out_specs=pl.BlockSpec((1, tm, D), lambda b, i: (b, i, 0)),
        out_shape=jax.ShapeDtypeStruct((nb, seq, D), F32),
        compiler_params=_cparams(("arbitrary", "arbitrary")),
        name="attn_outproj",
    )(o3, h3, mods_l, w)


MOE_TM = 512
FF_CHUNK = 1408
ROW_TILE = 8


def _to_token_tiles(ref, val):
    n = val.shape[0]
    for s in range(ROW_TILE):
        ref[pl.ds(s, n, stride=ROW_TILE), :] = val[:, 128 * s:128 * (s + 1)]


def _from_token_tiles(ref, n):
    return jnp.concatenate([ref[pl.ds(s, n, stride=ROW_TILE), :] for s in range(ROW_TILE)], axis=-1)


def _router_kernel(h_ref, mod_ref, g_ref, rw_ref, rb_ref, a8_ref, idx_ref, gate_ref):
    a = _norm_mod(h_ref[0], g_ref[...], mod_ref[0, :, 3 * D:4 * D], mod_ref[0, :, 4 * D:5 * D])
    _to_token_tiles(a8_ref, a)
    logits = _hdot(a, rw_ref[...]) + rb_ref[...]
    lane = lax.broadcasted_iota(jnp.int32, logits.shape, 1)
    logits = jnp.where(lane < N_EXPERTS, logits, -jnp.inf)
    m1 = jnp.max(logits, axis=-1, keepdims=True)
    i1 = jnp.min(jnp.where(logits == m1, lane, 128), axis=-1, keepdims=True)
    rest = jnp.where(lane == i1, -jnp.inf, logits)
    m2 = jnp.max(rest, axis=-1, keepdims=True)
    i2 = jnp.min(jnp.where(rest == m2, lane, 128), axis=-1, keepdims=True)
    ex = jnp.exp(m2 - m1)
    idx_ref[...] = jnp.where(lane == 0, i1, jnp.where(lane == 1, i2, 0))
    gate_ref[...] = jnp.where(lane == 0, 1.0 / (1.0 + ex), jnp.where(lane == 1, ex / (1.0 + ex), 0.0))


def _router(h3, mods_l, g, rw, rb):
    nb, seq, _ = h3.shape
    tm = 512
    ni = seq // tm
    rows = nb * seq
    return pl.pallas_call(
        _router_kernel,
        grid=(nb, ni),
        in_specs=[pl.BlockSpec((1, tm, D), lambda b, i: (b, i, 0)),
                  pl.BlockSpec((1, 1, 6 * D), lambda b, i: (b, 0, 0)),
                  pl.BlockSpec((1, D), lambda b, i: (0, 0)),
                  pl.BlockSpec((D, 128), lambda b, i: (0, 0)),
                  pl.BlockSpec((1, 128), lambda b, i: (0, 0))],
        out_specs=[pl.BlockSpec((tm * ROW_TILE, 128), lambda b, i: (b * ni + i, 0)),
                   pl.BlockSpec((tm, 128), lambda b, i: (b * ni + i, 0)),
                   pl.BlockSpec((tm, 128), lambda b, i: (b * ni + i, 0))],
        out_shape=[jax.ShapeDtypeStruct((rows * ROW_TILE, 128), F32),
                   jax.ShapeDtypeStruct((rows, 128), jnp.int32),
                   jax.ShapeDtypeStruct((rows, 128), F32)],
        compiler_params=_cparams(("arbitrary", "arbitrary")),
        name="moe_router",
    )(h3, mods_l, g, rw, rb)


def _route_plan(idx, rows):
    n_pairs = 2 * rows
    n_tiles = n_pairs // MOE_TM + N_EXPERTS
    expert =jnp.concatenate([idx[:, 0], idx[:, 1]])
    onehot = (expert[:, None] == jnp.arange(N_EXPERTS)[None, :]).astype(jnp.int32)
    cum = jnp.cumsum(onehot, axis=0)
    counts = cum[-1]
    rank = jnp.sum(onehot * cum, axis=1) - 1
    padded = (counts + MOE_TM - 1) // MOE_TM * MOE_TM
    ends = jnp.cumsum(padded)
    starts = ends - padded
    pos = starts[expert] + rank
    pair = jnp.arange(n_pairs, dtype=jnp.int32)
    real_dst = jnp.full((n_tiles * MOE_TM,), -1, jnp.int32).at[pos].set(pair)
    tok_of_row = jnp.where(real_dst >= 0, real_dst % rows, 0)
    row = jnp.arange(n_tiles * MOE_TM, dtype=jnp.int32)
    trash = n_pairs + (row // MOE_TM % 2) * MOE_TM + row % MOE_TM
    behind = jnp.concatenate([jnp.full((MOE_TM,), -1, jnp.int32), real_dst[:-MOE_TM]])
    dst_of_row = jnp.where(behind >= 0, behind, trash)
    tile_start = jnp.arange(n_tiles, dtype=jnp.int32) * MOE_TM
    tile_expert = jnp.minimum(jnp.sum((tile_start[:, None] >= ends[None, :]).astype(jnp.int32), axis=1),
                              N_EXPERTS - 1)
    n_valid = jnp.clip(starts[tile_expert] + counts[tile_expert] - tile_start, 0, MOE_TM)
    n_valid = jnp.where(tile_start < ends[-1], n_valid, 0)
    return tile_expert.astype(jnp.int32), n_valid.astype(jnp.int32), tok_of_row, dst_of_row


def _token_copy(src, src_tok, dst, dst_tok, sem, n_tok=1):
    n = n_tok * ROW_TILE
    return pltpu.make_async_copy(src.at[pl.ds(pl.multiple_of(src_tok * ROW_TILE, ROW_TILE), n)],
                                 dst.at[pl.ds(pl.multiple_of(dst_tok * ROW_TILE, ROW_TILE), n)], sem)


def _expert_kernel(texp_ref, nval_ref, tok_ref, dst_ref, a8_hbm, w1_ref, w3_ref, w2_ref, y8_hbm,
                   xbuf, ybuf, gsem, ssem):
    i = pl.program_id(0)
    nv = nval_ref[i]
    slot = i % 2

    def start_gather(tile, s):
        for r in range(MOE_TM):
            _token_copy(a8_hbm, tok_ref[tile * MOE_TM + r], xbuf.at[s], r, gsem.at[s]).start()

    def wait_gather(s):
        _token_copy(a8_hbm, 0, xbuf.at[s], 0, gsem.at[s], MOE_TM).wait()

    def start_scatter(s):
        for r in range(MOE_TM):
            _token_copy(ybuf.at[s], r, y8_hbm, dst_ref[i * MOE_TM + r], ssem.at[s]).start()

    def wait_scatter(s):
        _token_copy(ybuf.at[s], 0, y8_hbm, 0, ssem.at[s], MOE_TM).wait()

    @pl.when(nv > 0)
    def _():
        @pl.when(i == 0)
        def _():
            start_gather(0, 0)
            ybuf[...] = jnp.zeros_like(ybuf)
            n_slots = y8_hbm.shape[0] // ROW_TILE
            fill = _token_copy(ybuf.at[0], 0, y8_hbm, n_slots - MOE_TM, ssem.at[0], MOE_TM)
            fill.start()
            fill.wait()

        wait_gather(slot)
        x = _from_token_tiles(xbuf.at[slot], MOE_TM).astype(BF16)
        start_gather(i + 1, 1 - slot)
        start_scatter(1 - slot)
        acc = jnp.zeros((MOE_TM, D), F32)
        for c in range(D_FF // FF_CHUNK):
            sl = slice(FF_CHUNK * c, FF_CHUNK * (c + 1))
            hid = _silu(jnp.dot(x, w1_ref[0, :, sl], preferred_element_type=F32)) * jnp.dot(
                x, w3_ref[0, :, sl], preferred_element_type=F32)
            acc = acc + jnp.dot(hid.astype(BF16), w2_ref[0, sl, :], preferred_element_type=F32)

        @pl.when(i > 0)
        def _():
            wait_scatter(slot)

        _to_token_tiles(ybuf.at[slot], acc)

    @pl.when((nv == 0) & (i > 0) & (nval_ref[jnp.maximum(i - 1, 0)] > 0))
    def _():
        wait_gather(slot)
        start_scatter(1 - slot)
        wait_scatter(slot)
        wait_scatter(1 - slot)


def _experts(tile_expert, n_valid, tok_of_row, dst_of_row, a8, w1, w3, w2, rows):
    n_tiles = tile_expert.shape[0]
    wspec = lambda shape: pl.BlockSpec(shape, lambda i, te, nv, tk, ds: (te[i], 0, 0),
                                       pipeline_mode=pl.Buffered(1))
    buf = pltpu.VMEM((2, MOE_TM * ROW_TILE, 128), F32)
    return pl.pallas_call(
        _expert_kernel,
        grid_spec=pltpu.PrefetchScalarGridSpec(
            num_scalar_prefetch=4,
            grid=(n_tiles,),
            in_specs=[pl.BlockSpec(memory_space=pl.ANY),
                      wspec((1, D, D_FF)), wspec((1, D, D_FF)), wspec((1, D_FF, D))],
            out_specs=pl.BlockSpec(memory_space=pl.ANY),
            scratch_shapes=[buf, buf, pltpu.SemaphoreType.DMA((2,)), pltpu.SemaphoreType.DMA((2,))]),
        out_shape=jax.ShapeDtypeStruct(((2 * rows + 2 * MOE_TM) * ROW_TILE, 128), F32),
        compiler_params=_cparams(("arbitrary",)),
        name="moe_experts",
    )(tile_expert, n_valid, tok_of_row, dst_of_row, a8, w1, w3, w2)


def _combine_kernel(h_ref, mod_ref, gate_ref, y0_ref, y1_ref, gf_ref, o_ref):
    tm = h_ref.shape[1]
    gates = gate_ref[...]
    y = gates[:, 0:1] * _from_token_tiles(y0_ref, tm) + gates[:, 1:2] * _from_token_tiles(y1_ref, tm)
    out = h_ref[0] + mod_ref[0, :, 5 * D:6 * D] * y
    out = out * lax.rsqrt(jnp.mean(out * out, axis=-1, keepdims=True) + NORM_EPS)
    o_ref[0] = out * gf_ref[...]


def _combine(h3, mods_l, gates, y8, gf):
    nb, seq, _ = h3.shape
    tm = 512
    ni = seq // tm
    n_tok_tiles = nb * ni
    return pl.pallas_call(
        _combine_kernel,
        grid=(nb, ni),
        in_specs=[pl.BlockSpec((1, tm, D), lambda b, i: (b, i, 0)),
                  pl.BlockSpec((1, 1, 6 * D), lambda b, i: (b, 0, 0)),
                  pl.BlockSpec((tm, 128), lambda b, i: (b * ni + i, 0)),
                  pl.BlockSpec((tm * ROW_TILE, 128), lambda b, i: (b * ni + i, 0)),
                  pl.BlockSpec((tm * ROW_TILE, 128), lambda b, i: (n_tok_tiles + b * ni + i, 0)),
                  pl.BlockSpec((1, D), lambda b, i: (0, 0))],
        out_specs=pl.BlockSpec((1, tm, D), lambda b, i: (b, i, 0)),
        out_shape=jax.ShapeDtypeStruct((nb, seq, D), F32),
        compiler_params=_cparams(("arbitrary", "arbitrary")),
        name="moe_combine_norm",
    )(h3, mods_l, gates, y8, y8, gf)


def _moe(h3, mods_l, g, rw, rb, w1, w3, w2, gf):
    nb, seq, _ = h3.shape
    rows = nb * seq
    a8, idx, gates = _router(h3, mods_l, g, rw, rb)
    tile_expert, n_valid, tok_of_row, dst_of_row = _route_plan(idx, rows)
    y8 = _experts(tile_expert, n_valid, tok_of_row, dst_of_row, a8, w1, w3, w2, rows)
    return _combine(h3, mods_l, gates, y8, gf)


def _pad_cols(a, n):
    return jnp.pad(a, [(0, 0)] * (a.ndim - 1) + [(0, n - a.shape[-1])])


def _mix_in_layout(w_in, rw_mu, conv_w, conv_b):
    w_rw = _pad_cols(w_in[:, :RW_COLS], RW_PAD)
    mb = w_in[:, RW_COLS:]
    w_mb = _pad_cols(jnp.concatenate([mb[:, MB_DIM:MB_DIM + MB_XBC], mb[:, :MB_DIM], mb[:, MB_DIM + MB_XBC:]],
                                     axis=1), MB_PAD)
    w = jnp.concatenate([w_rw, w_mb], axis=1).astype(BF16)
    mu_p = _pad_cols(rw_mu[0], RW_PAD)
    mu_n = _pad_cols(rw_mu[1], RW_PAD)
    zeros = jnp.zeros((RW_PAD,), F32)
    cf_rw = jnp.stack([zeros, mu_p, 1.0 - mu_p - mu_n, mu_n, zeros, zeros, zeros, zeros])
    ident = jnp.zeros((8, MB_PAD - MB_XBC), F32).at[2].set(1.0)
    cf_xbc = jnp.concatenate([conv_w, conv_b[None], jnp.zeros((2, MB_XBC), F32)], axis=0)
    cf = jnp.concatenate([cf_rw, cf_xbc, ident], axis=1)
    return w, cf


def _lora_pad(up, offset):
    out = jnp.zeros((2, 128, up.shape[-1]), F32)
    for d in range(2):
        out = out.at[d, offset + RW_LORA * d:offset + RW_LORA * (d + 1)].set(up[d])
    return out


def kernel(x, c, ctx, c_ctx, w_ada, b_ada, norm_mix, norm_ffn, norm_final, mix_w_in, mix_w_out, rw_mu, rw_w0,
           rw_w_up, rw_a0, rw_a_up, rw_g_up, rw_k_k, rw_k_a, rw_r_k, rw_gn_w, rw_gn_b, mb_conv_w, mb_conv_b,
           mb_dt_bias, mb_a_log, mb_d, mb_norm_w, ffn_w1, ffn_w3, ffn_w2, na_w_qkv, na_w_out, na_rpb,
           moe_router_w, moe_router_b, moe_w1, moe_w3, moe_w2):
    nb, seq, _ = x.shape
    ctx_len = ctx.shape[1]
    t = seq + ctx_len
    nt, nl = t // TM, seq // TM
    ntb = t // TMB
    n_lat_last = seq - TMB * (ntb - 1)
    assert seq % TM == 0 and ctx_len % TM == 0 and seq % ctx_len == 0 and nb < 16
    assert t % TMB == 0 and 0 < n_lat_last <= TMB

    cvec = jnp.zeros((16, D), F32).at[:nb].set(c).at[nb].set(c_ctx)
    mods = _ada(cvec, w_ada, b_ada).reshape(w_ada.shape[0], 16, 1, 6 * D)

    h = jnp.concatenate([x, ctx], axis=1).reshape(nb * t, D)

    w_in, cf = _mix_in_layout(mix_w_in[0], rw_mu[0], mb_conv_w[0], mb_conv_b[0])
    z = _inproj(h, mods[0], norm_mix[0][None], w_in, cf, nb, nt, nl, RW_PAD // TN, (RW_PAD + MB_XBC) // TN)
    z3 = z.reshape(nb, t, MIX_PAD)
    wlw = _lora_pad(rw_w_up[0], 0)
    wla = _lora_pad(rw_a_up[0], 2 * RW_LORA)
    w0 = rw_w0[0][:, None, :]
    a0 = rw_a0[0][:, None, :]
    yrw = _rwkv_scan(z3, w0, wlw, a0, wla, rw_k_k[0][None], rw_k_a[0][None], seq)
    dtb = _pad_cols(mb_dt_bias[0].reshape(1, 2 * MB_HEADS), 128)
    alog = _pad_cols(mb_a_log[0].reshape(1, 2 * MB_HEADS), 128)
    ymb = _ssd_scan(z3, dtb, alog, seq)
    gup = jnp.pad(rw_g_up[0], ((0, 128 - RW_LORA_G), (0, 0))).astype(BF16)
    yrw = [y.reshape(nb * t, RW_DIM) for y in yrw]
    ymb = [y.reshape(nb * t, MB_DIM) for y in ymb]
    h = _finish(z, yrw, ymb, h, mods[0], a0, wla,
                rw_k_a[0][None], rw_r_k[0].reshape(2, 1, RW_DIM), gup, rw_gn_w[0][None], rw_gn_b[0][None],
                jnp.repeat(mb_d[0], HEAD)[None], mb_norm_w[0][None], mix_w_out[0].astype(BF16), nb, nt, nl)
    h = _ffn(h, mods[0], norm_ffn[0][None], ffn_w1[0].astype(BF16), ffn_w3[0].astype(BF16),
             ffn_w2[0].astype(BF16), nb, ntb, n_lat_last)

    qkv = _nmm(h, mods[1], norm_mix[1][None], na_w_qkv[0].astype(BF16), nb, ntb, n_lat_last, BF16)
    o = _attention(qkv.reshape(nb, t, 3 * D), _attn_bias_table(na_rpb[0]), seq, ctx_len)
    mods1 = mods[1].reshape(16, 1, 6 * D)
    h3 = _proj_res(o, h.reshape(nb, t, D), mods1, na_w_out[0].astype(BF16), seq)
    rw = _pad_cols(moe_router_w[0], 128)
    rb = _pad_cols(moe_router_b[0][None], 128)
    return _moe(h3, mods1, norm_ffn[1][None], rw, rb, moe_w1[0].astype(BF16), moe_w3[0].astype(BF16),
                moe_w2[0].astype(BF16), norm_final[None])
```

```python
import functools
import math

import numpy as np
import jax
import jax.numpy as jnp
from jax import lax
from jax.experimental import pallas as pl
from jax.experimental.pallas import tpu as pltpu

F32 = jnp.float32
BF16 = jnp.bfloat16

D = 1024
NORM_EPS = 1e-6
GRID_W = 64

HEAD = 64
RW_DIM = 512
RW_HEADS = 8
RW_LORA = 32
RW_LORA_G = 96
RW_GN_EPS = 64e-5
RW_COLS = 3 * RW_DIM + 4 * RW_LORA + RW_LORA_G
RW_PAD = 1792
RW_CHUNK = 64
RW_SEQ_PER_STEP = 2

MB_DIM = 512
MB_HEADS = 8
MB_GROUPS = 2
MB_STATE = 128
MB_CONV = 5
MB_XBC = MB_DIM + 2 * MB_GROUPS * MB_STATE
MB_PAD = 1792
MB_CHUNK = 128

MIX_PAD = RW_PAD + MB_PAD

NA_HEADS = 16
NA_KH = 8
NA_KW = 16

D_FF = 2816
N_EXPERTS = 8

TM = 256
TMB = 768
HALO = 16
TN = 256
TF = 256
VMEM_LIMIT = 56 * 1024 * 1024


def _cparams(sem):
    return pltpu.CompilerParams(dimension_semantics=sem, vmem_limit_bytes=VMEM_LIMIT)


def _bdot(a, b):
    return jnp.dot(a.astype(BF16), b.astype(BF16), preferred_element_type=F32)


def _bdot_nt(a, b):
    return lax.dot_general(a.astype(BF16), b.astype(BF16), (((1,), (1,)), ((), ())),
                           preferred_element_type=F32)


def _hdot(a, b):
    return jnp.dot(a, b, precision=lax.Precision.HIGHEST, preferred_element_type=F32)


def _split3(x):
    p1 = x.astype(BF16)
    r1 = x - p1.astype(F32)
    p2 = r1.astype(BF16)
    return p1, p2, (r1 - p2.astype(F32)).astype(BF16)


def _dot3(a, b):
    ah, al, _ = _split3(a)
    bh, bl, _ = _split3(b)
    dot = functools.partial(jnp.dot, preferred_element_type=F32)
    return dot(ah, bh) + (dot(ah, bl) + dot(al, bh))


def _tri_cumsum(mask, x):
    tri = jnp.where(mask, 1.0, 0.0).astype(BF16)
    p1, p2, p3 = _split3(x)
    dot = functools.partial(jnp.dot, preferred_element_type=F32)
    return dot(tri, p1) + (dot(tri, p2) + dot(tri, p3))


def _bmm(spec, a, b):
    return jnp.einsum(spec, a.astype(BF16), b.astype(BF16), preferred_element_type=F32)


def _sigmoid(x):
    return 1.0 / (1.0 + jnp.exp(-x))


def _silu(x):
    return x * _sigmoid(x)


def _softplus(x):
    return jnp.maximum(x, 0.0) + jnp.log(1.0 + jnp.exp(-jnp.abs(x)))


def _norm_mod(h, g, shift, scale):
    hn = h * lax.rsqrt(jnp.mean(h * h, axis=-1, keepdims=True) + NORM_EPS)
    return (hn * g) * (1.0 + scale) + shift


def _seg64_sum(x):
    outs = []
    for p in range(x.shape[-1] // 128):
        xp = x[:, 128 * p:128 * (p + 1)]
        lo = lax.broadcasted_iota(jnp.int32, xp.shape, 1) < HEAD
        s_lo = jnp.sum(jnp.where(lo, xp, 0.0), axis=-1, keepdims=True)
        s_hi = jnp.sum(jnp.where(lo, 0.0, xp), axis=-1, keepdims=True)
        outs.append(jnp.where(lo, s_lo, s_hi))
    return jnp.concatenate(outs, axis=-1)


def _heads(x):
    return jnp.stack([x[:, HEAD * h:HEAD * (h + 1)] for h in range(x.shape[-1] // HEAD)], axis=0)


def _unheads(x):
    return jnp.concatenate([x[h] for h in range(x.shape[0])], axis=-1)


def _ada_kernel(c_ref, w_ref, b_ref, o_ref):
    o_ref[0] = _hdot(_silu(c_ref[...]), w_ref[0]) + b_ref[0]


def _ada(cvec, w_ada, b_ada):
    depth = w_ada.shape[0]
    tn = 1536
    return pl.pallas_call(
        _ada_kernel,
        grid=(depth, 6 * D // tn),
        in_specs=[pl.BlockSpec((16, D), lambda l, j: (0, 0)),
                  pl.BlockSpec((1, D, tn), lambda l, j: (l, 0, j)),
                  pl.BlockSpec((1, 1, tn), lambda l, j: (l, 0, j))],
        out_specs=pl.BlockSpec((1, 16, tn), lambda l, j: (l, 0, j)),
        out_shape=jax.ShapeDtypeStruct((depth, 16, 6 * D), F32),
        compiler_params=_cparams(("arbitrary", "arbitrary")),
        name="adaln",
    )(cvec, w_ada, b_ada.reshape(depth, 1, 6 * D))


def _mod_row(i, nt, nl, nb):
    return jnp.where(i % nt >= nl, nb, i // nt)


def _resident(shape):
    return pl.BlockSpec(shape, lambda *_: (0,) * len(shape), pipeline_mode=pl.Buffered(1))


def _inproj_kernel(hp_ref, h_ref, hn_ref, mod_ref, g_ref, w_ref, cf_ref, o_ref, a_scr, z_scr,
                   *, nt, nl, silu_lo, silu_hi):
    t = pl.program_id(0) % nt
    first = (t == 0) | (t == nl)
    last = (t == nl - 1) | (t == nt - 1)
    shift = mod_ref[0, :, 0:D]
    scale = mod_ref[0, :, D:2 * D]
    g = g_ref[...]
    a_scr[0:HALO] = jnp.where(first, 0.0, _norm_mod(hp_ref[...], g, shift, scale)).astype(BF16)
    a_scr[HALO:HALO + TM] = _norm_mod(h_ref[...], g, shift, scale).astype(BF16)
    a_scr[HALO + TM:] = jnp.where(last, 0.0, _norm_mod(hn_ref[...], g, shift, scale)).astype(BF16)
    a = a_scr[...]
    for j in range(w_ref.shape[1] // TN):
        sl = slice(TN * j, TN * (j + 1))
        z = z_scr.at[j % 2]
        z[...] = jnp.dot(a, w_ref[:, sl], preferred_element_type=F32)

        def tap(k):
            return cf_ref[k:k + 1, sl] * z[HALO - 2 + k:HALO - 2 + k + TM, :]

        if j < silu_lo:
            o_ref[:, sl] = tap(1) + tap(2) + tap(3)
        elif j < silu_hi:
            o_ref[:, sl] = _silu(cf_ref[5:6, sl] + tap(0) + tap(1) + tap(2) + tap(3) + tap(4))
        else:
            o_ref[:, sl] = z[HALO:HALO + TM, :]


def _inproj(h2, mods_l, g, w, cf, nb, nt, nl, silu_lo, silu_hi):
    rows = h2.shape[0]
    n = w.shape[1]
    hb = TM // HALO
    return pl.pallas_call(
        functools.partial(_inproj_kernel, nt=nt, nl=nl, silu_lo=silu_lo, silu_hi=silu_hi),
        grid=(rows // TM,),
        in_specs=[pl.BlockSpec((HALO, D), lambda i: (jnp.maximum(i * hb - 1, 0), 0)),
                  pl.BlockSpec((TM, D), lambda i: (i, 0)),
                  pl.BlockSpec((HALO, D), lambda i: (jnp.minimum((i + 1) * hb, rows // HALO - 1), 0)),
                  pl.BlockSpec((1, 1, 6 * D), lambda i: (_mod_row(i, nt, nl, nb), 0, 0)),
                  _resident((1, D)), _resident((D, n)), _resident((8, n))],
        out_specs=pl.BlockSpec((TM, n), lambda i: (i, 0)),
        out_shape=jax.ShapeDtypeStruct((rows, n), F32),
        scratch_shapes=[pltpu.VMEM((TM + 2 * HALO, D), BF16), pltpu.VMEM((2, TM + 2 * HALO, TN), F32)],
        compiler_params=_cparams(("arbitrary",)),
        name="mix_inproj",
    )(h2, h2, h2, mods_l, g, w, cf)


def _ctx_rows(ntb, n_lat_last, tm):
    row = lax.broadcasted_iota(jnp.int32, (tm, 1), 0)
    return (pl.program_id(0) % ntb == ntb - 1) & (row >= n_lat_last)


def _row_mod(mod_ref, modc_ref, lo, is_ctx):
    return jnp.where(is_ctx, modc_ref[0, :, lo:lo + D], mod_ref[0, :, lo:lo + D])


def _big_tile_specs(nb, ntb):
    return [pl.BlockSpec((TMB, D), lambda i: (i, 0)),
            pl.BlockSpec((1, 1, 6 * D), lambda i: (i // ntb, 0, 0)),
            pl.BlockSpec((1, 1, 6 * D), lambda i: (nb, 0, 0)),
            _resident((1, D))]


def _nmm_kernel(h_ref, mod_ref, modc_ref, g_ref, w_ref, o_ref, *, ntb, n_lat_last):
    is_ctx = _ctx_rows(ntb, n_lat_last, TMB)
    a = _norm_mod(h_ref[...], g_ref[...], _row_mod(mod_ref, modc_ref, 0, is_ctx),
                  _row_mod(mod_ref, modc_ref, D, is_ctx)).astype(BF16)
    tn = 2 * TN
    for j in range(w_ref.shape[1] // tn):
        sl = slice(tn * j, tn * (j + 1))
        o_ref[:, sl] = jnp.dot(a, w_ref[:, sl], preferred_element_type=F32).astype(o_ref.dtype)


def _nmm(h2, mods_l, g, w, nb, ntb, n_lat_last, out_dtype):
    rows = h2.shape[0]
    n = w.shape[1]
    return pl.pallas_call(
        functools.partial(_nmm_kernel, ntb=ntb, n_lat_last=n_lat_last),
        grid=(rows // TMB,),
        in_specs=_big_tile_specs(nb, ntb) + [_resident((D, n))],
        out_specs=pl.BlockSpec((TMB, n), lambda i: (i, 0)),
        out_shape=jax.ShapeDtypeStruct((rows, n), out_dtype),
        compiler_params=_cparams(("arbitrary",)),
        name="norm_proj",
    )(h2, mods_l, mods_l, g, w)


def _rwkv_kernel(zf_ref, zb_ref, w0_ref, wlw_ref, a0_ref, wla_ref, kk_ref, ka_ref, yf_ref, yb_ref, s_scr):
    @pl.when(pl.program_id(1) == 0)
    def _():
        s_scr[...] = jnp.zeros_like(s_scr)

    n_seq = zf_ref.shape[0]
    prep = [_rwkv_prep(z_ref[q], w0_ref[d], wlw_ref[d], a0_ref[d], wla_ref[d], kk_ref[...], ka_ref[...],
                       backward=d == 1) for d, z_ref in enumerate((zf_ref, zb_ref)) for q in range(n_seq)]
    y, s_new = _rwkv_chain(*[jnp.concatenate(parts, axis=0) for parts in zip(*prep)], s_scr[...])
    for d, y_ref in enumerate((yf_ref, yb_ref)):
        for q in range(n_seq):
            lo = (d * n_seq + q) * RW_HEADS
            y_ref[q] = _unheads(y[lo:lo + RW_HEADS])
    s_scr[...] = s_new


def _rwkv_prep(z, w0, wlw, a0, wla, kkw, kaw, backward):
    L = RW_CHUNK
    r = z[:, 0:RW_DIM]
    k = z[:, RW_DIM:2 * RW_DIM]
    v = z[:, 2 * RW_DIM:3 * RW_DIM]
    lora = z[:, 3 * RW_DIM:3 * RW_DIM + 128]
    logw = w0 + _dot3(jnp.tanh(lora), wlw)
    logdec = -math.exp(-0.5) * _sigmoid(logw)
    iclr = _sigmoid(a0 + _dot3(lora, wla))
    kk = k * kkw
    kk = kk / jnp.maximum(jnp.sqrt(_seg64_sum(kk * kk)), 1e-12)
    kdir = k * (1.0 + (iclr - 1.0) * kaw)
    bvec = kk * iclr

    row = lax.broadcasted_iota(jnp.int32, (L, L), 0)
    col = lax.broadcasted_iota(jnp.int32, (L, L), 1)
    ahead = col - row if backward else row - col
    incl = ahead >= 0
    strict = ahead > 0
    lc = _tri_cumsum(incl, logdec)
    ltot = jnp.sum(logdec, axis=0, keepdims=True)
    g_in = jnp.exp(lc)
    g_ex = jnp.exp(lc - logdec)
    g_inv = jnp.exp(-lc)
    g_tail = jnp.exp(ltot - lc)

    ar = _heads(jnp.concatenate([-kk * g_ex, r * g_in], axis=0))
    bk = _heads(jnp.concatenate([bvec * g_inv, kdir * g_inv], axis=0))
    bk_tail = _heads(jnp.concatenate([bvec * g_tail, kdir * g_tail], axis=0))
    vh = _heads(v)
    g_tot = _heads(jnp.exp(ltot))
    tri = lambda m: jnp.broadcast_to(jnp.where(m, 1.0, 0.0)[None], (RW_HEADS, L, L))
    return ar, bk, bk_tail, vh, g_tot, tri(incl), tri(strict)


def _rwkv_chain(ar, bk, bk_tail, vh, g_tot, incl, strict, s0):
    L = RW_CHUNK
    m1 = _bmm('hlk,hsk->hls', ar, bk)
    keep = lambda mask, blk: jnp.where(mask > 0.5, blk, 0.0)
    nmat = keep(strict, m1[:, :L, :L])
    a_ak = keep(strict, m1[:, :L, L:])
    m_r = jnp.concatenate([keep(incl, m1[:, L:, :L]), keep(incl, m1[:, L:, L:])], axis=2)
    m2 = _bmm('hlk,hvk->hlv', ar, s0)
    x = m2[:, :L] + _bmm('hls,hsv->hlv', a_ak, vh)
    p = nmat
    steps = int(math.log2(L))
    for i in range(steps):
        x = x + _bmm('hls,hsv->hlv', p, x)
        if i < steps - 1:
            p = _bmm('hls,hst->hlt', p, p)
    uv = jnp.concatenate([x, vh], axis=1)
    y = m2[:, L:] + _bmm('hls,hsv->hlv', m_r, uv)
    return y, s0 * g_tot + _bmm('hvl,hlk->hvk', jnp.swapaxes(uv, 1, 2), bk_tail)


def _scan_chunks(nc, ncl, col_block=0):
    return (lambda b, c: (b, (c + ncl) % nc, col_block)), (lambda b, c: (b, nc - 1 - c, col_block))


def _rwkv_scan(z3, w0, wlw, a0, wla, kkw, kaw, seq):
    nb, t, _ = z3.shape
    nc = t // RW_CHUNK
    fwd, bwd = _scan_chunks(nc, seq // RW_CHUNK)
    y_shape = jax.ShapeDtypeStruct((nb, t, RW_DIM), F32)
    n_seq = RW_SEQ_PER_STEP if nb % RW_SEQ_PER_STEP == 0 else 1
    return pl.pallas_call(
        _rwkv_kernel,
        grid=(nb // n_seq, nc),
        in_specs=[pl.BlockSpec((n_seq, RW_CHUNK, RW_PAD), fwd), pl.BlockSpec((n_seq, RW_CHUNK, RW_PAD), bwd),
                  _resident((2, 1, RW_DIM)), _resident((2, 128, RW_DIM)), _resident((2, 1, RW_DIM)),
                  _resident((2, 128, RW_DIM)), _resident((1, RW_DIM)), _resident((1, RW_DIM))],
        out_specs=[pl.BlockSpec((n_seq, RW_CHUNK, RW_DIM), fwd), pl.BlockSpec((n_seq, RW_CHUNK, RW_DIM), bwd)],
        out_shape=[y_shape, y_shape],
        scratch_shapes=[pltpu.VMEM((2 * n_seq * RW_HEADS, HEAD, HEAD), F32)],
        compiler_params=_cparams(("arbitrary", "arbitrary")),
        name="rwkv7_scan",
    )(z3, z3, w0, wlw, a0, wla, kkw, kaw)


def _ssd_kernel(zf_ref, zb_ref, dtb_ref, alog_ref, sel_ref, yf_ref, yb_ref, s_scr):
    @pl.when(pl.program_id(1) == 0)
    def _():
        s_scr[...] = jnp.zeros_like(s_scr)

    res = [_ssd_chunk(z_ref[0], dtb_ref[...], alog_ref[...], sel_ref[d], s_scr[d], d)
           for d, z_ref in enumerate((zf_ref, zb_ref))]
    for d, y_ref in enumerate((yf_ref, yb_ref)):
        y_ref[0] = res[d][0]
        s_scr[d] = res[d][1]


def _lane_bcast(x, sel):
    p1, p2, p3 = _split3(x)
    dot = functools.partial(jnp.dot, preferred_element_type=F32)
    return dot(p1, sel) + (dot(p2, sel) + dot(p3, sel))


def _ssd_chunk(z, dtb, alog, sel, s_all, d):
    L = MB_CHUNK
    xm = z[:, 0:MB_DIM]
    dt_all = _softplus(z[:, 1536:1664] + dtb)
    a_all = dt_all * (-jnp.exp(alog))
    row = lax.broadcasted_iota(jnp.int32, (L, L), 0)
    col = lax.broadcasted_iota(jnp.int32, (L, L), 1)
    incl = (col >= row) if d == 1 else (row >= col)
    cs = _tri_cumsum(incl, a_all)
    cs_t = cs.T
    tot = jnp.sum(a_all, axis=0, keepdims=True)
    cs_b = _lane_bcast(cs, sel)
    dt_b = _lane_bcast(dt_all, sel)
    gmat = []
    for g in range(MB_GROUPS):
        bg = z[:, 512 + 128 * g:640 + 128 * g]
        cg = z[:, 768 + 128 * g:896 + 128 * g]
        gmat.append((bg, cg, _bdot_nt(cg, bg)))
    outs, states = [], []
    for h in range(MB_HEADS):
        bg, cg, cb = gmat[h // (MB_HEADS // MB_GROUPS)]
        j = MB_HEADS * d + h
        cs_h = cs_b[:, 128 * h:128 * (h + 1)]
        cs_col = cs_h[:, :HEAD]
        tot_h = tot[:, j:j + 1]
        lmat = jnp.exp(jnp.where(incl, cs_h - cs_t[j:j + 1, :], -jnp.inf))
        xh = xm[:, HEAD * h:HEAD * (h + 1)] * dt_b[:, 128 * h:128 * h + HEAD]
        s0 = s_all[h]
        outs.append(_bdot(cb * lmat, xh) + jnp.exp(cs_col) * _bdot_nt(cg, s0))
        xd = xh * jnp.exp(tot_h - cs_col)
        states.append(s0 * jnp.exp(tot_h) + _bdot(xd.T, bg))
    return jnp.concatenate(outs, axis=-1), jnp.stack(states, axis=0)


def _ssd_scan(z3, dtb, alog, seq):
    nb, t, _ = z3.shape
    nc = t // MB_CHUNK
    fwd, bwd = _scan_chunks(nc, seq // MB_CHUNK, col_block=1)
    yf, yb = _scan_chunks(nc, seq // MB_CHUNK)
    y_shape = jax.ShapeDtypeStruct((nb, t, MB_DIM), F32)
    return pl.pallas_call(
        _ssd_kernel,
        grid=(nb, nc),
        in_specs=[pl.BlockSpec((1, MB_CHUNK, MB_PAD), fwd), pl.BlockSpec((1, MB_CHUNK, MB_PAD), bwd),
                  _resident((1, 128)), _resident((1, 128)), _resident((2, 128, MB_HEADS * 128))],
        out_specs=[pl.BlockSpec((1, MB_CHUNK, MB_DIM), yf), pl.BlockSpec((1, MB_CHUNK, MB_DIM), yb)],
        out_shape=[y_shape, y_shape],
        scratch_shapes=[pltpu.VMEM((2, MB_HEADS, HEAD, MB_STATE), F32)],
        compiler_params=_cparams(("arbitrary", "arbitrary")),
        name="ssd_scan",
    )(z3, z3, dtb, alog, jnp.asarray(_head_selector(), BF16))


def _head_selector():
    sel = np.zeros((2, 128, MB_HEADS * 128), np.float32)
    for d in range(2):
        for h in range(MB_HEADS):
            sel[d, MB_HEADS * d + h, 128 * h:128 * (h + 1)] = 1.0
    return sel


def _finish_kernel(zr_ref, zm_ref, yrf_ref, yrb_ref, ymf_ref, ymb_ref, h_ref, mod_ref, a0_ref, wla_ref, ka_ref,
                   rk_ref, gup_ref,
                   gnw_ref, gnb_ref, mbd_ref, mbn_ref, wout_ref, o_ref):
    zr = zr_ref[...]
    r = zr[:, 0:RW_DIM]
    k = zr[:, RW_DIM:2 * RW_DIM]
    v = zr[:, 2 * RW_DIM:3 * RW_DIM]
    lora = zr[:, 3 * RW_DIM:3 * RW_DIM + 128]
    gate = _bdot(_sigmoid(zr[:, 3 * RW_DIM + 128:3 * RW_DIM + 256]), gup_ref[...])
    bonus = jnp.zeros_like(r)
    for dd in range(2):
        iclr = _sigmoid(a0_ref[dd] + _dot3(lora, wla_ref[dd]))
        bonus = bonus + r * (k * (1.0 + (iclr - 1.0) * ka_ref[...])) * rk_ref[dd]
    bonus = _seg64_sum(bonus) * v
    y = yrf_ref[...] + yrb_ref[...]
    mu = _seg64_sum(y) * (1.0 / HEAD)
    yc = y - mu
    var = _seg64_sum(yc * yc) * (1.0 / HEAD)
    y = yc * lax.rsqrt(var + RW_GN_EPS) * gnw_ref[...] + gnb_ref[...]
    o_rw = (y + bonus) * gate

    zm = zm_ref[...]
    xm = zm[:, 0:MB_DIM]
    zg = zm[:, 1024:1536]
    ym = (ymf_ref[...] + ymb_ref[...] + mbd_ref[...] * xm) * _silu(zg)
    gw = MB_DIM // MB_GROUPS
    parts = []
    for g in range(MB_GROUPS):
        yg = ym[:, gw * g:gw * (g + 1)]
        parts.append(yg * lax.rsqrt(jnp.mean(yg * yg, axis=-1, keepdims=True) + NORM_EPS))
    o_mb = jnp.concatenate(parts, axis=-1) * mbn_ref[...]
    o = jnp.concatenate([o_rw, o_mb], axis=-1)
    o_ref[...] = h_ref[...] + mod_ref[0, :, 2 * D:3 * D] * _bdot(o, wout_ref[...])


def _finish(z2, yrw, ymb, h2, mods_l, a0, wla, kaw, rk, gup, gnw, gnb, mbd, mbn, wout, nb, nt, nl):
    rows = h2.shape[0]
    full = lambda *shape: _resident(shape)
    half = pl.BlockSpec((TM, RW_DIM), lambda i: (i, 0))
    return pl.pallas_call(
        _finish_kernel,
        grid=(rows // TM,),
        in_specs=[pl.BlockSpec((TM, RW_PAD), lambda i: (i, 0)),
                  pl.BlockSpec((TM, MB_PAD), lambda i: (i, 1)),
                  half, half, half, half,
                  pl.BlockSpec((TM, D), lambda i: (i, 0)),
                  pl.BlockSpec((1, 1, 6 * D), lambda i: (_mod_row(i, nt, nl, nb), 0, 0)),
                  full(2, 1, RW_DIM), full(2, 128, RW_DIM), full(1, RW_DIM), full(2, 1, RW_DIM),
                  full(128, RW_DIM), full(1, RW_DIM), full(1, RW_DIM), full(1, MB_DIM), full(1, MB_DIM),
                  full(D, D)],
        out_specs=pl.BlockSpec((TM, D), lambda i: (i, 0)),
        out_shape=jax.ShapeDtypeStruct((rows, D), F32),
        compiler_params=_cparams(("arbitrary",)),
        name="mix_finish",
    )(z2, z2, yrw[0], yrw[1], ymb[0], ymb[1], h2, mods_l, a0, wla, kaw, rk, gup, gnw, gnb, mbd, mbn, wout)


def _ffn_kernel(h_ref, mod_ref, modc_ref, g_ref, w1_ref, w3_ref, w2_ref, o_ref, *, ntb, n_lat_last):
    is_ctx = _ctx_rows(ntb, n_lat_last, TMB)
    h = h_ref[...]
    a = _norm_mod(h, g_ref[...], _row_mod(mod_ref, modc_ref, 3 * D, is_ctx),
                  _row_mod(mod_ref, modc_ref, 4 * D, is_ctx)).astype(BF16)
    acc = jnp.zeros((TMB, D), F32)
    for c in range(D_FF // TF):
        sl = slice(TF * c, TF * (c + 1))
        hid = _silu(jnp.dot(a, w1_ref[:, sl], preferred_element_type=F32)) * jnp.dot(
            a, w3_ref[:, sl], preferred_element_type=F32)
        acc = acc + jnp.dot(hid.astype(BF16), w2_ref[sl, :], preferred_element_type=F32)
    o_ref[...] = h + _row_mod(mod_ref, modc_ref, 5 * D, is_ctx) * acc


def _ffn(h2, mods_l, g, w1, w3, w2, nb, ntb, n_lat_last):
    rows = h2.shape[0]
    return pl.pallas_call(
        functools.partial(_ffn_kernel, ntb=ntb, n_lat_last=n_lat_last),
        grid=(rows // TMB,),
        in_specs=_big_tile_specs(nb, ntb) + [_resident((D, D_FF)), _resident((D, D_FF)), _resident((D_FF, D))],
        out_specs=pl.BlockSpec((TMB, D), lambda i: (i, 0)),
        out_shape=jax.ShapeDtypeStruct((rows, D), F32),
        compiler_params=_cparams(("arbitrary",)),
        name="ffn_swiglu",
    )(h2, mods_l, mods_l, g, w1, w3, w2)


def _attn_kernel(q_ref, k_ref, v_ref, kc_ref, vc_ref, bias_ref, o_ref, *, n_rows):
    r = pl.program_id(1)
    rs = jnp.clip(r - NA_KH // 2, 0, n_rows - NA_KH)
    start = pl.multiple_of(rs * GRID_W, GRID_W)
    nk = NA_KH * GRID_W
    nc = kc_ref.shape[1]
    low = lax.broadcasted_iota(jnp.int32, (GRID_W, 128), 1) < HEAD
    for pair in range(NA_HEADS // 2):
        sl = slice(128 * pair, 128 * (pair + 1))
        q = q_ref[0, :, sl] * (HEAD ** -0.5)
        kp = k_ref[0, pl.ds(start, nk), sl]
        kcp = kc_ref[0, :, sl]
        v1 = jnp.concatenate([v_ref[0, pl.ds(start, nk), sl], jnp.ones((nk, 128), BF16)], axis=-1)
        vc1 = jnp.concatenate([vc_ref[0, :, sl], jnp.ones((nc, 128), BF16)], axis=-1)
        halves = []
        for half in range(2):
            qh = jnp.where(low if half == 0 else jnp.logical_not(low), q, jnp.zeros_like(q))
            s_lat = _bdot_nt(qh, kp) + bias_ref[0, 2 * pair + half]
            s_ctx = _bdot_nt(qh, kcp)
            m = s_ctx
            for j in range(nk // nc):
                m = jnp.maximum(m, s_lat[:, nc * j:nc * (j + 1)])
            m = jnp.max(m, axis=-1, keepdims=True)
            ov = _bdot(jnp.exp(s_lat - m), v1) + _bdot(jnp.exp(s_ctx - m), vc1)
            halves.append(ov[:, :128] / ov[:, 128:])
        o_ref[0, :, sl] = jnp.where(low, halves[0], halves[1]).astype(o_ref.dtype)


def _attention(qkv, bias, seq, ctx_len):
    nb = qkv.shape[0]
    n_rows = seq // GRID_W
    nk = NA_KH * GRID_W

    def cfg(b, r):
        return r - jnp.clip(r - NA_KH // 2, 0, n_rows - NA_KH)

    return pl.pallas_call(
        functools.partial(_attn_kernel, n_rows=n_rows),
        grid=(nb, n_rows),
        in_specs=[pl.BlockSpec((1, GRID_W, D), lambda b, r: (b, r, 0)),
                  pl.BlockSpec((1, seq, D), lambda b, r: (b, 0, 1)),
                  pl.BlockSpec((1, seq, D), lambda b, r: (b, 0, 2)),
                  pl.BlockSpec((1, ctx_len, D), lambda b, r: (b, seq // ctx_len, 1)),
                  pl.BlockSpec((1, ctx_len, D), lambda b, r: (b, seq // ctx_len, 2)),
                  pl.BlockSpec((1, NA_HEADS, GRID_W, nk), lambda b, r: (cfg(b, r), 0, 0, 0))],
        out_specs=pl.BlockSpec((1, GRID_W, D), lambda b, r: (b, r, 0)),
        out_shape=jax.ShapeDtypeStruct((nb, seq, D), BF16),
        compiler_params=_cparams(("arbitrary", "arbitrary")),
        name="nbr_attention",
    )(qkv, qkv, qkv, qkv, qkv, bias)


def _attn_bias_table(rpb):
    q = np.arange(GRID_W)[:, None]
    c = np.arange(GRID_W)[None, :]
    ws = np.clip(q - NA_KW // 2, 0, GRID_W - NA_KW)
    inside = (c >= ws) & (c < ws + NA_KW)
    cidx = np.clip(c - q + NA_KW - 1, 0, 2 * NA_KW - 2)
    pick = np.zeros((2 * NA_KW - 1, GRID_W * GRID_W), np.float32)
    pick[cidx.reshape(-1), np.arange(GRID_W * GRID_W)] = 1.0
    n_dr = 2 * NA_KH - 1
    cols = jnp.dot(rpb.astype(F32).reshape(NA_HEADS * n_dr, 2 * NA_KW - 1), pick,
                   precision=lax.Precision.HIGHEST).reshape(NA_HEADS, n_dr, GRID_W, GRID_W)
    cols = jnp.where(inside[None, None], cols, -jnp.inf)
    tab = jnp.stack([cols[:, NA_KH - 1 - cfg:2 * NA_KH - 1 - cfg] for cfg in range(NA_KH)])
    return tab.transpose(0, 1, 3, 2, 4).reshape(NA_KH, NA_HEADS, GRID_W, NA_KH * GRID_W)


def _proj_res_kernel(o_ref, h_ref, mod_ref, w_ref, out_ref):
    out_ref[0] = h_ref[0] + mod_ref[0, :, 2 * D:3 * D] * jnp.dot(o_ref[0], w_ref[...],
                                                                 preferred_element_type=F32)


def _proj_res(o3, h3, mods_l, w, seq):
    nb = o3.shape[0]
    tm = 512
    return pl.pallas_call(
        _proj_res_kernel,
        grid=(nb, seq // tm),
        in_specs=[pl.BlockSpec((1, tm, D), lambda b, i: (b, i, 0)),
                  pl.BlockSpec((1, tm, D), lambda b, i: (b, i, 0)),
                  pl.BlockSpec((1, 1, 6 * D), lambda b, i: (b, 0, 0)),
                  pl.BlockSpec((D, D), lambda b, i: (0, 0))],
        out_specs=pl.BlockSpec((1, tm, D), lambda b, i: (b, i, 0)),
        out_shape=jax.ShapeDtypeStruct((nb, seq, D), F32),
        compiler_params=_cparams(("arbitrary", "arbitrary")),
        name="attn_outproj",
    )(o3, h3, mods_l, w)


MOE_TM = 512
FF_CHUNK = 1408
ROW_TILE = 8


def _to_token_tiles(ref, val):
    n = val.shape[0]
    for s in range(ROW_TILE):
        ref[pl.ds(s, n, stride=ROW_TILE), :] = val[:, 128 * s:128 * (s + 1)]


def _from_token_tiles(ref, n):
    return jnp.concatenate([ref[pl.ds(s, n, stride=ROW_TILE), :] for s in range(ROW_TILE)], axis=-1)


def _router_kernel(h_ref, mod_ref, g_ref, rw_ref, rb_ref, a8_ref, idx_ref, gate_ref):
    a = _norm_mod(h_ref[0], g_ref[...], mod_ref[0, :, 3 * D:4 * D], mod_ref[0, :, 4 * D:5 * D])
    _to_token_tiles(a8_ref, a)
    logits = _hdot(a, rw_ref[...]) + rb_ref[...]
    lane = lax.broadcasted_iota(jnp.int32, logits.shape, 1)
    logits = jnp.where(lane < N_EXPERTS, logits, -jnp.inf)
    m1 = jnp.max(logits, axis=-1, keepdims=True)
    i1 = jnp.min(jnp.where(logits == m1, lane, 128), axis=-1, keepdims=True)
    rest = jnp.where(lane == i1, -jnp.inf, logits)
    m2 = jnp.max(rest, axis=-1, keepdims=True)
    i2 = jnp.min(jnp.where(rest == m2, lane, 128), axis=-1, keepdims=True)
    ex = jnp.exp(m2 - m1)
    idx_ref[...] = jnp.where(lane == 0, i1, jnp.where(lane == 1, i2, 0))
    gate_ref[...] = jnp.where(lane == 0, 1.0 / (1.0 + ex), jnp.where(lane == 1, ex / (1.0 + ex), 0.0))


def _router(h3, mods_l, g, rw, rb):
    nb, seq, _ = h3.shape
    tm = 512
    ni = seq // tm
    rows = nb * seq
    return pl.pallas_call(
        _router_kernel,
        grid=(nb, ni),
        in_specs=[pl.BlockSpec((1, tm, D), lambda b, i: (b, i, 0)),
                  pl.BlockSpec((1, 1, 6 * D), lambda b, i: (b, 0, 0)),
                  pl.BlockSpec((1, D), lambda b, i: (0, 0)),
                  pl.BlockSpec((D, 128), lambda b, i: (0, 0)),
                  pl.BlockSpec((1, 128), lambda b, i: (0, 0))],
        out_specs=[pl.BlockSpec((tm * ROW_TILE, 128), lambda b, i: (b * ni + i, 0)),
                   pl.BlockSpec((tm, 128), lambda b, i: (b * ni + i, 0)),
                   pl.BlockSpec((tm, 128), lambda b, i: (b * ni + i, 0))],
        out_shape=[jax.ShapeDtypeStruct((rows * ROW_TILE, 128), F32),
                   jax.ShapeDtypeStruct((rows, 128), jnp.int32),
                   jax.ShapeDtypeStruct((rows, 128), F32)],
        compiler_params=_cparams(("arbitrary", "arbitrary")),
        name="moe_router",
    )(h3, mods_l, g, rw, rb)


def _route_plan(idx, rows):
    n_pairs = 2 * rows
    n_tiles = n_pairs // MOE_TM + N_EXPERTS
    expert =jnp.concatenate([idx[:, 0], idx[:, 1]])
    onehot = (expert[:, None] == jnp.arange(N_EXPERTS)[None, :]).astype(jnp.int32)
    cum = jnp.cumsum(onehot, axis=0)
    counts = cum[-1]
    rank = jnp.sum(onehot * cum, axis=1) - 1
    padded = (counts + MOE_TM - 1) // MOE_TM * MOE_TM
    ends = jnp.cumsum(padded)
    starts = ends - padded
    pos = starts[expert] + rank
    pair = jnp.arange(n_pairs, dtype=jnp.int32)
    real_dst = jnp.full((n_tiles * MOE_TM,), -1, jnp.int32).at[pos].set(pair)
    tok_of_row = jnp.where(real_dst >= 0, real_dst % rows, 0)
    row = jnp.arange(n_tiles * MOE_TM, dtype=jnp.int32)
    trash = n_pairs + (row // MOE_TM % 2) * MOE_TM + row % MOE_TM
    behind = jnp.concatenate([jnp.full((MOE_TM,), -1, jnp.int32), real_dst[:-MOE_TM]])
    dst_of_row = jnp.where(behind >= 0, behind, trash)
    tile_start = jnp.arange(n_tiles, dtype=jnp.int32) * MOE_TM
    tile_expert = jnp.minimum(jnp.sum((tile_start[:, None] >= ends[None, :]).astype(jnp.int32), axis=1),
                              N_EXPERTS - 1)
    n_valid = jnp.clip(starts[tile_expert] + counts[tile_expert] - tile_start, 0, MOE_TM)
    n_valid = jnp.where(tile_start < ends[-1], n_valid, 0)
    return tile_expert.astype(jnp.int32), n_valid.astype(jnp.int32), tok_of_row, dst_of_row


def _token_copy(src, src_tok, dst, dst_tok, sem, n_tok=1):
    n = n_tok * ROW_TILE
    return pltpu.make_async_copy(src.at[pl.ds(pl.multiple_of(src_tok * ROW_TILE, ROW_TILE), n)],
                                 dst.at[pl.ds(pl.multiple_of(dst_tok * ROW_TILE, ROW_TILE), n)], sem)


def _expert_kernel(texp_ref, nval_ref, tok_ref, dst_ref, a8_hbm, w1_ref, w3_ref, w2_ref, y8_hbm,
                   xbuf, ybuf, gsem, ssem):
    i = pl.program_id(0)
    nv = nval_ref[i]
    slot = i % 2

    def start_gather(tile, s):
        for r in range(MOE_TM):
            _token_copy(a8_hbm, tok_ref[tile * MOE_TM + r], xbuf.at[s], r, gsem.at[s]).start()

    def wait_gather(s):
        _token_copy(a8_hbm, 0, xbuf.at[s], 0, gsem.at[s], MOE_TM).wait()

    def start_scatter(s):
        for r in range(MOE_TM):
            _token_copy(ybuf.at[s], r, y8_hbm, dst_ref[i * MOE_TM + r], ssem.at[s]).start()

    def wait_scatter(s):
        _token_copy(ybuf.at[s], 0, y8_hbm, 0, ssem.at[s], MOE_TM).wait()

    @pl.when(nv > 0)
    def _():
        @pl.when(i == 0)
        def _():
            start_gather(0, 0)
            ybuf[...] = jnp.zeros_like(ybuf)
            n_slots = y8_hbm.shape[0] // ROW_TILE
            fill = _token_copy(ybuf.at[0], 0, y8_hbm, n_slots - MOE_TM, ssem.at[0], MOE_TM)
            fill.start()
            fill.wait()

        wait_gather(slot)
        x = _from_token_tiles(xbuf.at[slot], MOE_TM).astype(BF16)
        start_gather(i + 1, 1 - slot)
        start_scatter(1 - slot)
        acc = jnp.zeros((MOE_TM, D), F32)
        for c in range(D_FF // FF_CHUNK):
            sl = slice(FF_CHUNK * c, FF_CHUNK * (c + 1))
            hid = _silu(jnp.dot(x, w1_ref[0, :, sl], preferred_element_type=F32)) * jnp.dot(
                x, w3_ref[0, :, sl], preferred_element_type=F32)
            acc = acc + jnp.dot(hid.astype(BF16), w2_ref[0, sl, :], preferred_element_type=F32)

        @pl.when(i > 0)
        def _():
            wait_scatter(slot)

        _to_token_tiles(ybuf.at[slot], acc)

    @pl.when((nv == 0) & (i > 0) & (nval_ref[jnp.maximum(i - 1, 0)] > 0))
    def _():
        wait_gather(slot)
        start_scatter(1 - slot)
        wait_scatter(slot)
        wait_scatter(1 - slot)


def _experts(tile_expert, n_valid, tok_of_row, dst_of_row, a8, w1, w3, w2, rows):
    n_tiles = tile_expert.shape[0]
    wspec = lambda shape: pl.BlockSpec(shape, lambda i, te, nv, tk, ds: (te[i], 0, 0),
                                       pipeline_mode=pl.Buffered(1))
    buf = pltpu.VMEM((2, MOE_TM * ROW_TILE, 128), F32)
    return pl.pallas_call(
        _expert_kernel,
        grid_spec=pltpu.PrefetchScalarGridSpec(
            num_scalar_prefetch=4,
            grid=(n_tiles,),
            in_specs=[pl.BlockSpec(memory_space=pl.ANY),
                      wspec((1, D, D_FF)), wspec((1, D, D_FF)), wspec((1, D_FF, D))],
            out_specs=pl.BlockSpec(memory_space=pl.ANY),
            scratch_shapes=[buf, buf, pltpu.SemaphoreType.DMA((2,)), pltpu.SemaphoreType.DMA((2,))]),
        out_shape=jax.ShapeDtypeStruct(((2 * rows + 2 * MOE_TM) * ROW_TILE, 128), F32),
        compiler_params=_cparams(("arbitrary",)),
        name="moe_experts",
    )(tile_expert, n_valid, tok_of_row, dst_of_row, a8, w1, w3, w2)


def _combine_kernel(h_ref, mod_ref, gate_ref, y0_ref, y1_ref, gf_ref, o_ref):
    tm = h_ref.shape[1]
    gates = gate_ref[...]
    y = gates[:, 0:1] * _from_token_tiles(y0_ref, tm) + gates[:, 1:2] * _from_token_tiles(y1_ref, tm)
    out = h_ref[0] + mod_ref[0, :, 5 * D:6 * D] * y
    out = out * lax.rsqrt(jnp.mean(out * out, axis=-1, keepdims=True) + NORM_EPS)
    o_ref[0] = out * gf_ref[...]


def _combine(h3, mods_l, gates, y8, gf):
    nb, seq, _ = h3.shape
    tm = 512
    ni = seq // tm
    n_tok_tiles = nb * ni
    return pl.pallas_call(
        _combine_kernel,
        grid=(nb, ni),
        in_specs=[pl.BlockSpec((1, tm, D), lambda b, i: (b, i, 0)),
                  pl.BlockSpec((1, 1, 6 * D), lambda b, i: (b, 0, 0)),
                  pl.BlockSpec((tm, 128), lambda b, i: (b * ni + i, 0)),
                  pl.BlockSpec((tm * ROW_TILE, 128), lambda b, i: (b * ni + i, 0)),
                  pl.BlockSpec((tm * ROW_TILE, 128), lambda b, i: (n_tok_tiles + b * ni + i, 0)),
                  pl.BlockSpec((1, D), lambda b, i: (0, 0))],
        out_specs=pl.BlockSpec((1, tm, D), lambda b, i: (b, i, 0)),
        out_shape=jax.ShapeDtypeStruct((nb, seq, D), F32),
        compiler_params=_cparams(("arbitrary", "arbitrary")),
        name="moe_combine_norm",
    )(h3, mods_l, gates, y8, y8, gf)


def _moe(h3, mods_l, g, rw, rb, w1, w3, w2, gf):
    nb, seq, _ = h3.shape
    rows = nb * seq
    a8, idx, gates = _router(h3, mods_l, g, rw, rb)
    tile_expert, n_valid, tok_of_row, dst_of_row = _route_plan(idx, rows)
    y8 = _experts(tile_expert, n_valid, tok_of_row, dst_of_row, a8, w1, w3, w2, rows)
    return _combine(h3, mods_l, gates, y8, gf)


def _pad_cols(a, n):
    return jnp.pad(a, [(0, 0)] * (a.ndim - 1) + [(0, n - a.shape[-1])])


def _mix_in_layout(w_in, rw_mu, conv_w, conv_b):
    w_rw = _pad_cols(w_in[:, :RW_COLS], RW_PAD)
    mb = w_in[:, RW_COLS:]
    w_mb = _pad_cols(jnp.concatenate([mb[:, MB_DIM:MB_DIM + MB_XBC], mb[:, :MB_DIM], mb[:, MB_DIM + MB_XBC:]],
                                     axis=1), MB_PAD)
    w = jnp.concatenate([w_rw, w_mb], axis=1).astype(BF16)
    mu_p = _pad_cols(rw_mu[0], RW_PAD)
    mu_n = _pad_cols(rw_mu[1], RW_PAD)
    zeros = jnp.zeros((RW_PAD,), F32)
    cf_rw = jnp.stack([zeros, mu_p, 1.0 - mu_p - mu_n, mu_n, zeros, zeros, zeros, zeros])
    ident = jnp.zeros((8, MB_PAD - MB_XBC), F32).at[2].set(1.0)
    cf_xbc = jnp.concatenate([conv_w, conv_b[None], jnp.zeros((2, MB_XBC), F32)], axis=0)
    cf = jnp.concatenate([cf_rw, cf_xbc, ident], axis=1)
    return w, cf


def _lora_pad(up, offset):
    out = jnp.zeros((2, 128, up.shape[-1]), F32)
    for d in range(2):
        out = out.at[d, offset + RW_LORA * d:offset + RW_LORA * (d + 1)].set(up[d])
    return out


def kernel(x, c, ctx, c_ctx, w_ada, b_ada, norm_mix, norm_ffn, norm_final, mix_w_in, mix_w_out, rw_mu, rw_w0,
           rw_w_up, rw_a0, rw_a_up, rw_g_up, rw_k_k, rw_k_a, rw_r_k, rw_gn_w, rw_gn_b, mb_conv_w, mb_conv_b,
           mb_dt_bias, mb_a_log, mb_d, mb_norm_w, ffn_w1, ffn_w3, ffn_w2, na_w_qkv, na_w_out, na_rpb,
           moe_router_w, moe_router_b, moe_w1, moe_w3, moe_w2):
    nb, seq, _ = x.shape
    ctx_len = ctx.shape[1]
    t = seq + ctx_len
    nt, nl = t // TM, seq // TM
    ntb = t // TMB
    n_lat_last = seq - TMB * (ntb - 1)
    assert seq % TM == 0 and ctx_len % TM == 0 and seq % ctx_len == 0 and nb < 16
    assert t % TMB == 0 and 0 < n_lat_last <= TMB

    cvec = jnp.zeros((16, D), F32).at[:nb].set(c).at[nb].set(c_ctx)
    mods = _ada(cvec, w_ada, b_ada).reshape(w_ada.shape[0], 16, 1, 6 * D)

    h = jnp.concatenate([x, ctx], axis=1).reshape(nb * t, D)

    w_in, cf = _mix_in_layout(mix_w_in[0], rw_mu[0], mb_conv_w[0], mb_conv_b[0])
    z = _inproj(h, mods[0], norm_mix[0][None], w_in, cf, nb, nt, nl, RW_PAD // TN, (RW_PAD + MB_XBC) // TN)
    z3 = z.reshape(nb, t, MIX_PAD)
    wlw = _lora_pad(rw_w_up[0], 0)
    wla = _lora_pad(rw_a_up[0], 2 * RW_LORA)
    w0 = rw_w0[0][:, None, :]
    a0 = rw_a0[0][:, None, :]
    yrw = _rwkv_scan(z3, w0, wlw, a0, wla, rw_k_k[0][None], rw_k_a[0][None], seq)
    dtb = _pad_cols(mb_dt_bias[0].reshape(1, 2 * MB_HEADS), 128)
    alog = _pad_cols(mb_a_log[0].reshape(1, 2 * MB_HEADS), 128)
    ymb = _ssd_scan(z3, dtb, alog, seq)
    gup = jnp.pad(rw_g_up[0], ((0, 128 - RW_LORA_G), (0, 0))).astype(BF16)
    yrw = [y.reshape(nb * t, RW_DIM) for y in yrw]
    ymb = [y.reshape(nb * t, MB_DIM) for y in ymb]
    h = _finish(z, yrw, ymb, h, mods[0], a0, wla,
                rw_k_a[0][None], rw_r_k[0].reshape(2, 1, RW_DIM), gup, rw_gn_w[0][None], rw_gn_b[0][None],
                jnp.repeat(mb_d[0], HEAD)[None], mb_norm_w[0][None], mix_w_out[0].astype(BF16), nb, nt, nl)
    h = _ffn(h, mods[0], norm_ffn[0][None], ffn_w1[0].astype(BF16), ffn_w3[0].astype(BF16),
             ffn_w2[0].astype(BF16), nb, ntb, n_lat_last)

    qkv = _nmm(h, mods[1], norm_mix[1][None], na_w_qkv[0].astype(BF16), nb, ntb, n_lat_last, BF16)
    o = _attention(qkv.reshape(nb, t, 3 * D), _attn_bias_table(na_rpb[0]), seq, ctx_len)
    mods1 = mods[1].reshape(16, 1, 6 * D)
    h3 = _proj_res(o, h.reshape(nb, t, D), mods1, na_w_out[0].astype(BF16), seq)
    rw = _pad_cols(moe_router_w[0], 128)
    rb = _pad_cols(moe_router_b[0][None], 128)
    return _moe(h3, mods1, norm_ffn[1][None], rw, rb, moe_w1[0].astype(BF16), moe_w3[0].astype(BF16),
                moe_w2[0].astype(BF16), norm_final[None])
```

```python
import functools
import math

import numpy as np
import jax
import jax.numpy as jnp
from jax import lax
from jax.experimental import pallas as pl
from jax.experimental.pallas import tpu as pltpu

F32 = jnp.float32
BF16 = jnp.bfloat16

D = 1024
NORM_EPS = 1e-6
GRID_W = 64

HEAD = 64
RW_DIM = 512
RW_HEADS = 8
RW_LORA = 32
RW_LORA_G = 96
RW_GN_EPS = 64e-5
RW_COLS = 3 * RW_DIM + 4 * RW_LORA + RW_LORA_G
RW_PAD = 1792
RW_CHUNK = 64
RW_SEQ_PER_STEP = 2

MB_DIM = 512
MB_HEADS = 8
MB_GROUPS = 2
MB_STATE = 128
MB_CONV = 5
MB_XBC = MB_DIM + 2 * MB_GROUPS * MB_STATE
MB_PAD = 1792
MB_CHUNK = 128

MIX_PAD = RW_PAD + MB_PAD

NA_HEADS = 16
NA_KH = 8
NA_KW = 16

D_FF = 2816
N_EXPERTS = 8

TM = 256
TMB = 768
HALO = 16
TN = 256
TF = 256
VMEM_LIMIT = 56 * 1024 * 1024


def _cparams(sem):
    return pltpu.CompilerParams(dimension_semantics=sem, vmem_limit_bytes=VMEM_LIMIT)


def _bdot(a, b):
    return jnp.dot(a.astype(BF16), b.astype(BF16), preferred_element_type=F32)


def _bdot_nt(a, b):
    return lax.dot_general(a.astype(BF16), b.astype(BF16), (((1,), (1,)), ((), ())),
                           preferred_element_type=F32)


def _hdot(a, b):
    return jnp.dot(a, b, precision=lax.Precision.HIGHEST, preferred_element_type=F32)


def _split3(x):
    p1 = x.astype(BF16)
    r1 = x - p1.astype(F32)
    p2 = r1.astype(BF16)
    return p1, p2, (r1 - p2.astype(F32)).astype(BF16)


def _dot3(a, b):
    ah, al, _ = _split3(a)
    bh, bl, _ = _split3(b)
    dot = functools.partial(jnp.dot, preferred_element_type=F32)
    return dot(ah, bh) + (dot(ah, bl) + dot(al, bh))


def _tri_cumsum(mask, x):
    tri = jnp.where(mask, 1.0, 0.0).astype(BF16)
    p1, p2, p3 = _split3(x)
    dot = functools.partial(jnp.dot, preferred_element_type=F32)
    return dot(tri, p1) + (dot(tri, p2) + dot(tri, p3))


def _bmm(spec, a, b):
    return jnp.einsum(spec, a.astype(BF16), b.astype(BF16), preferred_element_type=F32)


def _sigmoid(x):
    return 1.0 / (1.0 + jnp.exp(-x))


def _silu(x):
    return x * _sigmoid(x)


def _softplus(x):
    return jnp.maximum(x, 0.0) + jnp.log(1.0 + jnp.exp(-jnp.abs(x)))


def _norm_mod(h, g, shift, scale):
    hn = h * lax.rsqrt(jnp.mean(h * h, axis=-1, keepdims=True) + NORM_EPS)
    return (hn * g) * (1.0 + scale) + shift


def _seg64_sum(x):
    outs = []
    for p in range(x.shape[-1] // 128):
        xp = x[:, 128 * p:128 * (p + 1)]
        lo = lax.broadcasted_iota(jnp.int32, xp.shape, 1) < HEAD
        s_lo = jnp.sum(jnp.where(lo, xp, 0.0), axis=-1, keepdims=True)
        s_hi = jnp.sum(jnp.where(lo, 0.0, xp), axis=-1, keepdims=True)
        outs.append(jnp.where(lo, s_lo, s_hi))
    return jnp.concatenate(outs, axis=-1)


def _heads(x):
    return jnp.stack([x[:, HEAD * h:HEAD * (h + 1)] for h in range(x.shape[-1] // HEAD)], axis=0)


def _unheads(x):
    return jnp.concatenate([x[h] for h in range(x.shape[0])], axis=-1)


def _ada_kernel(c_ref, w_ref, b_ref, o_ref):
    o_ref[0] = _hdot(_silu(c_ref[...]), w_ref[0]) + b_ref[0]


def _ada(cvec, w_ada, b_ada):
    depth = w_ada.shape[0]
    tn = 1536
    return pl.pallas_call(
        _ada_kernel,
        grid=(depth, 6 * D // tn),
        in_specs=[pl.BlockSpec((16, D), lambda l, j: (0, 0)),
                  pl.BlockSpec((1, D, tn), lambda l, j: (l, 0, j)),
                  pl.BlockSpec((1, 1, tn), lambda l, j: (l, 0, j))],
        out_specs=pl.BlockSpec((1, 16, tn), lambda l, j: (l, 0, j)),
        out_shape=jax.ShapeDtypeStruct((depth, 16, 6 * D), F32),
        compiler_params=_cparams(("arbitrary", "arbitrary")),
        name="adaln",
    )(cvec, w_ada, b_ada.reshape(depth, 1, 6 * D))


def _mod_row(i, nt, nl, nb):
    return jnp.where(i % nt >= nl, nb, i // nt)


def _resident(shape):
    return pl.BlockSpec(shape, lambda *_: (0,) * len(shape), pipeline_mode=pl.Buffered(1))


def _inproj_kernel(hp_ref, h_ref, hn_ref, mod_ref, g_ref, w_ref, cf_ref, o_ref, a_scr, z_scr,
                   *, nt, nl, silu_lo, silu_hi):
    t = pl.program_id(0) % nt
    first = (t == 0) | (t == nl)
    last = (t == nl - 1) | (t == nt - 1)
    shift = mod_ref[0, :, 0:D]
    scale = mod_ref[0, :, D:2 * D]
    g = g_ref[...]
    a_scr[0:HALO] = jnp.where(first, 0.0, _norm_mod(hp_ref[...], g, shift, scale)).astype(BF16)
    a_scr[HALO:HALO + TM] = _norm_mod(h_ref[...], g, shift, scale).astype(BF16)
    a_scr[HALO + TM:] = jnp.where(last, 0.0, _norm_mod(hn_ref[...], g, shift, scale)).astype(BF16)
    a = a_scr[...]
    for j in range(w_ref.shape[1] // TN):
        sl = slice(TN * j, TN * (j + 1))
        z = z_scr.at[j % 2]
        z[...] = jnp.dot(a, w_ref[:, sl], preferred_element_type=F32)

        def tap(k):
            return cf_ref[k:k + 1, sl] * z[HALO - 2 + k:HALO - 2 + k + TM, :]

        if j < silu_lo:
            o_ref[:, sl] = tap(1) + tap(2) + tap(3)
        elif j < silu_hi:
            o_ref[:, sl] = _silu(cf_ref[5:6, sl] + tap(0) + tap(1) + tap(2) + tap(3) + tap(4))
        else:
            o_ref[:, sl] = z[HALO:HALO + TM, :]


def _inproj(h2, mods_l, g, w, cf, nb, nt, nl, silu_lo, silu_hi):
    rows = h2.shape[0]
    n = w.shape[1]
    hb = TM // HALO
    return pl.pallas_call(
        functools.partial(_inproj_kernel, nt=nt, nl=nl, silu_lo=silu_lo, silu_hi=silu_hi),
        grid=(rows // TM,),
        in_specs=[pl.BlockSpec((HALO, D), lambda i: (jnp.maximum(i * hb - 1, 0), 0)),
                  pl.BlockSpec((TM, D), lambda i: (i, 0)),
                  pl.BlockSpec((HALO, D), lambda i: (jnp.minimum((i + 1) * hb, rows // HALO - 1), 0)),
                  pl.BlockSpec((1, 1, 6 * D), lambda i: (_mod_row(i, nt, nl, nb), 0, 0)),
                  _resident((1, D)), _resident((D, n)), _resident((8, n))],
        out_specs=pl.BlockSpec((TM, n), lambda i: (i, 0)),
        out_shape=jax.ShapeDtypeStruct((rows, n), F32),
        scratch_shapes=[pltpu.VMEM((TM + 2 * HALO, D), BF16), pltpu.VMEM((2, TM + 2 * HALO, TN), F32)],
        compiler_params=_cparams(("arbitrary",)),
        name="mix_inproj",
    )(h2, h2, h2, mods_l, g, w, cf)


def _ctx_rows(ntb, n_lat_last, tm):
    row = lax.broadcasted_iota(jnp.int32, (tm, 1), 0)
    return (pl.program_id(0) % ntb == ntb - 1) & (row >= n_lat_last)


def _row_mod(mod_ref, modc_ref, lo, is_ctx):
    return jnp.where(is_ctx, modc_ref[0, :, lo:lo + D], mod_ref[0, :, lo:lo + D])


def _big_tile_specs(nb, ntb):
    return [pl.BlockSpec((TMB, D), lambda i: (i, 0)),
            pl.BlockSpec((1, 1, 6 * D), lambda i: (i // ntb, 0, 0)),
            pl.BlockSpec((1, 1, 6 * D), lambda i: (nb, 0, 0)),
            _resident((1, D))]


def _nmm_kernel(h_ref, mod_ref, modc_ref, g_ref, w_ref, o_ref, *, ntb, n_lat_last):
    is_ctx = _ctx_rows(ntb, n_lat_last, TMB)
    a = _norm_mod(h_ref[...], g_ref[...], _row_mod(mod_ref, modc_ref, 0, is_ctx),
                  _row_mod(mod_ref, modc_ref, D, is_ctx)).astype(BF16)
    tn = 2 * TN
    for j in range(w_ref.shape[1] // tn):
        sl = slice(tn * j, tn * (j + 1))
        o_ref[:, sl] = jnp.dot(a, w_ref[:, sl], preferred_element_type=F32).astype(o_ref.dtype)


def _nmm(h2, mods_l, g, w, nb, ntb, n_lat_last, out_dtype):
    rows = h2.shape[0]
    n = w.shape[1]
    return pl.pallas_call(
        functools.partial(_nmm_kernel, ntb=ntb, n_lat_last=n_lat_last),
        grid=(rows // TMB,),
        in_specs=_big_tile_specs(nb, ntb) + [_resident((D, n))],
        out_specs=pl.BlockSpec((TMB, n), lambda i: (i, 0)),
        out_shape=jax.ShapeDtypeStruct((rows, n), out_dtype),
        compiler_params=_cparams(("arbitrary",)),
        name="norm_proj",
    )(h2, mods_l, mods_l, g, w)


def _rwkv_kernel(zf_ref, zb_ref, w0_ref, wlw_ref, a0_ref, wla_ref, kk_ref, ka_ref, yf_ref, yb_ref, s_scr):
    @pl.when(pl.program_id(1) == 0)
    def _():
        s_scr[...] = jnp.zeros_like(s_scr)

    n_seq = zf_ref.shape[0]
    prep = [_rwkv_prep(z_ref[q], w0_ref[d], wlw_ref[d], a0_ref[d], wla_ref[d], kk_ref[...], ka_ref[...],
                       backward=d == 1) for d, z_ref in enumerate((zf_ref, zb_ref)) for q in range(n_seq)]
    y, s_new = _rwkv_chain(*[jnp.concatenate(parts, axis=0) for parts in zip(*prep)], s_scr[...])
    for d, y_ref in enumerate((yf_ref, yb_ref)):
        for q in range(n_seq):
            lo = (d * n_seq + q) * RW_HEADS
            y_ref[q] = _unheads(y[lo:lo + RW_HEADS])
    s_scr[...] = s_new


def _rwkv_prep(z, w0, wlw, a0, wla, kkw, kaw, backward):
    L = RW_CHUNK
    r = z[:, 0:RW_DIM]
    k = z[:, RW_DIM:2 * RW_DIM]
    v = z[:, 2 * RW_DIM:3 * RW_DIM]
    lora = z[:, 3 * RW_DIM:3 * RW_DIM + 128]
    logw = w0 + _dot3(jnp.tanh(lora), wlw)
    logdec = -math.exp(-0.5) * _sigmoid(logw)
    iclr = _sigmoid(a0 + _dot3(lora, wla))
    kk = k * kkw
    kk = kk / jnp.maximum(jnp.sqrt(_seg64_sum(kk * kk)), 1e-12)
    kdir = k * (1.0 + (iclr - 1.0) * kaw)
    bvec = kk * iclr

    row = lax.broadcasted_iota(jnp.int32, (L, L), 0)
    col = lax.broadcasted_iota(jnp.int32, (L, L), 1)
    ahead = col - row if backward else row - col
    incl = ahead >= 0
    strict = ahead > 0
    lc = _tri_cumsum(incl, logdec)
    ltot = jnp.sum(logdec, axis=0, keepdims=True)
    g_in = jnp.exp(lc)
    g_ex = jnp.exp(lc - logdec)
    g_inv = jnp.exp(-lc)
    g_tail = jnp.exp(ltot - lc)

    ar = _heads(jnp.concatenate([-kk * g_ex, r * g_in], axis=0))
    bk = _heads(jnp.concatenate([bvec * g_inv, kdir * g_inv], axis=0))
    bk_tail = _heads(jnp.concatenate([bvec * g_tail, kdir * g_tail], axis=0))
    vh = _heads(v)
    g_tot = _heads(jnp.exp(ltot))
    tri = lambda m: jnp.broadcast_to(jnp.where(m, 1.0, 0.0)[None], (RW_HEADS, L, L))
    return ar, bk, bk_tail, vh, g_tot, tri(incl), tri(strict)


def _rwkv_chain(ar, bk, bk_tail, vh, g_tot, incl, strict, s0):
    L = RW_CHUNK
    m1 = _bmm('hlk,hsk->hls', ar, bk)
    keep = lambda mask, blk: jnp.where(mask > 0.5, blk, 0.0)
    nmat = keep(strict, m1[:, :L, :L])
    a_ak = keep(strict, m1[:, :L, L:])
    m_r = jnp.concatenate([keep(incl, m1[:, L:, :L]), keep(incl, m1[:, L:, L:])], axis=2)
    m2 = _bmm('hlk,hvk->hlv', ar, s0)
    x = m2[:, :L] + _bmm('hls,hsv->hlv', a_ak, vh)
    p = nmat
    steps = int(math.log2(L))
    for i in range(steps):
        x = x + _bmm('hls,hsv->hlv', p, x)
        if i < steps - 1:
            p = _bmm('hls,hst->hlt', p, p)
    uv = jnp.concatenate([x, vh], axis=1)
    y = m2[:, L:] + _bmm('hls,hsv->hlv', m_r, uv)
    return y, s0 * g_tot + _bmm('hvl,hlk->hvk', jnp.swapaxes(uv, 1, 2), bk_tail)


def _scan_chunks(nc, ncl, col_block=0):
    return (lambda b, c: (b, (c + ncl) % nc, col_block)), (lambda b, c: (b, nc - 1 - c, col_block))


def _rwkv_scan(z3, w0, wlw, a0, wla, kkw, kaw, seq):
    nb, t, _ = z3.shape
    nc = t // RW_CHUNK
    fwd, bwd = _scan_chunks(nc, seq // RW_CHUNK)
    y_shape = jax.ShapeDtypeStruct((nb, t, RW_DIM), F32)
    n_seq = RW_SEQ_PER_STEP if nb % RW_SEQ_PER_STEP == 0 else 1
    return pl.pallas_call(
        _rwkv_kernel,
        grid=(nb // n_seq, nc),
        in_specs=[pl.BlockSpec((n_seq, RW_CHUNK, RW_PAD), fwd), pl.BlockSpec((n_seq, RW_CHUNK, RW_PAD), bwd),
                  _resident((2, 1, RW_DIM)), _resident((2, 128, RW_DIM)), _resident((2, 1, RW_DIM)),
                  _resident((2, 128, RW_DIM)), _resident((1, RW_DIM)), _resident((1, RW_DIM))],
        out_specs=[pl.BlockSpec((n_seq, RW_CHUNK, RW_DIM), fwd), pl.BlockSpec((n_seq, RW_CHUNK, RW_DIM), bwd)],
        out_shape=[y_shape, y_shape],
        scratch_shapes=[pltpu.VMEM((2 * n_seq * RW_HEADS, HEAD, HEAD), F32)],
        compiler_params=_cparams(("arbitrary", "arbitrary")),
        name="rwkv7_scan",
    )(z3, z3, w0, wlw, a0, wla, kkw, kaw)


def _ssd_kernel(zf_ref, zb_ref, dtb_ref, alog_ref, sel_ref, yf_ref, yb_ref, s_scr):
    @pl.when(pl.program_id(1) == 0)
    def _():
        s_scr[...] = jnp.zeros_like(s_scr)

    res = [_ssd_chunk(z_ref[0], dtb_ref[...], alog_ref[...], sel_ref[d], s_scr[d], d)
           for d, z_ref in enumerate((zf_ref, zb_ref))]
    for d, y_ref in enumerate((yf_ref, yb_ref)):
        y_ref[0] = res[d][0]
        s_scr[d] = res[d][1]


def _lane_bcast(x, sel):
    p1, p2, p3 = _split3(x)
    dot = functools.partial(jnp.dot, preferred_element_type=F32)
    return dot(p1, sel) + (dot(p2, sel) + dot(p3, sel))


def _ssd_chunk(z, dtb, alog, sel, s_all, d):
    L = MB_CHUNK
    xm = z[:, 0:MB_DIM]
    dt_all = _softplus(z[:, 1536:1664] + dtb)
    a_all = dt_all * (-jnp.exp(alog))
    row = lax.broadcasted_iota(jnp.int32, (L, L), 0)
    col = lax.broadcasted_iota(jnp.int32, (L, L), 1)
    incl = (col >= row) if d == 1 else (row >= col)
    cs = _tri_cumsum(incl, a_all)
    cs_t = cs.T
    tot = jnp.sum(a_all, axis=0, keepdims=True)
    cs_b = _lane_bcast(cs, sel)
    dt_b = _lane_bcast(dt_all, sel)
    gmat = []
    for g in range(MB_GROUPS):
        bg = z[:, 512 + 128 * g:640 + 128 * g]
        cg = z[:, 768 + 128 * g:896 + 128 * g]
        gmat.append((bg, cg, _bdot_nt(cg, bg)))
    outs, states = [], []
    for h in range(MB_HEADS):
        bg, cg, cb = gmat[h // (MB_HEADS // MB_GROUPS)]
        j = MB_HEADS * d + h
        cs_h = cs_b[:, 128 * h:128 * (h + 1)]
        cs_col = cs_h[:, :HEAD]
        tot_h = tot[:, j:j + 1]
        lmat = jnp.exp(jnp.where(incl, cs_h - cs_t[j:j + 1, :], -jnp.inf))
        xh = xm[:, HEAD * h:HEAD * (h + 1)] * dt_b[:, 128 * h:128 * h + HEAD]
        s0 = s_all[h]
        outs.append(_bdot(cb * lmat, xh) + jnp.exp(cs_col) * _bdot_nt(cg, s0))
        xd = xh * jnp.exp(tot_h - cs_col)
        states.append(s0 * jnp.exp(tot_h) + _bdot(xd.T, bg))
    return jnp.concatenate(outs, axis=-1), jnp.stack(states, axis=0)


def _ssd_scan(z3, dtb, alog, seq):
    nb, t, _ = z3.shape
    nc = t // MB_CHUNK
    fwd, bwd = _scan_chunks(nc, seq // MB_CHUNK, col_block=1)
    yf, yb = _scan_chunks(nc, seq // MB_CHUNK)
    y_shape = jax.ShapeDtypeStruct((nb, t, MB_DIM), F32)
    return pl.pallas_call(
        _ssd_kernel,
        grid=(nb, nc),
        in_specs=[pl.BlockSpec((1, MB_CHUNK, MB_PAD), fwd), pl.BlockSpec((1, MB_CHUNK, MB_PAD), bwd),
                  _resident((1, 128)), _resident((1, 128)), _resident((2, 128, MB_HEADS * 128))],
        out_specs=[pl.BlockSpec((1, MB_CHUNK, MB_DIM), yf), pl.BlockSpec((1, MB_CHUNK, MB_DIM), yb)],
        out_shape=[y_shape, y_shape],
        scratch_shapes=[pltpu.VMEM((2, MB_HEADS, HEAD, MB_STATE), F32)],
        compiler_params=_cparams(("arbitrary", "arbitrary")),
        name="ssd_scan",
    )(z3, z3, dtb, alog, jnp.asarray(_head_selector(), BF16))


def _head_selector():
    sel = np.zeros((2, 128, MB_HEADS * 128), np.float32)
    for d in range(2):
        for h in range(MB_HEADS):
            sel[d, MB_HEADS * d + h, 128 * h:128 * (h + 1)] = 1.0
    return sel


def _finish_kernel(zr_ref, zm_ref, yrf_ref, yrb_ref, ymf_ref, ymb_ref, h_ref, mod_ref, a0_ref, wla_ref, ka_ref,
                   rk_ref, gup_ref,
                   gnw_ref, gnb_ref, mbd_ref, mbn_ref, wout_ref, o_ref):
    zr = zr_ref[...]
    r = zr[:, 0:RW_DIM]
    k = zr[:, RW_DIM:2 * RW_DIM]
    v = zr[:, 2 * RW_DIM:3 * RW_DIM]
    lora = zr[:, 3 * RW_DIM:3 * RW_DIM + 128]
    gate = _bdot(_sigmoid(zr[:, 3 * RW_DIM + 128:3 * RW_DIM + 256]), gup_ref[...])
    bonus = jnp.zeros_like(r)
    for dd in range(2):
        iclr = _sigmoid(a0_ref[dd] + _dot3(lora, wla_ref[dd]))
        bonus = bonus + r * (k * (1.0 + (iclr - 1.0) * ka_ref[...])) * rk_ref[dd]
    bonus = _seg64_sum(bonus) * v
    y = yrf_ref[...] + yrb_ref[...]
    mu = _seg64_sum(y) * (1.0 / HEAD)
    yc = y - mu
    var = _seg64_sum(yc * yc) * (1.0 / HEAD)
    y = yc * lax.rsqrt(var + RW_GN_EPS) * gnw_ref[...] + gnb_ref[...]
    o_rw = (y + bonus) * gate

    zm = zm_ref[...]
    xm = zm[:, 0:MB_DIM]
    zg = zm[:, 1024:1536]
    ym = (ymf_ref[...] + ymb_ref[...] + mbd_ref[...] * xm) * _silu(zg)
    gw = MB_DIM // MB_GROUPS
    parts = []
    for g in range(MB_GROUPS):
        yg = ym[:, gw * g:gw * (g + 1)]
        parts.append(yg * lax.rsqrt(jnp.mean(yg * yg, axis=-1, keepdims=True) + NORM_EPS))
    o_mb = jnp.concatenate(parts, axis=-1) * mbn_ref[...]
    o = jnp.concatenate([o_rw, o_mb], axis=-1)
    o_ref[...] = h_ref[...] + mod_ref[0, :, 2 * D:3 * D] * _bdot(o, wout_ref[...])


def _finish(z2, yrw, ymb, h2, mods_l, a0, wla, kaw, rk, gup, gnw, gnb, mbd, mbn, wout, nb, nt, nl):
    rows = h2.shape[0]
    full = lambda *shape: _resident(shape)
    half = pl.BlockSpec((TM, RW_DIM), lambda i: (i, 0))
    return pl.pallas_call(
        _finish_kernel,
        grid=(rows // TM,),
        in_specs=[pl.BlockSpec((TM, RW_PAD), lambda i: (i, 0)),
                  pl.BlockSpec((TM, MB_PAD), lambda i: (i, 1)),
                  half, half, half, half,
                  pl.BlockSpec((TM, D), lambda i: (i, 0)),
                  pl.BlockSpec((1, 1, 6 * D), lambda i: (_mod_row(i, nt, nl, nb), 0, 0)),
                  full(2, 1, RW_DIM), full(2, 128, RW_DIM), full(1, RW_DIM), full(2, 1, RW_DIM),
                  full(128, RW_DIM), full(1, RW_DIM), full(1, RW_DIM), full(1, MB_DIM), full(1, MB_DIM),
                  full(D, D)],
        out_specs=pl.BlockSpec((TM, D), lambda i: (i, 0)),
        out_shape=jax.ShapeDtypeStruct((rows, D), F32),
        compiler_params=_cparams(("arbitrary",)),
        name="mix_finish",
    )(z2, z2, yrw[0], yrw[1], ymb[0], ymb[1], h2, mods_l, a0, wla, kaw, rk, gup, gnw, gnb, mbd, mbn, wout)


def _ffn_kernel(h_ref, mod_ref, modc_ref, g_ref, w1_ref, w3_ref, w2_ref, o_ref, *, ntb, n_lat_last):
    is_ctx = _ctx_rows(ntb, n_lat_last, TMB)
    h = h_ref[...]
    a = _norm_mod(h, g_ref[...], _row_mod(mod_ref, modc_ref, 3 * D, is_ctx),
                  _row_mod(mod_ref, modc_ref, 4 * D, is_ctx)).astype(BF16)
    acc = jnp.zeros((TMB, D), F32)
    for c in range(D_FF // TF):
        sl = slice(TF * c, TF * (c + 1))
        hid = _silu(jnp.dot(a, w1_ref[:, sl], preferred_element_type=F32)) * jnp.dot(
            a, w3_ref[:, sl], preferred_element_type=F32)
        acc = acc + jnp.dot(hid.astype(BF16), w2_ref[sl, :], preferred_element_type=F32)
    o_ref[...] = h + _row_mod(mod_ref, modc_ref, 5 * D, is_ctx) * acc


def _ffn(h2, mods_l, g, w1, w3, w2, nb, ntb, n_lat_last):
    rows = h2.shape[0]
    return pl.pallas_call(
        functools.partial(_ffn_kernel, ntb=ntb, n_lat_last=n_lat_last),
        grid=(rows // TMB,),
        in_specs=_big_tile_specs(nb, ntb) + [_resident((D, D_FF)), _resident((D, D_FF)), _resident((D_FF, D))],
        out_specs=pl.BlockSpec((TMB, D), lambda i: (i, 0)),
        out_shape=jax.ShapeDtypeStruct((rows, D), F32),
        compiler_params=_cparams(("arbitrary",)),
        name="ffn_swiglu",
    )(h2, mods_l, mods_l, g, w1, w3, w2)


def _attn_kernel(q_ref, k_ref, v_ref, kc_ref, vc_ref, bias_ref, o_ref, *, n_rows):
    r = pl.program_id(1)
    rs = jnp.clip(r - NA_KH // 2, 0, n_rows - NA_KH)
    start = pl.multiple_of(rs * GRID_W, GRID_W)
    nk = NA_KH * GRID_W
    nc = kc_ref.shape[1]
    low = lax.broadcasted_iota(jnp.int32, (GRID_W, 128), 1) < HEAD
    for pair in range(NA_HEADS // 2):
        sl = slice(128 * pair, 128 * (pair + 1))
        q = q_ref[0, :, sl] * (HEAD ** -0.5)
        kp = k_ref[0, pl.ds(start, nk), sl]
        kcp = kc_ref[0, :, sl]
        v1 = jnp.concatenate([v_ref[0, pl.ds(start, nk), sl], jnp.ones((nk, 128), BF16)], axis=-1)
        vc1 = jnp.concatenate([vc_ref[0, :, sl], jnp.ones((nc, 128), BF16)], axis=-1)
        halves = []
        for half in range(2):
            qh = jnp.where(low if half == 0 else jnp.logical_not(low), q, jnp.zeros_like(q))
            s_lat = _bdot_nt(qh, kp) + bias_ref[0, 2 * pair + half]
            s_ctx = _bdot_nt(qh, kcp)
            m = s_ctx
            for j in range(nk // nc):
                m = jnp.maximum(m, s_lat[:, nc * j:nc * (j + 1)])
            m = jnp.max(m, axis=-1, keepdims=True)
            ov = _bdot(jnp.exp(s_lat - m), v1) + _bdot(jnp.exp(s_ctx - m), vc1)
            halves.append(ov[:, :128] / ov[:, 128:])
        o_ref[0, :, sl] = jnp.where(low, halves[0], halves[1]).astype(o_ref.dtype)


def _attention(qkv, bias, seq, ctx_len):
    nb = qkv.shape[0]
    n_rows = seq // GRID_W
    nk = NA_KH * GRID_W

    def cfg(b, r):
        return r - jnp.clip(r - NA_KH // 2, 0, n_rows - NA_KH)

    return pl.pallas_call(
        functools.partial(_attn_kernel, n_rows=n_rows),
        grid=(nb, n_rows),
        in_specs=[pl.BlockSpec((1, GRID_W, D), lambda b, r: (b, r, 0)),
                  pl.BlockSpec((1, seq, D), lambda b, r: (b, 0, 1)),
                  pl.BlockSpec((1, seq, D), lambda b, r: (b, 0, 2)),
                  pl.BlockSpec((1, ctx_len, D), lambda b, r: (b, seq // ctx_len, 1)),
                  pl.BlockSpec((1, ctx_len, D), lambda b, r: (b, seq // ctx_len, 2)),
                  pl.BlockSpec((1, NA_HEADS, GRID_W, nk), lambda b, r: (cfg(b, r), 0, 0, 0))],
        out_specs=pl.BlockSpec((1, GRID_W, D), lambda b, r: (b, r, 0)),
        out_shape=jax.ShapeDtypeStruct((nb, seq, D), BF16),
        compiler_params=_cparams(("arbitrary", "arbitrary")),
        name="nbr_attention",
    )(qkv, qkv, qkv, qkv, qkv, bias)


def _attn_bias_table(rpb):
    q = np.arange(GRID_W)[:, None]
    c = np.arange(GRID_W)[None, :]
    ws = np.clip(q - NA_KW // 2, 0, GRID_W - NA_KW)
    inside = (c >= ws) & (c < ws + NA_KW)
    cidx = np.clip(c - q + NA_KW - 1, 0, 2 * NA_KW - 2)
    pick = np.zeros((2 * NA_KW - 1, GRID_W * GRID_W), np.float32)
    pick[cidx.reshape(-1), np.arange(GRID_W * GRID_W)] = 1.0
    n_dr = 2 * NA_KH - 1
    cols = jnp.dot(rpb.astype(F32).reshape(NA_HEADS * n_dr, 2 * NA_KW - 1), pick,
                   precision=lax.Precision.HIGHEST).reshape(NA_HEADS, n_dr, GRID_W, GRID_W)
    cols = jnp.where(inside[None, None], cols, -jnp.inf)
    tab = jnp.stack([cols[:, NA_KH - 1 - cfg:2 * NA_KH - 1 - cfg] for cfg in range(NA_KH)])
    return tab.transpose(0, 1, 3, 2, 4).reshape(NA_KH, NA_HEADS, GRID_W, NA_KH * GRID_W)


def _proj_res_kernel(o_ref, h_ref, mod_ref, w_ref, out_ref):
    out_ref[0] = h_ref[0] + mod_ref[0, :, 2 * D:3 * D] * jnp.dot(o_ref[0], w_ref[...],
                                                                 preferred_element_type=F32)


def _proj_res(o3, h3, mods_l, w, seq):
    nb = o3.shape[0]
    tm = 512
    return pl.pallas_call(
        _proj_res_kernel,
        grid=(nb, seq // tm),
        in_specs=[pl.BlockSpec((1, tm, D), lambda b, i: (b, i, 0)),
                  pl.BlockSpec((1, tm, D), lambda b, i: (b, i, 0)),
                  pl.BlockSpec((1, 1, 6 * D), lambda b, i: (b, 0, 0)),
                  pl.BlockSpec((D, D), lambda b, i: (0, 0))],
        out_specs=pl.BlockSpec((1, tm, D), lambda b, i: (b, i, 0)),
        out_shape=jax.ShapeDtypeStruct((nb, seq, D), F32),
        compiler_params=_cparams(("arbitrary", "arbitrary")),
        name="attn_outproj",
    )(o3, h3, mods_l, w)


MOE_TM = 512
FF_CHUNK = 2816
ROW_TILE = 8


def _to_token_tiles(ref, val):
    n = val.shape[0]
    for s in range(ROW_TILE):
        ref[pl.ds(s, n, stride=ROW_TILE), :] = val[:, 128 * s:128 * (s + 1)]


def _from_token_tiles(ref, n):
    return jnp.concatenate([ref[pl.ds(s, n, stride=ROW_TILE), :] for s in range(ROW_TILE)], axis=-1)


def _router_kernel(h_ref, mod_ref, g_ref, rw_ref, rb_ref, a8_ref, idx_ref, gate_ref):
    a = _norm_mod(h_ref[0], g_ref[...], mod_ref[0, :, 3 * D:4 * D], mod_ref[0, :, 4 * D:5 * D])
    _to_token_tiles(a8_ref, a)
    logits = _hdot(a, rw_ref[...]) + rb_ref[...]
    lane = lax.broadcasted_iota(jnp.int32, logits.shape, 1)
    logits = jnp.where(lane < N_EXPERTS, logits, -jnp.inf)
    m1 = jnp.max(logits, axis=-1, keepdims=True)
    i1 = jnp.min(jnp.where(logits == m1, lane, 128), axis=-1, keepdims=True)
    rest = jnp.where(lane == i1, -jnp.inf, logits)
    m2 = jnp.max(rest, axis=-1, keepdims=True)
    i2 = jnp.min(jnp.where(rest == m2, lane, 128), axis=-1, keepdims=True)
    ex = jnp.exp(m2 - m1)
    idx_ref[...] = jnp.where(lane == 0, i1, jnp.where(lane == 1, i2, 0))
    gate_ref[...] = jnp.where(lane == 0, 1.0 / (1.0 + ex), jnp.where(lane == 1, ex / (1.0 + ex), 0.0))


def _router(h3, mods_l, g, rw, rb):
    nb, seq, _ = h3.shape
    tm = 512
    ni = seq // tm
    rows = nb * seq
    return pl.pallas_call(
        _router_kernel,
        grid=(nb, ni),
        in_specs=[pl.BlockSpec((1, tm, D), lambda b, i: (b, i, 0)),
                  pl.BlockSpec((1, 1, 6 * D), lambda b, i: (b, 0, 0)),
                  pl.BlockSpec((1, D), lambda b, i: (0, 0)),
                  pl.BlockSpec((D, 128), lambda b, i: (0, 0)),
                  pl.BlockSpec((1, 128), lambda b, i: (0, 0))],
        out_specs=[pl.BlockSpec((tm * ROW_TILE, 128), lambda b, i: (b * ni + i, 0)),
                   pl.BlockSpec((tm, 128), lambda b, i: (b * ni + i, 0)),
                   pl.BlockSpec((tm, 128), lambda b, i: (b * ni + i, 0))],
        out_shape=[jax.ShapeDtypeStruct((rows * ROW_TILE, 128), F32),
                   jax.ShapeDtypeStruct((rows, 128), jnp.int32),
                   jax.ShapeDtypeStruct((rows, 128), F32)],
        compiler_params=_cparams(("arbitrary", "arbitrary")),
        name="moe_router",
    )(h3, mods_l, g, rw, rb)


def _route_plan(idx, rows):
    n_pairs = 2 * rows
    n_tiles = n_pairs // MOE_TM + N_EXPERTS
    expert =jnp.concatenate([idx[:, 0], idx[:, 1]])
    onehot = (expert[:, None] == jnp.arange(N_EXPERTS)[None, :]).astype(jnp.int32)
    cum = jnp.cumsum(onehot, axis=0)
    counts = cum[-1]
    rank = jnp.sum(onehot * cum, axis=1) - 1
    padded = (counts + MOE_TM - 1) // MOE_TM * MOE_TM
    ends = jnp.cumsum(padded)
    starts = ends - padded
    pos = starts[expert] + rank
    pair = jnp.arange(n_pairs, dtype=jnp.int32)
    real_dst = jnp.full((n_tiles * MOE_TM,), -1, jnp.int32).at[pos].set(pair)
    tok_of_row = jnp.where(real_dst >= 0, real_dst % rows, 0)
    row = jnp.arange(n_tiles * MOE_TM, dtype=jnp.int32)
    trash = n_pairs + (row // MOE_TM % 2) * MOE_TM + row % MOE_TM
    behind = jnp.concatenate([jnp.full((MOE_TM,), -1, jnp.int32), real_dst[:-MOE_TM]])
    dst_of_row = jnp.where(behind >= 0, behind, trash)
    tile_start = jnp.arange(n_tiles, dtype=jnp.int32) * MOE_TM
    tile_expert = jnp.minimum(jnp.sum((tile_start[:, None] >= ends[None, :]).astype(jnp.int32), axis=1),
                              N_EXPERTS - 1)
    n_valid = jnp.clip(starts[tile_expert] + counts[tile_expert] - tile_start, 0, MOE_TM)
    n_valid = jnp.where(tile_start < ends[-1], n_valid, 0)
    return tile_expert.astype(jnp.int32), n_valid.astype(jnp.int32), tok_of_row, dst_of_row


def _token_copy(src, src_tok, dst, dst_tok, sem, n_tok=1):
    n = n_tok * ROW_TILE
    return pltpu.make_async_copy(src.at[pl.ds(pl.multiple_of(src_tok * ROW_TILE, ROW_TILE), n)],
                                 dst.at[pl.ds(pl.multiple_of(dst_tok * ROW_TILE, ROW_TILE), n)], sem)


def _expert_kernel(texp_ref, nval_ref, tok_ref, dst_ref, a8_hbm, w1_ref, w3_ref, w2_ref, y8_hbm,
                   xbuf, ybuf, gsem, ssem):
    i = pl.program_id(0)
    nv = nval_ref[i]
    slot = i % 2

    def start_gather(tile, s):
        for r in range(MOE_TM):
            _token_copy(a8_hbm, tok_ref[tile * MOE_TM + r], xbuf.at[s], r, gsem.at[s]).start()

    def wait_gather(s):
        _token_copy(a8_hbm, 0, xbuf.at[s], 0, gsem.at[s], MOE_TM).wait()

    def start_scatter(s):
        for r in range(MOE_TM):
            _token_copy(ybuf.at[s], r, y8_hbm, dst_ref[i * MOE_TM + r], ssem.at[s]).start()

    def wait_scatter(s):
        _token_copy(ybuf.at[s], 0, y8_hbm, 0, ssem.at[s], MOE_TM).wait()

    @pl.when(nv > 0)
    def _():
        @pl.when(i == 0)
        def _():
            start_gather(0, 0)
            ybuf[...] = jnp.zeros_like(ybuf)
            n_slots = y8_hbm.shape[0] // ROW_TILE
            fill = _token_copy(ybuf.at[0], 0, y8_hbm, n_slots - MOE_TM, ssem.at[0], MOE_TM)
            fill.start()
            fill.wait()

        wait_gather(slot)
        x = _from_token_tiles(xbuf.at[slot], MOE_TM).astype(BF16)
        start_gather(i + 1, 1 - slot)
        start_scatter(1 - slot)
        acc = jnp.zeros((MOE_TM, D), F32)
        for c in range(D_FF // FF_CHUNK):
            sl = slice(FF_CHUNK * c, FF_CHUNK * (c + 1))
            hid = _silu(jnp.dot(x, w1_ref[0, :, sl], preferred_element_type=F32)) * jnp.dot(
                x, w3_ref[0, :, sl], preferred_element_type=F32)
            acc = acc + jnp.dot(hid.astype(BF16), w2_ref[0, sl, :], preferred_element_type=F32)

        @pl.when(i > 0)
        def _():
            wait_scatter(slot)

        _to_token_tiles(ybuf.at[slot], acc)

    @pl.when((nv == 0) & (i > 0) & (nval_ref[jnp.maximum(i - 1, 0)] > 0))
    def _():
        wait_gather(slot)
        start_scatter(1 - slot)
        wait_scatter(slot)
        wait_scatter(1 - slot)


def _experts(tile_expert, n_valid, tok_of_row, dst_of_row, a8, w1, w3, w2, rows):
    n_tiles = tile_expert.shape[0]
    wspec = lambda shape: pl.BlockSpec(shape, lambda i, te, nv, tk, ds: (te[i], 0, 0),
                                       pipeline_mode=pl.Buffered(1))
    buf = pltpu.VMEM((2, MOE_TM * ROW_TILE, 128), F32)
    return pl.pallas_call(
        _expert_kernel,
        grid_spec=pltpu.PrefetchScalarGridSpec(
            num_scalar_prefetch=4,
            grid=(n_tiles,),
            in_specs=[pl.BlockSpec(memory_space=pl.ANY),
                      wspec((1, D, D_FF)), wspec((1, D, D_FF)), wspec((1, D_FF, D))],
            out_specs=pl.BlockSpec(memory_space=pl.ANY),
            scratch_shapes=[buf, buf, pltpu.SemaphoreType.DMA((2,)), pltpu.SemaphoreType.DMA((2,))]),
        out_shape=jax.ShapeDtypeStruct(((2 * rows + 2 * MOE_TM) * ROW_TILE, 128), F32),
        compiler_params=_cparams(("arbitrary",)),
        name="moe_experts",
    )(tile_expert, n_valid, tok_of_row, dst_of_row, a8, w1, w3, w2)


def _combine_kernel(h_ref, mod_ref, gate_ref, y0_ref, y1_ref, gf_ref, o_ref):
    tm = h_ref.shape[1]
    gates = gate_ref[...]
    y = gates[:, 0:1] * _from_token_tiles(y0_ref, tm) + gates[:, 1:2] * _from_token_tiles(y1_ref, tm)
    out = h_ref[0] + mod_ref[0, :, 5 * D:6 * D] * y
    out = out * lax.rsqrt(jnp.mean(out * out, axis=-1, keepdims=True) + NORM_EPS)
    o_ref[0] = out * gf_ref[...]


def _combine(h3, mods_l, gates, y8, gf):
    nb, seq, _ = h3.shape
    tm = 512
    ni = seq // tm
    n_tok_tiles = nb * ni
    return pl.pallas_call(
        _combine_kernel,
        grid=(nb, ni),
        in_specs=[pl.BlockSpec((1, tm, D), lambda b, i: (b, i, 0)),
                  pl.BlockSpec((1, 1, 6 * D), lambda b, i: (b, 0, 0)),
                  pl.BlockSpec((tm, 128), lambda b, i: (b * ni + i, 0)),
                  pl.BlockSpec((tm * ROW_TILE, 128), lambda b, i: (b * ni + i, 0)),
                  pl.BlockSpec((tm * ROW_TILE, 128), lambda b, i: (n_tok_tiles + b * ni + i, 0)),
                  pl.BlockSpec((1, D), lambda b, i: (0, 0))],
        out_specs=pl.BlockSpec((1, tm, D), lambda b, i: (b, i, 0)),
        out_shape=jax.ShapeDtypeStruct((nb, seq, D), F32),
        compiler_params=_cparams(("arbitrary", "arbitrary")),
        name="moe_combine_norm",
    )(h3, mods_l, gates, y8, y8, gf)


def _moe(h3, mods_l, g, rw, rb, w1, w3, w2, gf):
    nb, seq, _ = h3.shape
    rows = nb * seq
    a8, idx, gates = _router(h3, mods_l, g, rw, rb)
    tile_expert, n_valid, tok_of_row, dst_of_row = _route_plan(idx, rows)
    y8 = _experts(tile_expert, n_valid, tok_of_row, dst_of_row, a8, w1, w3, w2, rows)
    return _combine(h3, mods_l, gates, y8, gf)


def _pad_cols(a, n):
    return jnp.pad(a, [(0, 0)] * (a.ndim - 1) + [(0, n - a.shape[-1])])


def _mix_in_layout(w_in, rw_mu, conv_w, conv_b):
    w_rw = _pad_cols(w_in[:, :RW_COLS], RW_PAD)
    mb = w_in[:, RW_COLS:]
    w_mb = _pad_cols(jnp.concatenate([mb[:, MB_DIM:MB_DIM + MB_XBC], mb[:, :MB_DIM], mb[:, MB_DIM + MB_XBC:]],
                                     axis=1), MB_PAD)
    w = jnp.concatenate([w_rw, w_mb], axis=1).astype(BF16)
    mu_p = _pad_cols(rw_mu[0], RW_PAD)
    mu_n = _pad_cols(rw_mu[1], RW_PAD)
    zeros = jnp.zeros((RW_PAD,), F32)
    cf_rw = jnp.stack([zeros, mu_p, 1.0 - mu_p - mu_n, mu_n, zeros, zeros, zeros, zeros])
    ident = jnp.zeros((8, MB_PAD - MB_XBC), F32).at[2].set(1.0)
    cf_xbc = jnp.concatenate([conv_w, conv_b[None], jnp.zeros((2, MB_XBC), F32)], axis=0)
    cf = jnp.concatenate([cf_rw, cf_xbc, ident], axis=1)
    return w, cf


def _lora_pad(up, offset):
    out = jnp.zeros((2, 128, up.shape[-1]), F32)
    for d in range(2):
        out = out.at[d, offset + RW_LORA * d:offset + RW_LORA * (d + 1)].set(up[d])
    return out


def kernel(x, c, ctx, c_ctx, w_ada, b_ada, norm_mix, norm_ffn, norm_final, mix_w_in, mix_w_out, rw_mu, rw_w0,
           rw_w_up, rw_a0, rw_a_up, rw_g_up, rw_k_k, rw_k_a, rw_r_k, rw_gn_w, rw_gn_b, mb_conv_w, mb_conv_b,
           mb_dt_bias, mb_a_log, mb_d, mb_norm_w, ffn_w1, ffn_w3, ffn_w2, na_w_qkv, na_w_out, na_rpb,
           moe_router_w, moe_router_b, moe_w1, moe_w3, moe_w2):
    nb, seq, _ = x.shape
    ctx_len = ctx.shape[1]
    t = seq + ctx_len
    nt, nl = t // TM, seq // TM
    ntb = t // TMB
    n_lat_last = seq - TMB * (ntb - 1)
    assert seq % TM == 0 and ctx_len % TM == 0 and seq % ctx_len == 0 and nb < 16
    assert t % TMB == 0 and 0 < n_lat_last <= TMB

    cvec = jnp.zeros((16, D), F32).at[:nb].set(c).at[nb].set(c_ctx)
    mods = _ada(cvec, w_ada, b_ada).reshape(w_ada.shape[0], 16, 1, 6 * D)

    h = jnp.concatenate([x, ctx], axis=1).reshape(nb * t, D)

    w_in, cf = _mix_in_layout(mix_w_in[0], rw_mu[0], mb_conv_w[0], mb_conv_b[0])
    z = _inproj(h, mods[0], norm_mix[0][None], w_in, cf, nb, nt, nl, RW_PAD // TN, (RW_PAD + MB_XBC) // TN)
    z3 = z.reshape(nb, t, MIX_PAD)
    wlw = _lora_pad(rw_w_up[0], 0)
    wla = _lora_pad(rw_a_up[0], 2 * RW_LORA)
    w0 = rw_w0[0][:, None, :]
    a0 = rw_a0[0][:, None, :]
    yrw = _rwkv_scan(z3, w0, wlw, a0, wla, rw_k_k[0][None], rw_k_a[0][None], seq)
    dtb = _pad_cols(mb_dt_bias[0].reshape(1, 2 * MB_HEADS), 128)
    alog = _pad_cols(mb_a_log[0].reshape(1, 2 * MB_HEADS), 128)
    ymb = _ssd_scan(z3, dtb, alog, seq)
    gup = jnp.pad(rw_g_up[0], ((0, 128 - RW_LORA_G), (0, 0))).astype(BF16)
    yrw = [y.reshape(nb * t, RW_DIM) for y in yrw]
    ymb = [y.reshape(nb * t, MB_DIM) for y in ymb]
    h = _finish(z, yrw, ymb, h, mods[0], a0, wla,
                rw_k_a[0][None], rw_r_k[0].reshape(2, 1, RW_DIM), gup, rw_gn_w[0][None], rw_gn_b[0][None],
                jnp.repeat(mb_d[0], HEAD)[None], mb_norm_w[0][None], mix_w_out[0].astype(BF16), nb, nt, nl)
    h = _ffn(h, mods[0], norm_ffn[0][None], ffn_w1[0].astype(BF16), ffn_w3[0].astype(BF16),
             ffn_w2[0].astype(BF16), nb, ntb, n_lat_last)

    qkv = _nmm(h, mods[1], norm_mix[1][None], na_w_qkv[0].astype(BF16), nb, ntb, n_lat_last, BF16)
    o = _attention(qkv.reshape(nb, t, 3 * D), _attn_bias_table(na_rpb[0]), seq, ctx_len)
    mods1 = mods[1].reshape(16, 1, 6 * D)
    h3 = _proj_res(o, h.reshape(nb, t, D), mods1, na_w_out[0].astype(BF16), seq)
    rw = _pad_cols(moe_router_w[0], 128)
    rb = _pad_cols(moe_router_b[0][None], 128)
    return _moe(h3, mods1, norm_ffn[1][None], rw, rb, moe_w1[0].astype(BF16), moe_w3[0].astype(BF16),
                moe_w2[0].astype(BF16), norm_final[None])
```
